```python
import math
import jax, jax.numpy as jnp
from jax import lax
import numpy as np

D_MODEL = 1024
BATCH = 8
SEQ = 2048
DEPTH = 2
DEC_BATCH = 128
DEC_SEQ = 4
PAST_LEN = 16384
PAGE_SIZE = 128

N_MIXERS = 2
N_GLA_LAYERS = (DEPTH + 1) // 2
N_S5_LAYERS = DEPTH // 2
N_SANDWICH_NORMS = 6
D_FF = ((8 * D_MODEL // 3 + 127) // 128) * 128
GLA_HEADS = 4
GLA_DK = D_MODEL // 2 // GLA_HEADS
GLA_DV = D_MODEL // GLA_HEADS
GLA_KEY_DIM = GLA_HEADS * GLA_DK
GLA_VAL_DIM = GLA_HEADS * GLA_DV
GATE_RANK = 16
GATE_TAU = 16.0
GLA_CHUNK = 64
S5_GROUP = 16
S5_GROUPS = D_MODEL // S5_GROUP
S5_STATE = 64
DT_MIN = 1e-3
DT_MAX = 1e-1
EPS = 1e-6

kernel_name = 'hybrid_gla_s5_macaron_step'


def rmsnorm(x, g):
    xf = x.astype(jnp.float32)
    y = xf * lax.rsqrt(jnp.mean(xf * xf, axis=-1, keepdims=True) + EPS)
    return (y * g.astype(jnp.float32)).astype(x.dtype)


def swiglu(x, w_gu, w_down):
    gate, up = jnp.split(x @ w_gu, 2, axis=-1)
    return (jax.nn.silu(gate) * up) @ w_down


def gla_recurrence(q, k, v, logf, s0):
    B, T, H = q.shape[:3]
    c = min(GLA_CHUNK, T)
    n = -(-T // c)
    pad = n * c - T
    f32 = jnp.float32
    q, k, v, logf = (a.astype(f32) for a in (q, k, v, logf))
    if pad:
        pw = ((0, 0), (0, pad), (0, 0), (0, 0))
        q, k, v, logf = (jnp.pad(a, pw) for a in (q, k, v, logf))

    def to_chunks(a):
        return a.reshape(B, n, c, H, a.shape[-1]).transpose(1, 0, 3, 2, 4)

    causal = jnp.tril(jnp.ones((c, c), dtype=bool))

    def step(S, inp):
        qc, kc, vc, gc = inp
        b = jnp.cumsum(gc, axis=2)
        diff = b[:, :, :, None, :] - b[:, :, None, :, :]
        decay = jnp.exp(jnp.where(causal[:, :, None], diff, -jnp.inf))
        att = jnp.einsum('bhtd,bhsd,bhtsd->bhts', qc, kc, decay)
        o = (jnp.einsum('bhts,bhsv->bhtv', att, vc)
             + jnp.einsum('bhtd,bhdv->bhtv', qc * jnp.exp(b), S))
        b_end = b[:, :, -1:, :]
        S_new = (jnp.exp(b_end[:, :, 0, :])[..., None] * S
                 + jnp.einsum('bhsd,bhsv->bhdv', kc * jnp.exp(b_end - b), vc))
        return S_new, o

    S, o = lax.scan(step, s0.astype(f32), (to_chunks(q), to_chunks(k), to_chunks(v), to_chunks(logf)))
    o = o.transpose(1, 0, 3, 2, 4).reshape(B, n * c, H, v.shape[-1])[:, :T]
    return o, S.astype(s0.dtype)


def gla_mixer(x, s0, w_in, w_g2, b_g, g_onorm, w_out):
    B, T, _ = x.shape
    proj = x @ w_in
    q, k, v, r, glr = jnp.split(
        proj, [GLA_KEY_DIM, 2 * GLA_KEY_DIM, 2 * GLA_KEY_DIM + GLA_VAL_DIM,
               2 * GLA_KEY_DIM + 2 * GLA_VAL_DIM], axis=-1)
    logf = jax.nn.log_sigmoid((glr @ w_g2 + b_g).astype(jnp.float32)) / GATE_TAU
    q = q.reshape(B, T, GLA_HEADS, GLA_DK) * (GLA_DK ** -0.5)
    k = k.reshape(B, T, GLA_HEADS, GLA_DK)
    v = v.reshape(B, T, GLA_HEADS, GLA_DV)
    logf = logf.reshape(B, T, GLA_HEADS, GLA_DK)
    o, S = gla_recurrence(q, k, v, logf, s0)
    o = rmsnorm(o.astype(x.dtype), g_onorm).reshape(B, T, GLA_VAL_DIM)
    o = o * jax.nn.silu(r)
    return o @ w_out, S


def s5_discretize(lam_re, lam_im, log_dt, b_re, b_im):
    f32 = jnp.float32
    lam_re, lam_im, b_re, b_im = (a.astype(f32) for a in (lam_re, lam_im, b_re, b_im))
    dt = jnp.exp(log_dt.astype(f32))[:, None]
    mag = jnp.exp(lam_re * dt)
    ang = lam_im * dt
    a_re, a_im = mag * jnp.cos(ang), mag * jnp.sin(ang)
    nr, ni = a_re - 1.0, a_im
    den = lam_re * lam_re + lam_im * lam_im
    f_re = (nr * lam_re + ni * lam_im) / den
    f_im = (ni * lam_re - nr * lam_im) / den
    bb_re = f_re[..., None] * b_re - f_im[..., None] * b_im
    bb_im = f_re[..., None] * b_im + f_im[..., None] * b_re
    return a_re, a_im, bb_re, bb_im


def s5_mixer(x, h0_re, h0_im, lam_re, lam_im, log_dt, b_re, b_im, c_re, c_im, d_skip, w_glu, b_glu):
    B, T, _ = x.shape
    f32 = jnp.float32
    xf = x.astype(f32)
    u = xf.reshape(B, T, S5_GROUPS, S5_GROUP)
    a_re, a_im, bb_re, bb_im = s5_discretize(lam_re, lam_im, log_dt, b_re, b_im)
    bu_re = jnp.einsum('btgc,gpc->btgp', u, bb_re)
    bu_im = jnp.einsum('btgc,gpc->btgp', u, bb_im)
    hr, hi = h0_re.astype(f32), h0_im.astype(f32)
    bu_re = bu_re.at[:, 0].add(a_re * hr - a_im * hi)
    bu_im = bu_im.at[:, 0].add(a_re * hi + a_im * hr)
    A_re = jnp.broadcast_to(a_re[None, None], (1, T, S5_GROUPS, S5_STATE))
    A_im = jnp.broadcast_to(a_im[None, None], (1, T, S5_GROUPS, S5_STATE))

    def combine(e1, e2):
        ar1, ai1, br1, bi1 = e1
        ar2, ai2, br2, bi2 = e2
        return (ar1 * ar2 - ai1 * ai2,
                ar1 * ai2 + ai1 * ar2,
                ar2 * br1 - ai2 * bi1 + br2,
                ar2 * bi1 + ai2 * br1 + bi2)

    _, _, h_re, h_im = lax.associative_scan(combine, (A_re, A_im, bu_re, bu_im), axis=1)
    y = (jnp.einsum('btgp,gcp->btgc', h_re, c_re.astype(f32))
         - jnp.einsum('btgp,gcp->btgc', h_im, c_im.astype(f32)))
    z = (y.reshape(B, T, D_MODEL) + d_skip.astype(f32) * xf).astype(x.dtype)
    val, gate = jnp.split(z @ w_glu + b_glu, 2, axis=-1)
    out = val * jax.nn.sigmoid(gate)
    return out, h_re[:, -1].astype(h0_re.dtype), h_im[:, -1].astype(h0_im.dtype)


def setup_inputs(seed: int = 0) -> dict:
    key = jax.random.key(seed)
    ks = jax.random.split(key, 24)
    f32 = jnp.float32

    def nrm(k, shape, scale):
        return scale * jax.random.normal(k, shape, f32)

    s5_shape = (N_S5_LAYERS, S5_GROUPS, S5_STATE)
    n_idx = jnp.arange(S5_STATE, dtype=f32)
    return {
        'x_prompt': nrm(ks[0], (BATCH, SEQ, D_MODEL), 1.0),
        'x_sample': nrm(ks[1], (DEC_BATCH, DEC_SEQ, D_MODEL), 1.0),
        'state_gla': nrm(ks[2], (N_GLA_LAYERS, DEC_BATCH, GLA_HEADS, GLA_DK, GLA_DV), 0.5),
        'state_s5_re': nrm(ks[3], (N_S5_LAYERS, DEC_BATCH, S5_GROUPS, S5_STATE), 0.1),
        'state_s5_im': nrm(ks[4], (N_S5_LAYERS, DEC_BATCH, S5_GROUPS, S5_STATE), 0.1),
        'norm_g': 1.0 + nrm(ks[5], (DEPTH, N_SANDWICH_NORMS, D_MODEL), 0.05),
        'w_ffn_gu': nrm(ks[6], (DEPTH, 2, D_MODEL, 2 * D_FF), D_MODEL ** -0.5),
        'w_ffn_down': nrm(ks[7], (DEPTH, 2, D_FF, D_MODEL), D_FF ** -0.5),
        'gla_w_in': nrm(ks[8], (N_GLA_LAYERS, D_MODEL, 2 * GLA_KEY_DIM + 2 * GLA_VAL_DIM + GATE_RANK), D_MODEL ** -0.5),
        'gla_w_g2': nrm(ks[9], (N_GLA_LAYERS, GATE_RANK, GLA_KEY_DIM), GATE_RANK ** -0.5),
        'gla_b_g': nrm(ks[10], (N_GLA_LAYERS, GLA_KEY_DIM), 0.1),
        'gla_g_onorm': 1.0 + nrm(ks[11], (N_GLA_LAYERS, GLA_DV), 0.05),
        'gla_w_out': nrm(ks[12], (N_GLA_LAYERS, GLA_VAL_DIM, D_MODEL), GLA_VAL_DIM ** -0.5),
        's5_lam_re': -0.5 * jnp.exp(nrm(ks[13], s5_shape, 0.02)),
        's5_lam_im': math.pi * n_idx + nrm(ks[14], s5_shape, 0.01),
        's5_log_dt': jax.random.uniform(ks[15], (N_S5_LAYERS, S5_GROUPS), f32, math.log(DT_MIN), math.log(DT_MAX)),
        's5_b_re': nrm(ks[16], (N_S5_LAYERS, S5_GROUPS, S5_STATE, S5_GROUP), (2 * S5_GROUP) ** -0.5),
        's5_b_im': nrm(ks[17], (N_S5_LAYERS, S5_GROUPS, S5_STATE, S5_GROUP), (2 * S5_GROUP) ** -0.5),
        's5_c_re': nrm(ks[18], (N_S5_LAYERS, S5_GROUPS, S5_GROUP, S5_STATE), S5_STATE ** -0.5),
        's5_c_im': nrm(ks[19], (N_S5_LAYERS, S5_GROUPS, S5_GROUP, S5_STATE), S5_STATE ** -0.5),
        's5_d': nrm(ks[20], (N_S5_LAYERS, D_MODEL), 1.0),
        's5_w_glu': nrm(ks[21], (N_S5_LAYERS, D_MODEL, 2 * D_MODEL), D_MODEL ** -0.5),
        's5_b_glu': nrm(ks[22], (N_S5_LAYERS, 2 * D_MODEL), 0.01),
    }


def reference(x_prompt, x_sample, state_gla, state_s5_re, state_s5_im, norm_g, w_ffn_gu, w_ffn_down,
              gla_w_in, gla_w_g2, gla_b_g, gla_g_onorm, gla_w_out,
              s5_lam_re, s5_lam_im, s5_log_dt, s5_b_re, s5_b_im, s5_c_re, s5_c_im, s5_d, s5_w_glu, s5_b_glu):

    def run(x, gla_s0, s5_h0_re, s5_h0_im):
        new_gla, new_re, new_im = [], [], []
        for i in range(DEPTH):
            g = norm_g[i]
            x = x + 0.5 * rmsnorm(swiglu(rmsnorm(x, g[0]), w_ffn_gu[i, 0], w_ffn_down[i, 0]), g[1])
            h = rmsnorm(x, g[2])
            j = i // N_MIXERS
            if i % N_MIXERS == 0:
                m, s = gla_mixer(h, gla_s0[j], gla_w_in[j], gla_w_g2[j], gla_b_g[j], gla_g_onorm[j], gla_w_out[j])
                new_gla.append(s)
            else:
                m, hr, hi = s5_mixer(h, s5_h0_re[j], s5_h0_im[j], s5_lam_re[j], s5_lam_im[j], s5_log_dt[j],
                                     s5_b_re[j], s5_b_im[j], s5_c_re[j], s5_c_im[j], s5_d[j],
                                     s5_w_glu[j], s5_b_glu[j])
                new_re.append(hr)
                new_im.append(hi)
            x = x + rmsnorm(m, g[3])
            x = x + 0.5 * rmsnorm(swiglu(rmsnorm(x, g[4]), w_ffn_gu[i, 1], w_ffn_down[i, 1]), g[5])
        return x, jnp.stack(new_gla), jnp.stack(new_re), jnp.stack(new_im)

    zero_gla = jnp.zeros((N_GLA_LAYERS, x_prompt.shape[0], GLA_HEADS, GLA_DK, GLA_DV), x_prompt.dtype)
    zero_s5 = jnp.zeros((N_S5_LAYERS, x_prompt.shape[0], S5_GROUPS, S5_STATE), x_prompt.dtype)
    y_prompt, gla_p, s5re_p, s5im_p = run(x_prompt, zero_gla, zero_s5, zero_s5)
    y_sample, gla_s, s5re_s, s5im_s = run(x_sample, state_gla, state_s5_re, state_s5_im)
    return (y_prompt, y_sample, gla_p, s5re_p, s5im_p, gla_s, s5re_s, s5im_s)
```

```python
import functools
import math

import jax
import jax.numpy as jnp
import numpy as np
from jax import lax
from jax.experimental import pallas as pl
from jax.experimental.pallas import tpu as pltpu

F32 = jnp.float32
BF16 = jnp.bfloat16

D_MODEL = 1024
D_FF = 2816
GLA_HEADS = 4
GLA_DK = 128
GLA_DV = 256
GLA_KEY_DIM = GLA_HEADS * GLA_DK
GLA_VAL_DIM = GLA_HEADS * GLA_DV
GATE_RANK = 16
GATE_TAU = 16.0
S5_GROUP = 16
S5_GROUPS = 64
S5_STATE = 64
S5_FLAT = S5_GROUPS * S5_STATE
S5_SUPER = 8
S5_NSUPER = S5_GROUPS // S5_SUPER
S5_SUPER_CH = S5_SUPER * S5_GROUP
S5_SUPER_ST = S5_SUPER * S5_STATE
EPS = 1e-6
LANE = 128
SUBLANE = 8
VMEM_LIMIT = 56 * 1024 * 1024

FFN_ROWS = 512
FFN_CHUNK = 256
GLA_CHUNK = 64
GLA_TBLOCK = 256
GLA_SAMPLE_SEQS = 8
S5_LANES = 512


def _dot(a, b):
    return jnp.dot(a, b, preferred_element_type=F32)


def _dot_nt(a, b):
    return lax.dot_general(a, b, (((1,), (1,)), ((), ())), preferred_element_type=F32)


def _rms(x, g):
    ms = jnp.mean(x * x, axis=-1, keepdims=True)
    return x * lax.rsqrt(ms + EPS) * g


def _sigmoid(x):
    return 1.0 / (1.0 + jnp.exp(-x))


def _split_bf16(x):
    hi = x.astype(BF16)
    lo = (x - hi.astype(F32)).astype(BF16)
    return hi, lo


def _ffn_kernel(x_ref, g_ref, wgu_ref, wd_ref, o_ref, act_ref):
    x = x_ref[...]
    xn = _rms(x, g_ref[0:1, :]).astype(BF16)
    for c in range(D_FF // FFN_CHUNK):
        lo = c * FFN_CHUNK
        gate = _dot(xn, wgu_ref[:, lo:lo + FFN_CHUNK])
        up = _dot(xn, wgu_ref[:, D_FF + lo:D_FF + lo + FFN_CHUNK])
        act_ref[:, lo:lo + FFN_CHUNK] = (gate * _sigmoid(gate) * up).astype(BF16)
    y = _dot(act_ref[...], wd_ref[...])
    o_ref[...] = x + 0.5 * _rms(y, g_ref[1:2, :])


def _ffn(x2d, g2, wgu, wd):
    n = x2d.shape[0]
    tm = min(FFN_ROWS, n)
    assert n % tm == 0
    resident = dict(pipeline_mode=pl.Buffered(1))
    return pl.pallas_call(
        _ffn_kernel,
        grid=(n // tm,),
        in_specs=[
            pl.BlockSpec((tm, D_MODEL), lambda i: (i, 0)),
            pl.BlockSpec((2, D_MODEL), lambda i: (0, 0)),
            pl.BlockSpec((D_MODEL, 2 * D_FF), lambda i: (0, 0), **resident),
            pl.BlockSpec((D_FF, D_MODEL), lambda i: (0, 0), **resident),
        ],
        out_specs=pl.BlockSpec((tm, D_MODEL), lambda i: (i, 0)),
        out_shape=jax.ShapeDtypeStruct((n, D_MODEL), F32),
        scratch_shapes=[pltpu.VMEM((tm, D_FF), BF16)],
        compiler_params=pltpu.CompilerParams(
            dimension_semantics=("arbitrary",), vmem_limit_bytes=VMEM_LIMIT),
        name="ffn",
    )(x2d, g2, wgu, wd)


def _gla_tables(rows, seq):
    t = np.arange(rows)[:, None]
    r = np.arange(rows)[None, :]
    same = (t // seq) == (r // seq)
    incl = same & (r <= t)
    mats = [incl, same & (r > t)]
    masks = [t == r]
    m = seq // 2
    while m >= 1:
        mid = (t // (2 * m)) * (2 * m) + m
        mats.append(incl.astype(np.int32) - (same & (r <= mid - 1)).astype(np.int32))
        masks.append(((t // (2 * m)) == (r // (2 * m))) & (t % (2 * m) >= m) & (r % (2 * m) < m))
        m //= 2
    stack = np.concatenate([np.asarray(a, np.float32) for a in mats], axis=0)
    return jnp.asarray(stack, BF16), jnp.asarray(np.stack(masks).astype(np.float32))


def _gla_project(x, g_ref, win_ref, wglr_ref, wg2_ref, bg_ref, proj_ref, lf_ref):
    hn = _rms(x, g_ref[0:1, :]).astype(BF16)
    proj_ref[...] = _dot(hn, win_ref[...])
    glr = _dot(hn, wglr_ref[...]).astype(BF16)
    gate = _dot(glr, wg2_ref[...]) + bg_ref[...]
    lf_ref[...] = (jnp.minimum(gate, 0.0) - jnp.log1p(jnp.exp(-jnp.abs(gate)))) * (1.0 / GATE_TAU)


def _gla_decay_table(lf, stack_ref, eb_ref):
    hi, lo = _split_bf16(lf)
    eb_ref[...] = _dot(stack_ref[...], hi) + _dot(stack_ref[...], lo)


def _gla_attention(qs, k, eb_ref, mask_ref, rows, col):
    att = _dot_nt(qs.astype(BF16), k.astype(BF16)) * mask_ref[0]
    for lvl in range(mask_ref.shape[0] - 1):
        e = eb_ref[(2 + lvl) * rows:(3 + lvl) * rows, col]
        w = jnp.exp(-jnp.abs(e))
        att = att + _dot_nt((qs * w).astype(BF16), (k * w).astype(BF16)) * mask_ref[1 + lvl]
    return att


def _gla_finish_head(o, r, gon_ref):
    on = _rms(o, gon_ref[...])
    return (on * (r * _sigmoid(r))).astype(BF16)


def _gla_prompt_kernel(x_ref, g_ref, win_ref, wglr_ref, wg2_ref, bg_ref, gon_ref, wout_ref,
                       stack_ref, mask_ref, o_ref, snew_ref,
                       proj_ref, lf_ref, eb_ref, og_ref, s_ref):
    tb = pl.program_id(1)
    c = GLA_CHUNK

    @pl.when(tb == 0)
    def _():
        s_ref[...] = jnp.zeros_like(s_ref)

    x = x_ref[0]
    _gla_project(x, g_ref, win_ref, wglr_ref, wg2_ref, bg_ref, proj_ref, lf_ref)

    def chunk(ci, carry):
        r0 = pl.multiple_of(ci * c, c)
        rows = pl.ds(r0, c)
        _gla_decay_table(lf_ref[rows, :], stack_ref, eb_ref)
        for h in range(GLA_HEADS):
            kcol = slice(h * GLA_DK, (h + 1) * GLA_DK)
            vcol = slice(h * GLA_DV, (h + 1) * GLA_DV)
            qs = proj_ref[rows, kcol] * (GLA_DK ** -0.5)
            k = proj_ref[rows, GLA_KEY_DIM + h * GLA_DK:GLA_KEY_DIM + (h + 1) * GLA_DK]
            v = proj_ref[rows, 2 * GLA_KEY_DIM + h * GLA_DV:2 * GLA_KEY_DIM + (h + 1) * GLA_DV].astype(BF16)
            r = proj_ref[rows, 2 * GLA_KEY_DIM + GLA_VAL_DIM + h * GLA_DV:
                         2 * GLA_KEY_DIM + GLA_VAL_DIM + (h + 1) * GLA_DV]
            att = _gla_attention(qs, k, eb_ref, mask_ref, c, kcol)
            b = eb_ref[0:c, kcol]
            s = s_ref[h]
            o = _dot(att.astype(BF16), v) + _dot((qs * jnp.exp(b)).astype(BF16), s.astype(BF16))
            khat = k * jnp.exp(eb_ref[c:2 * c, kcol])
            tile = jnp.concatenate(
                [khat, jnp.exp(b[c - SUBLANE:c, :]), jnp.zeros((LANE - c - SUBLANE, GLA_DK), F32)], axis=0)
            tt = jnp.transpose(tile)
            decay = tt[:, c + SUBLANE - 1:c + SUBLANE]
            s_ref[h] = s * decay + _dot(tt[:, 0:c].astype(BF16), v)
            og_ref[rows, vcol] = _gla_finish_head(o, r, gon_ref)
        return carry

    lax.fori_loop(0, GLA_TBLOCK // c, chunk, 0)
    out = _dot(og_ref[...], wout_ref[...])
    o_ref[0] = x + _rms(out, g_ref[1:2, :])

    @pl.when(tb == pl.num_programs(1) - 1)
    def _():
        snew_ref[0] = s_ref[...]


def _gla_sample_kernel(x_ref, g_ref, win_ref, wglr_ref, wg2_ref, bg_ref, gon_ref, wout_ref,
                       stack_ref, mask_ref, s0_ref, o_ref, snew_ref,
                       proj_ref, lf_ref, eb_ref, og_ref, *, seq):
    rows = x_ref.shape[0]
    grp = 2 * SUBLANE
    x = x_ref[...]
    _gla_project(x, g_ref, win_ref, wglr_ref, wg2_ref, bg_ref, proj_ref, lf_ref)
    _gla_decay_table(lf_ref[...], stack_ref, eb_ref)
    rid = lax.broadcasted_iota(jnp.int32, (grp, 1), 0)
    for h in range(GLA_HEADS):
        kcol = slice(h * GLA_DK, (h + 1) * GLA_DK)
        vcol = slice(h * GLA_DV, (h + 1) * GLA_DV)
        qs = proj_ref[:, kcol] * (GLA_DK ** -0.5)
        k = proj_ref[:, GLA_KEY_DIM + h * GLA_DK:GLA_KEY_DIM + (h + 1) * GLA_DK]
        v = proj_ref[:, 2 * GLA_KEY_DIM + h * GLA_DV:2 * GLA_KEY_DIM + (h + 1) * GLA_DV]
        r = proj_ref[:, 2 * GLA_KEY_DIM + GLA_VAL_DIM + h * GLA_DV:
                     2 * GLA_KEY_DIM + GLA_VAL_DIM + (h + 1) * GLA_DV]
        att = _gla_attention(qs, k, eb_ref, mask_ref, rows, kcol)
        eb = jnp.exp(eb_ref[0:rows, kcol])
        qb = (qs * eb).astype(BF16)
        khat = k * jnp.exp(eb_ref[rows:2 * rows, kcol])
        o_intra = _dot(att.astype(BF16), v.astype(BF16))
        o_parts = []
        for gi in range(rows // grp):
            gr = slice(gi * grp, (gi + 1) * grp)
            tile = jnp.concatenate(
                [khat[gr], eb[gr], jnp.zeros((LANE - 2 * grp, GLA_DK), F32)], axis=0)
            tt = jnp.transpose(tile)
            kt = tt[:, 0:grp].astype(BF16)
            qg = qb[gr]
            vg = v[gr]
            o_g = jnp.zeros((grp, GLA_DV), F32)
            for si in range(grp // seq):
                sq = gi * (grp // seq) + si
                mine = (rid >= si * seq) & (rid < (si + 1) * seq)
                s = s0_ref[sq, h]
                o_g = o_g + jnp.where(mine, _dot(qg, s.astype(BF16)), 0.0)
                vm = jnp.where(mine, vg, 0.0).astype(BF16)
                last = grp + (si + 1) * seq - 1
                snew_ref[sq, h] = s * tt[:, last:last + 1] + _dot(kt, vm)
            o_parts.append(o_g)
        o = o_intra + jnp.concatenate(o_parts, axis=0)
        og_ref[:, vcol] = _gla_finish_head(o, r, gon_ref)
    out = _dot(og_ref[...], wout_ref[...])
    o_ref[...] = x + _rms(out, g_ref[1:2, :])


def _gla_weight_specs(nidx):
    zero = (lambda *_: (0, 0))
    resident = dict(pipeline_mode=pl.Buffered(1))
    del nidx
    return [
        pl.BlockSpec((2, D_MODEL), zero),
        pl.BlockSpec((D_MODEL, 2 * GLA_KEY_DIM + 2 * GLA_VAL_DIM), zero, **resident),
        pl.BlockSpec((D_MODEL, LANE), zero, **resident),
        pl.BlockSpec((LANE, GLA_KEY_DIM), zero, **resident),
        pl.BlockSpec((1, GLA_KEY_DIM), zero),
        pl.BlockSpec((1, GLA_DV), zero),
        pl.BlockSpec((GLA_VAL_DIM, D_MODEL), zero, **resident),
    ]


def _gla_weights(w_in, w_g2, b_g, g_onorm, w_out):
    n_main = 2 * GLA_KEY_DIM + 2 * GLA_VAL_DIM
    w_main = w_in[:, :n_main].astype(BF16)
    w_glr = jnp.pad(w_in[:, n_main:], ((0, 0), (0, LANE - GATE_RANK))).astype(BF16)
    w_g2p = jnp.pad(w_g2, ((0, LANE - GATE_RANK), (0, 0))).astype(BF16)
    return (w_main, w_glr, w_g2p, b_g.reshape(1, GLA_KEY_DIM), g_onorm.reshape(1, GLA_DV),
            w_out.astype(BF16))


def _gla_prompt(x, g2, weights):
    bsz, t, _ = x.shape
    stack, masks = _gla_tables(GLA_CHUNK, GLA_CHUNK)
    n_main = 2 * GLA_KEY_DIM + 2 * GLA_VAL_DIM
    const2 = lambda b, i: (0, 0)
    return pl.pallas_call(
        _gla_prompt_kernel,
        grid=(bsz, t // GLA_TBLOCK),
        in_specs=[pl.BlockSpec((1, GLA_TBLOCK, D_MODEL), lambda b, i: (b, i, 0))]
        + _gla_weight_specs(2)
        + [pl.BlockSpec(stack.shape, const2),
           pl.BlockSpec(masks.shape, lambda b, i: (0, 0, 0))],
        out_specs=[
            pl.BlockSpec((1, GLA_TBLOCK, D_MODEL), lambda b, i: (b, i, 0)),
            pl.BlockSpec((1, GLA_HEADS, GLA_DK, GLA_DV), lambda b, i: (b, 0, 0, 0)),
        ],
        out_shape=[
            jax.ShapeDtypeStruct(x.shape, F32),
            jax.ShapeDtypeStruct((bsz, GLA_HEADS, GLA_DK, GLA_DV), F32),
        ],
        scratch_shapes=[
            pltpu.VMEM((GLA_TBLOCK, n_main), F32),
            pltpu.VMEM((GLA_TBLOCK, GLA_KEY_DIM), F32),
            pltpu.VMEM((stack.shape[0], GLA_KEY_DIM), F32),
            pltpu.VMEM((GLA_TBLOCK, GLA_VAL_DIM), BF16),
            pltpu.VMEM((GLA_HEADS, GLA_DK, GLA_DV), F32),
        ],
        compiler_params=pltpu.CompilerParams(
            dimension_semantics=("arbitrary", "arbitrary"), vmem_limit_bytes=VMEM_LIMIT),
        name="gla_prompt",
    )(x, g2, *weights, stack, masks)


def _gla_sample(x, s0, g2, weights):
    bsz, seq, _ = x.shape
    rows = GLA_SAMPLE_SEQS * seq
    stack, masks = _gla_tables(rows, seq)
    n_main = 2 * GLA_KEY_DIM + 2 * GLA_VAL_DIM
    state_spec = pl.BlockSpec((GLA_SAMPLE_SEQS, GLA_HEADS, GLA_DK, GLA_DV), lambda i: (i, 0, 0, 0))
    y, snew = pl.pallas_call(
        functools.partial(_gla_sample_kernel, seq=seq),
        grid=(bsz // GLA_SAMPLE_SEQS,),
        in_specs=[pl.BlockSpec((rows, D_MODEL), lambda i: (i, 0))]
        + _gla_weight_specs(1)
        + [pl.BlockSpec(stack.shape, lambda i: (0, 0)),
           pl.BlockSpec(masks.shape, lambda i: (0, 0, 0)),
           state_spec],
        out_specs=[pl.BlockSpec((rows, D_MODEL), lambda i: (i, 0)), state_spec],
        out_shape=[
            jax.ShapeDtypeStruct((bsz * seq, D_MODEL), F32),
            jax.ShapeDtypeStruct(s0.shape, F32),
        ],
        scratch_shapes=[
            pltpu.VMEM((rows, n_main), F32),
            pltpu.VMEM((rows, GLA_KEY_DIM), F32),
            pltpu.VMEM((stack.shape[0], GLA_KEY_DIM), F32),
            pltpu.VMEM((rows, GLA_VAL_DIM), BF16),
        ],
        compiler_params=pltpu.CompilerParams(
            dimension_semantics=("arbitrary",), vmem_limit_bytes=VMEM_LIMIT),
        name="gla_sample",
    )(x.reshape(bsz * seq, D_MODEL), g2, *weights, stack, masks, s0)
    return y.reshape(x.shape), snew


def _s5_discretize(lam_re, lam_im, log_dt):
    dt = jnp.exp(log_dt)
    mag = jnp.exp(lam_re * dt)
    ang = lam_im * dt
    a_re, a_im = mag * jnp.cos(ang), mag * jnp.sin(ang)
    nr, ni = a_re - 1.0, a_im
    den = lam_re * lam_re + lam_im * lam_im
    f_re = (nr * lam_re + ni * lam_im) / den
    f_im = (ni * lam_re - nr * lam_im) / den
    return a_re, a_im, f_re, f_im


def _s5_prep_kernel(lre_ref, lim_ref, ldt_ref, bre_ref, bim_ref, cre_ref, cim_ref,
                    lref_ref, limf_ref, ldtf_ref, wb_ref, wc_ref, are_ref, aim_ref):
    _, _, f_re, f_im = _s5_discretize(lre_ref[...], lim_ref[...], ldt_ref[...])
    b_re, b_im = bre_ref[...], bim_ref[...]
    row = lax.broadcasted_iota(jnp.int32, (S5_SUPER_CH, S5_SUPER_ST), 0)
    col = lax.broadcasted_iota(jnp.int32, (S5_SUPER_CH, S5_SUPER_ST), 1)
    own = (row // S5_GROUP) == (col // S5_STATE)
    wb_ref[0, :, 0:S5_SUPER_ST] = jnp.where(own, f_re * b_re - f_im * b_im, 0.0).astype(BF16)
    wb_ref[0, :, S5_SUPER_ST:] = jnp.where(own, f_re * b_im + f_im * b_re, 0.0).astype(BF16)
    rowc = lax.broadcasted_iota(jnp.int32, (S5_SUPER_ST, S5_SUPER_CH), 0)
    colc = lax.broadcasted_iota(jnp.int32, (S5_SUPER_ST, S5_SUPER_CH), 1)
    ownc = (rowc // S5_STATE) == (colc // S5_GROUP)
    wc_ref[0, 0:S5_SUPER_ST, :] = jnp.where(ownc, cre_ref[...], 0.0).astype(BF16)
    wc_ref[0, S5_SUPER_ST:, :] = jnp.where(ownc, -cim_ref[...], 0.0).astype(BF16)
    a_re, a_im, _, _ = _s5_discretize(lref_ref[...], limf_ref[...], ldtf_ref[...])
    are_ref[...] = a_re
    aim_ref[...] = a_im


def _s5_prep(lam_re, lam_im, log_dt, b_re, b_im, c_re, c_im):
    g, p, ch, sg = S5_GROUPS, S5_STATE, S5_GROUP, S5_SUPER

    def rows_by_group(a_gp):
        return jnp.broadcast_to(a_gp[:, None, None, :], (g, ch, sg, p)).reshape(g * ch, sg * p)

    def b_layout(b):
        bt = jnp.transpose(b, (0, 2, 1))
        return jnp.broadcast_to(bt[:, :, None, :], (g, ch, sg, p)).reshape(g * ch, sg * p)

    def c_layout(cm):
        ct = jnp.transpose(cm.reshape(g // sg, sg, ch, p), (3, 0, 1, 2)).reshape(p, g * ch)
        return jnp.broadcast_to(ct[None], (sg, p, g * ch)).reshape(sg * p, g * ch)

    ldt_gp = jnp.broadcast_to(log_dt[:, None], (g, p))
    big = pl.BlockSpec((S5_SUPER_CH, S5_SUPER_ST), lambda j: (j, 0))
    cspec = pl.BlockSpec((S5_SUPER_ST, S5_SUPER_CH), lambda j: (0, j))
    flat = pl.BlockSpec((1, S5_SUPER_ST), lambda j: (0, j))
    return pl.pallas_call(
        _s5_prep_kernel,
        grid=(S5_NSUPER,),
        in_specs=[big] * 5 + [cspec] * 2 + [flat] * 3,
        out_specs=[
            pl.BlockSpec((1, S5_SUPER_CH, 2 * S5_SUPER_ST), lambda j: (j, 0, 0)),
            pl.BlockSpec((1, 2 * S5_SUPER_ST, S5_SUPER_CH), lambda j: (j, 0, 0)),
            flat, flat,
        ],
        out_shape=[
            jax.ShapeDtypeStruct((S5_NSUPER, S5_SUPER_CH, 2 * S5_SUPER_ST), BF16),
            jax.ShapeDtypeStruct((S5_NSUPER, 2 * S5_SUPER_ST, S5_SUPER_CH), BF16),
            jax.ShapeDtypeStruct((1, S5_FLAT), F32),
            jax.ShapeDtypeStruct((1, S5_FLAT), F32),
        ],
        compiler_params=pltpu.CompilerParams(dimension_semantics=("arbitrary",)),
        name="s5_prep",
    )(rows_by_group(lam_re), rows_by_group(lam_im), rows_by_group(ldt_gp),
      b_layout(b_re), b_layout(b_im), c_layout(c_re), c_layout(c_im),
      lam_re.reshape(1, S5_FLAT), lam_im.reshape(1, S5_FLAT), ldt_gp.reshape(1, S5_FLAT))


def _s5_kernel(x_ref, g_ref, wb_ref, wc_ref, are_ref, aim_ref, d_ref, wglu_ref, bglu_ref,
               h0re_ref, h0im_ref, o_ref, hre_ref, him_ref,
               u_ref, sre_ref, sim_ref, z_ref, *, bsz, steps):
    i = pl.program_id(0)

    @pl.when(i == 0)
    def _():
        hre_ref[...] = h0re_ref[...]
        him_ref[...] = h0im_ref[...]

    x = x_ref[...]
    u = _rms(x, g_ref[0:1, :])
    u_ref[...] = u
    ub = u.astype(BF16)
    for j in range(S5_NSUPER):
        bu = _dot(ub[:, j * S5_SUPER_CH:(j + 1) * S5_SUPER_CH], wb_ref[j])
        sre_ref[:, j * S5_SUPER_ST:(j + 1) * S5_SUPER_ST] = bu[:, :S5_SUPER_ST]
        sim_ref[:, j * S5_SUPER_ST:(j + 1) * S5_SUPER_ST] = bu[:, S5_SUPER_ST:]

    for lc in range(S5_FLAT // S5_LANES):
        ls = slice(lc * S5_LANES, (lc + 1) * S5_LANES)
        a_re = jnp.broadcast_to(are_ref[:, ls], (SUBLANE, S5_LANES))
        a_im = jnp.broadcast_to(aim_ref[:, ls], (SUBLANE, S5_LANES))
        for rt in range(bsz // SUBLANE):
            rs = slice(rt * SUBLANE, (rt + 1) * SUBLANE)

            def step(t, carry, ls=ls, rt=rt, a_re=a_re, a_im=a_im):
                h_re, h_im = carry
                rows = pl.ds(pl.multiple_of(t * bsz + rt * SUBLANE, SUBLANE), SUBLANE)
                n_re = a_re * h_re - a_im * h_im + sre_ref[rows, ls]
                n_im = a_re * h_im + a_im * h_re + sim_ref[rows, ls]
                sre_ref[rows, ls] = n_re
                sim_ref[rows, ls] = n_im
                return n_re, n_im

            h_re, h_im = lax.fori_loop(0, steps, step, (hre_ref[rs, ls], him_ref[rs, ls]),
                                       unroll=True)
            hre_ref[rs, ls] = h_re
            him_ref[rs, ls] = h_im

    for j in range(S5_NSUPER):
        st = slice(j * S5_SUPER_ST, (j + 1) * S5_SUPER_ST)
        ch = slice(j * S5_SUPER_CH, (j + 1) * S5_SUPER_CH)
        y = (_dot(sre_ref[:, st].astype(BF16), wc_ref[j, 0:S5_SUPER_ST, :])
             + _dot(sim_ref[:, st].astype(BF16), wc_ref[j, S5_SUPER_ST:, :]))
        z_ref[:, ch] = (y + d_ref[:, ch] * u_ref[:, ch]).astype(BF16)
    zz = _dot(z_ref[...], wglu_ref[...]) + bglu_ref[...]
    out = zz[:, :D_MODEL] * _sigmoid(zz[:, D_MODEL:])
    o_ref[...] = x + _rms(out, g_ref[1:2, :])


def _s5(x_tm, bsz, steps_per_block, g2, prep, d_skip, w_glu, b_glu, h0_re, h0_im):
    wb, wc, a_re, a_im = prep
    n = x_tm.shape[0]
    rows = steps_per_block * bsz
    zero2 = lambda i: (0, 0)
    zero3 = lambda i: (0, 0, 0)
    resident = dict(pipeline_mode=pl.Buffered(1))
    state = pl.BlockSpec((bsz, S5_FLAT), zero2)
    return pl.pallas_call(
        functools.partial(_s5_kernel, bsz=bsz, steps=steps_per_block),
        grid=(n // rows,),
        in_specs=[
            pl.BlockSpec((rows, D_MODEL), lambda i: (i, 0)),
            pl.BlockSpec((2, D_MODEL), zero2),
            pl.BlockSpec(wb.shape, zero3, **resident),
            pl.BlockSpec(wc.shape, zero3, **resident),
            pl.BlockSpec((1, S5_FLAT), zero2),
            pl.BlockSpec((1, S5_FLAT), zero2),
            pl.BlockSpec((1, D_MODEL), zero2),
            pl.BlockSpec((D_MODEL, 2 * D_MODEL), zero2, **resident),
            pl.BlockSpec((1, 2 * D_MODEL), zero2),
            state, state,
        ],
        out_specs=[pl.BlockSpec((rows, D_MODEL), lambda i: (i, 0)), state, state],
        out_shape=[
            jax.ShapeDtypeStruct((n, D_MODEL), F32),
            jax.ShapeDtypeStruct((bsz, S5_FLAT), F32),
            jax.ShapeDtypeStruct((bsz, S5_FLAT), F32),
        ],
        scratch_shapes=[
            pltpu.VMEM((rows, D_MODEL), F32),
            pltpu.VMEM((rows, S5_FLAT), F32),
            pltpu.VMEM((rows, S5_FLAT), F32),
            pltpu.VMEM((rows, D_MODEL), BF16),
        ],
        compiler_params=pltpu.CompilerParams(
            dimension_semantics=("arbitrary",), vmem_limit_bytes=VMEM_LIMIT),
        name="s5",
    )(x_tm, g2, wb, wc, a_re, a_im, d_skip.reshape(1, D_MODEL), w_glu.astype(BF16),
      b_glu.reshape(1, 2 * D_MODEL), h0_re, h0_im)


def kernel(x_prompt, x_sample, state_gla, state_s5_re, state_s5_im, norm_g, w_ffn_gu, w_ffn_down,
           gla_w_in, gla_w_g2, gla_b_g, gla_g_onorm, gla_w_out,
           s5_lam_re, s5_lam_im, s5_log_dt, s5_b_re, s5_b_im, s5_c_re, s5_c_im, s5_d, s5_w_glu,
           s5_b_glu):
    pb, pt, _ = x_prompt.shape
    sb, st, _ = x_sample.shape
    wgu = w_ffn_gu.astype(BF16)
    wd = w_ffn_down.astype(BF16)

    def ffn(x, layer, which):
        g2 = norm_g[layer, 4 * which:4 * which + 2]
        return _ffn(x.reshape(-1, D_MODEL), g2, wgu[layer, which], wd[layer, which]).reshape(x.shape)

    gla_w = _gla_weights(gla_w_in[0], gla_w_g2[0], gla_b_g[0], gla_g_onorm[0], gla_w_out[0])
    xp = ffn(x_prompt, 0, 0)
    xs = ffn(x_sample, 0, 0)
    xp, gla_p = _gla_prompt(xp, norm_g[0, 2:4], gla_w)
    xs, gla_s = _gla_sample(xs, state_gla[0], norm_g[0, 2:4], gla_w)
    xp = ffn(xp, 0, 1)
    xs = ffn(xs, 0, 1)

    xp = jnp.transpose(xp, (1, 0, 2)).reshape(pt * pb, D_MODEL)
    xs = jnp.transpose(xs, (1, 0, 2)).reshape(st * sb, D_MODEL)
    xp = ffn(xp, 1, 0)
    xs = ffn(xs, 1, 0)
    prep = _s5_prep(s5_lam_re[0], s5_lam_im[0], s5_log_dt[0], s5_b_re[0], s5_b_im[0],
                    s5_c_re[0], s5_c_im[0])
    zeros = jnp.zeros((pb, S5_FLAT), F32)
    s5_args = (norm_g[1, 2:4], prep, s5_d[0], s5_w_glu[0], s5_b_glu[0])
    xp, hre_p, him_p = _s5(xp, pb, 32, *s5_args, zeros, zeros)
    xs, hre_s, him_s = _s5(xs, sb, st, *s5_args,
                           state_s5_re[0].reshape(sb, S5_FLAT), state_s5_im[0].reshape(sb, S5_FLAT))
    xp = ffn(xp, 1, 1)
    xs = ffn(xs, 1, 1)
    y_prompt = jnp.transpose(xp.reshape(pt, pb, D_MODEL), (1, 0, 2))
    y_sample = jnp.transpose(xs.reshape(st, sb, D_MODEL), (1, 0, 2))

    def s5_state(h, b):
        return h.reshape(1, b, S5_GROUPS, S5_STATE)

    return (y_prompt, y_sample, gla_p[None], s5_state(hre_p, pb), s5_state(him_p, pb),
            gla_s[None], s5_state(hre_s, sb), s5_state(him_s, sb))
```

```python
import functools
import math

import jax
import jax.numpy as jnp
import numpy as np
from jax import lax
from jax.experimental import pallas as pl
from jax.experimental.pallas import tpu as pltpu

F32 = jnp.float32
BF16 = jnp.bfloat16

D_MODEL = 1024
D_FF = 2816
GLA_HEADS = 4
GLA_DK = 128
GLA_DV = 256
GLA_KEY_DIM = GLA_HEADS * GLA_DK
GLA_VAL_DIM = GLA_HEADS * GLA_DV
GATE_RANK = 16
GATE_TAU = 16.0
S5_GROUP = 16
S5_GROUPS = 64
S5_STATE = 64
S5_FLAT = S5_GROUPS * S5_STATE
S5_SUPER = 8
S5_NSUPER = S5_GROUPS // S5_SUPER
S5_SUPER_CH = S5_SUPER * S5_GROUP
S5_SUPER_ST = S5_SUPER * S5_STATE
EPS = 1e-6
LANE = 128
SUBLANE = 8
VMEM_LIMIT = 56 * 1024 * 1024

FFN_ROWS = 512
FFN_CHUNK = 256
GLA_CHUNK = 64
GLA_TBLOCK = 256
GLA_PROJ_CHUNK = 256
LOG2E = math.log2(math.e)
GLA_SAMPLE_SEQS = 8
S5_LANES = 512


def _dot(a, b):
    return jnp.dot(a, b, preferred_element_type=F32)


def _dot_nt(a, b):
    return lax.dot_general(a, b, (((1,), (1,)), ((), ())), preferred_element_type=F32)


def _rms(x, g):
    ms = jnp.mean(x * x, axis=-1, keepdims=True)
    return x * lax.rsqrt(ms + EPS) * g


def _sigmoid(x):
    return 1.0 / (1.0 + jnp.exp(-x))


def _split_bf16(x):
    hi = x.astype(BF16)
    lo = (x - hi.astype(F32)).astype(BF16)
    return hi, lo


def _ffn_kernel(x_ref, g_ref, wgu_ref, wd_ref, o_ref, act_ref):
    x = x_ref[...]
    xn = _rms(x, g_ref[0:1, :]).astype(BF16)
    for c in range(D_FF // FFN_CHUNK):
        lo = c * FFN_CHUNK
        gate = _dot(xn, wgu_ref[:, lo:lo + FFN_CHUNK])
        up = _dot(xn, wgu_ref[:, D_FF + lo:D_FF + lo + FFN_CHUNK])
        act_ref[:, lo:lo + FFN_CHUNK] = (gate * _sigmoid(gate) * up).astype(BF16)
    y = _dot(act_ref[...], wd_ref[...])
    o_ref[...] = x + 0.5 * _rms(y, g_ref[1:2, :])


def _row_layout(kind, bsz, t, tm):
    if kind == "rows":
        return (bsz * t, D_MODEL), pl.BlockSpec((tm, D_MODEL), lambda b, i: (i, 0)), (1, bsz * t // tm)
    if kind == "batch":
        return (bsz, t, D_MODEL), pl.BlockSpec((None, tm, D_MODEL), lambda b, i: (b, i, 0)), (bsz, t // tm)
    assert kind == "time"
    return (t, bsz * D_MODEL), pl.BlockSpec((tm, D_MODEL), lambda b, i: (i, b)), (bsz, t // tm)


def _ffn(x, g2, wgu, wd, bsz, t, src, dst):
    tm = min(FFN_ROWS, t if "rows" not in (src, dst) else bsz * t)
    in_shape, in_spec, grid = _row_layout(src, bsz, t, tm)
    out_shape, out_spec, grid_out = _row_layout(dst, bsz, t, tm)
    assert grid == grid_out and x.shape == in_shape
    zero = lambda b, i: (0, 0)
    resident = dict(pipeline_mode=pl.Buffered(1))
    return pl.pallas_call(
        _ffn_kernel,
        grid=grid,
        in_specs=[
            in_spec,
            pl.BlockSpec((2, D_MODEL), zero),
            pl.BlockSpec((D_MODEL, 2 * D_FF), zero, **resident),
            pl.BlockSpec((D_FF, D_MODEL), zero, **resident),
        ],
        out_specs=out_spec,
        out_shape=jax.ShapeDtypeStruct(out_shape, F32),
        scratch_shapes=[pltpu.VMEM((tm, D_FF), BF16)],
        compiler_params=pltpu.CompilerParams(
            dimension_semantics=("arbitrary", "arbitrary"), vmem_limit_bytes=VMEM_LIMIT),
        name="ffn",
    )(x, g2, wgu, wd)


def _gla_tables(rows, seq):
    t = np.arange(rows)[:, None]
    r = np.arange(rows)[None, :]
    cum = ((t // seq) == (r // seq)) & (r <= t)
    masks = [t == r]
    m = seq // 2
    while m >= 1:
        masks.append(((t // (2 * m)) == (r // (2 * m))) & (t % (2 * m) >= m) & (r % (2 * m) < m))
        m //= 2
    return jnp.asarray(cum.astype(np.float32), BF16), jnp.asarray(np.stack(masks).astype(np.float32))


def _gla_project(x, g_ref, win_ref, wglr_ref, wg2_ref, bg_ref, cum_ref, proj_ref, lf_ref, b_ref):
    hn = _rms(x, g_ref[0:1, :]).astype(BF16)
    for c0 in range(0, proj_ref.shape[1], GLA_PROJ_CHUNK):
        proj_ref[:, c0:c0 + GLA_PROJ_CHUNK] = _dot(hn, win_ref[:, c0:c0 + GLA_PROJ_CHUNK])
    glr = _dot(hn, wglr_ref[...]).astype(BF16)
    gate = _dot(glr, wg2_ref[...]) + bg_ref[...]
    lf = (jnp.minimum(gate, 0.0) - jnp.log1p(jnp.exp(-jnp.abs(gate)))) * (1.0 / GATE_TAU)
    lf_ref[...] = lf
    hi, lo = _split_bf16(lf)
    b_ref[...] = _dot(cum_ref[...], hi) + _dot(cum_ref[...], lo)


def _block_rows(ref, r0, col, rows, period, offset):
    def bc(row, n):
        tile = ref[pl.ds(r0 + row // SUBLANE * SUBLANE, SUBLANE), col]
        return jnp.broadcast_to(tile[row % SUBLANE:row % SUBLANE + 1, :], (n, GLA_DK))

    if period >= SUBLANE:
        parts = [bc(p0 + offset, period) for p0 in range(0, rows, period)]
    else:
        assert 2 * period == SUBLANE
        low = lax.broadcasted_iota(jnp.int32, (SUBLANE, 1), 0) < period
        parts = [jnp.where(low, bc(t0 + offset, SUBLANE), bc(t0 + period + offset, SUBLANE))
                 for t0 in range(0, rows, SUBLANE)]
    return parts[0] if len(parts) == 1 else jnp.concatenate(parts, axis=0)


def _gla_attention(qs, k, b, lf, b_ref, r0, col, rows, seq, mask_ref):
    qb, kb = qs.astype(BF16), k.astype(BF16)
    att = _dot_nt(qb, kb) * mask_ref[0]
    lvl, m = 1, seq // 2
    while m >= 1:
        if m > 1:
            d = b - _block_rows(b_ref, r0, col, rows, 2 * m, m - 1)
            w = jnp.exp2(jnp.abs(d) * (-LOG2E))
        else:
            odd = lax.broadcasted_iota(jnp.int32, (rows, 1), 0) % 2 == 1
            w = jnp.where(odd, jnp.exp(lf), 1.0)
        wb = w.astype(BF16)
        att = att + _dot_nt(qb * wb, kb * wb) * mask_ref[lvl]
        lvl, m = lvl + 1, m // 2
    return att


def _gla_finish_head(o, r, gon_ref):
    on = _rms(o, gon_ref[...])
    return (on * (r * _sigmoid(r))).astype(BF16)


def _gla_prompt_kernel(x_ref, g_ref, win_ref, wglr_ref, wg2_ref, bg_ref, gon_ref, wout_ref,
                       cum_ref, mask_ref, o_ref, snew_ref,
                       proj_ref, lf_ref, b_ref, og_ref, s_ref):
    tb = pl.program_id(1)
    c = GLA_CHUNK

    @pl.when(tb == 0)
    def _():
        s_ref[...] = jnp.zeros_like(s_ref)

    x = x_ref[0]
    _gla_project(x, g_ref, win_ref, wglr_ref, wg2_ref, bg_ref, cum_ref, proj_ref, lf_ref, b_ref)

    for r0 in range(0, GLA_TBLOCK, c):
        rows = slice(r0, r0 + c)
        for h in range(GLA_HEADS):
            kcol = slice(h * GLA_DK, (h + 1) * GLA_DK)
            vcol = slice(h * GLA_DV, (h + 1) * GLA_DV)
            qs = proj_ref[rows, kcol] * (GLA_DK ** -0.5)
            k = proj_ref[rows, GLA_KEY_DIM + h * GLA_DK:GLA_KEY_DIM + (h + 1) * GLA_DK]
            v = proj_ref[rows, 2 * GLA_KEY_DIM + h * GLA_DV:2 * GLA_KEY_DIM + (h + 1) * GLA_DV].astype(BF16)
            r = proj_ref[rows, 2 * GLA_KEY_DIM + GLA_VAL_DIM + h * GLA_DV:
                         2 * GLA_KEY_DIM + GLA_VAL_DIM + (h + 1) * GLA_DV]
            b = b_ref[rows, kcol]
            att = _gla_attention(qs, k, b, lf_ref[rows, kcol], b_ref, r0, kcol, c, c, mask_ref)
            s = s_ref[h]
            o = _dot(att.astype(BF16), v) + _dot((qs * jnp.exp(b)).astype(BF16), s.astype(BF16))
            khat = k * jnp.exp(_block_rows(b_ref, r0, kcol, c, c, c - 1) - b)
            tile = jnp.concatenate(
                [khat, jnp.exp(b[c - SUBLANE:c, :]), jnp.zeros((LANE - c - SUBLANE, GLA_DK), F32)], axis=0)
            tt = jnp.transpose(tile)
            decay = tt[:, c + SUBLANE - 1:c + SUBLANE]
            s_ref[h] = s * decay + _dot(tt[:, 0:c].astype(BF16), v)
            og_ref[rows, vcol] = _gla_finish_head(o, r, gon_ref)
    out = _dot(og_ref[...], wout_ref[...])
    o_ref[0] = x + _rms(out, g_ref[1:2, :])

    @pl.when(tb == pl.num_programs(1) - 1)
    def _():
        snew_ref[0] = s_ref[...]


def _gla_sample_kernel(x_ref, g_ref, win_ref, wglr_ref, wg2_ref, bg_ref, gon_ref, wout_ref,
                       cum_ref, mask_ref, s0_ref, o_ref, snew_ref,
                       proj_ref, lf_ref, b_ref, og_ref, *, seq):
    rows = x_ref.shape[0]
    grp = 2 * SUBLANE
    x = x_ref[...]
    _gla_project(x, g_ref, win_ref, wglr_ref, wg2_ref, bg_ref, cum_ref, proj_ref, lf_ref, b_ref)
    rid = lax.broadcasted_iota(jnp.int32, (grp, 1), 0)
    for h in range(GLA_HEADS):
        kcol = slice(h * GLA_DK, (h + 1) * GLA_DK)
        vcol = slice(h * GLA_DV, (h + 1) * GLA_DV)
        qs = proj_ref[:, kcol] * (GLA_DK ** -0.5)
        k = proj_ref[:, GLA_KEY_DIM + h * GLA_DK:GLA_KEY_DIM + (h + 1) * GLA_DK]
        v = proj_ref[:, 2 * GLA_KEY_DIM + h * GLA_DV:2 * GLA_KEY_DIM + (h + 1) * GLA_DV]
        r = proj_ref[:, 2 * GLA_KEY_DIM + GLA_VAL_DIM + h * GLA_DV:
                     2 * GLA_KEY_DIM + GLA_VAL_DIM + (h + 1) * GLA_DV]
        b = b_ref[:, kcol]
        att = _gla_attention(qs, k, b, lf_ref[:, kcol], b_ref, 0, kcol, rows, seq, mask_ref)
        eb = jnp.exp(b)
        qb = (qs * eb).astype(BF16)
        khat = k * jnp.exp(_block_rows(b_ref, 0, kcol, rows, seq, seq - 1) - b)
        o_intra = _dot(att.astype(BF16), v.astype(BF16))
        o_parts = []
        for gi in range(rows // grp):
            gr = slice(gi * grp, (gi + 1) * grp)
            tile = jnp.concatenate(
                [khat[gr], eb[gr], jnp.zeros((LANE - 2 * grp, GLA_DK), F32)], axis=0)
            tt = jnp.transpose(tile)
            kt = tt[:, 0:grp].astype(BF16)
            qg = qb[gr]
            vg = v[gr]
            o_g = jnp.zeros((grp, GLA_DV), F32)
            for si in range(grp // seq):
                sq = gi * (grp // seq) + si
                mine = (rid >= si * seq) & (rid < (si + 1) * seq)
                s = s0_ref[sq, h]
                o_g = o_g + jnp.where(mine, _dot(qg, s.astype(BF16)), 0.0)
                vm = jnp.where(mine, vg, 0.0).astype(BF16)
                last = grp + (si + 1) * seq - 1
                snew_ref[sq, h] = s * tt[:, last:last + 1] + _dot(kt, vm)
            o_parts.append(o_g)
        o = o_intra + jnp.concatenate(o_parts, axis=0)
        og_ref[:, vcol] = _gla_finish_head(o, r, gon_ref)
    out = _dot(og_ref[...], wout_ref[...])
    o_ref[...] = x + _rms(out, g_ref[1:2, :])


def _gla_weight_specs(nidx):
    zero = (lambda *_: (0, 0))
    resident = dict(pipeline_mode=pl.Buffered(1))
    del nidx
    return [
        pl.BlockSpec((2, D_MODEL), zero),
        pl.BlockSpec((D_MODEL, 2 * GLA_KEY_DIM + 2 * GLA_VAL_DIM), zero, **resident),
        pl.BlockSpec((D_MODEL, LANE), zero, **resident),
        pl.BlockSpec((LANE, GLA_KEY_DIM), zero, **resident),
        pl.BlockSpec((1, GLA_KEY_DIM), zero),
        pl.BlockSpec((1, GLA_DV), zero),
        pl.BlockSpec((GLA_VAL_DIM, D_MODEL), zero, **resident),
    ]


def _gla_weights(w_in, w_g2, b_g, g_onorm, w_out):
    n_main = 2 * GLA_KEY_DIM + 2 * GLA_VAL_DIM
    w_main = w_in[:, :n_main].astype(BF16)
    w_glr = jnp.pad(w_in[:, n_main:], ((0, 0), (0, LANE - GATE_RANK))).astype(BF16)
    w_g2p = jnp.pad(w_g2, ((0, LANE - GATE_RANK), (0, 0))).astype(BF16)
    return (w_main, w_glr, w_g2p, b_g.reshape(1, GLA_KEY_DIM), g_onorm.reshape(1, GLA_DV),
            w_out.astype(BF16))


def _gla_prompt(x, g2, weights):
    bsz, t, _ = x.shape
    cum, _ = _gla_tables(GLA_TBLOCK, GLA_CHUNK)
    _, masks = _gla_tables(GLA_CHUNK, GLA_CHUNK)
    n_main = 2 * GLA_KEY_DIM + 2 * GLA_VAL_DIM
    const2 = lambda b, i: (0, 0)
    return pl.pallas_call(
        _gla_prompt_kernel,
        grid=(bsz, t // GLA_TBLOCK),
        in_specs=[pl.BlockSpec((1, GLA_TBLOCK, D_MODEL), lambda b, i: (b, i, 0))]
        + _gla_weight_specs(2)
        + [pl.BlockSpec(cum.shape, const2),
           pl.BlockSpec(masks.shape, lambda b, i: (0, 0, 0))],
        out_specs=[
            pl.BlockSpec((1, GLA_TBLOCK, D_MODEL), lambda b, i: (b, i, 0)),
            pl.BlockSpec((1, GLA_HEADS, GLA_DK, GLA_DV), lambda b, i: (b, 0, 0, 0)),
        ],
        out_shape=[
            jax.ShapeDtypeStruct(x.shape, F32),
            jax.ShapeDtypeStruct((bsz, GLA_HEADS, GLA_DK, GLA_DV), F32),
        ],
        scratch_shapes=[
            pltpu.VMEM((GLA_TBLOCK, n_main), F32),
            pltpu.VMEM((GLA_TBLOCK, GLA_KEY_DIM), F32),
            pltpu.VMEM((GLA_TBLOCK, GLA_KEY_DIM), F32),
            pltpu.VMEM((GLA_TBLOCK, GLA_VAL_DIM), BF16),
            pltpu.VMEM((GLA_HEADS, GLA_DK, GLA_DV), F32),
        ],
        compiler_params=pltpu.CompilerParams(
            dimension_semantics=("arbitrary", "arbitrary"), vmem_limit_bytes=VMEM_LIMIT),
        name="gla_prompt",
    )(x, g2, *weights, cum, masks)


def _gla_sample(x, s0, g2, weights):
    bsz, seq, _ = x.shape
    rows = GLA_SAMPLE_SEQS * seq
    cum, masks = _gla_tables(rows, seq)
    n_main = 2 * GLA_KEY_DIM + 2 * GLA_VAL_DIM
    state_spec = pl.BlockSpec((GLA_SAMPLE_SEQS, GLA_HEADS, GLA_DK, GLA_DV), lambda i: (i, 0, 0, 0))
    y, snew = pl.pallas_call(
        functools.partial(_gla_sample_kernel, seq=seq),
        grid=(bsz // GLA_SAMPLE_SEQS,),
        in_specs=[pl.BlockSpec((rows, D_MODEL), lambda i: (i, 0))]
        + _gla_weight_specs(1)
        + [pl.BlockSpec(cum.shape, lambda i: (0, 0)),
           pl.BlockSpec(masks.shape, lambda i: (0, 0, 0)),
           state_spec],
        out_specs=[pl.BlockSpec((rows, D_MODEL), lambda i: (i, 0)), state_spec],
        out_shape=[
            jax.ShapeDtypeStruct((bsz * seq, D_MODEL), F32),
            jax.ShapeDtypeStruct(s0.shape, F32),
        ],
        scratch_shapes=[
            pltpu.VMEM((rows, n_main), F32),
            pltpu.VMEM((rows, GLA_KEY_DIM), F32),
            pltpu.VMEM((rows, GLA_KEY_DIM), F32),
            pltpu.VMEM((rows, GLA_VAL_DIM), BF16),
        ],
        compiler_params=pltpu.CompilerParams(
            dimension_semantics=("arbitrary",), vmem_limit_bytes=VMEM_LIMIT),
        name="gla_sample",
    )(x.reshape(bsz * seq, D_MODEL), g2, *weights, cum, masks, s0)
    return y.reshape(x.shape), snew


def _s5_discretize(lam_re, lam_im, log_dt):
    dt = jnp.exp(log_dt)
    mag = jnp.exp(lam_re * dt)
    ang = lam_im * dt
    a_re, a_im = mag * jnp.cos(ang), mag * jnp.sin(ang)
    nr, ni = a_re - 1.0, a_im
    den = lam_re * lam_re + lam_im * lam_im
    f_re = (nr * lam_re + ni * lam_im) / den
    f_im = (ni * lam_re - nr * lam_im) / den
    return a_re, a_im, f_re, f_im


def _s5_prep_kernel(lre_ref, lim_ref, ldt_ref, bre_ref, bim_ref, cre_ref, cim_ref,
                    lref_ref, limf_ref, ldtf_ref, wb_ref, wc_ref, are_ref, aim_ref):
    _, _, f_re, f_im = _s5_discretize(lre_ref[...], lim_ref[...], ldt_ref[...])
    b_re, b_im = bre_ref[...], bim_ref[...]
    row = lax.broadcasted_iota(jnp.int32, (S5_SUPER_CH, S5_SUPER_ST), 0)
    col = lax.broadcasted_iota(jnp.int32, (S5_SUPER_CH, S5_SUPER_ST), 1)
    own = (row // S5_GROUP) == (col // S5_STATE)
    wb_ref[0, :, 0:S5_SUPER_ST] = jnp.where(own, f_re * b_re - f_im * b_im, 0.0).astype(BF16)
    wb_ref[0, :, S5_SUPER_ST:] = jnp.where(own, f_re * b_im + f_im * b_re, 0.0).astype(BF16)
    rowc = lax.broadcasted_iota(jnp.int32, (S5_SUPER_ST, S5_SUPER_CH), 0)
    colc = lax.broadcasted_iota(jnp.int32, (S5_SUPER_ST, S5_SUPER_CH), 1)
    ownc = (rowc // S5_STATE) == (colc // S5_GROUP)
    wc_ref[0, 0:S5_SUPER_ST, :] = jnp.where(ownc, cre_ref[...], 0.0).astype(BF16)
    wc_ref[0, S5_SUPER_ST:, :] = jnp.where(ownc, -cim_ref[...], 0.0).astype(BF16)
    a_re, a_im, _, _ = _s5_discretize(lref_ref[...], limf_ref[...], ldtf_ref[...])
    are_ref[...] = a_re
    aim_ref[...] = a_im


def _s5_prep(lam_re, lam_im, log_dt, b_re, b_im, c_re, c_im):
    g, p, ch, sg = S5_GROUPS, S5_STATE, S5_GROUP, S5_SUPER

    def rows_by_group(a_gp):
        return jnp.broadcast_to(a_gp[:, None, None, :], (g, ch, sg, p)).reshape(g * ch, sg * p)

    def b_layout(b):
        bt = jnp.transpose(b, (0, 2, 1))
        return jnp.broadcast_to(bt[:, :, None, :], (g, ch, sg, p)).reshape(g * ch, sg * p)

    def c_layout(cm):
        ct = jnp.transpose(cm.reshape(g // sg, sg, ch, p), (3, 0, 1, 2)).reshape(p, g * ch)
        return jnp.broadcast_to(ct[None], (sg, p, g * ch)).reshape(sg * p, g * ch)

    ldt_gp = jnp.broadcast_to(log_dt[:, None], (g, p))
    big = pl.BlockSpec((S5_SUPER_CH, S5_SUPER_ST), lambda j: (j, 0))
    cspec = pl.BlockSpec((S5_SUPER_ST, S5_SUPER_CH), lambda j: (0, j))
    flat = pl.BlockSpec((1, S5_SUPER_ST), lambda j: (0, j))
    return pl.pallas_call(
        _s5_prep_kernel,
        grid=(S5_NSUPER,),
        in_specs=[big] * 5 + [cspec] * 2 + [flat] * 3,
        out_specs=[
            pl.BlockSpec((1, S5_SUPER_CH, 2 * S5_SUPER_ST), lambda j: (j, 0, 0)),
            pl.BlockSpec((1, 2 * S5_SUPER_ST, S5_SUPER_CH), lambda j: (j, 0, 0)),
            flat, flat,
        ],
        out_shape=[
            jax.ShapeDtypeStruct((S5_NSUPER, S5_SUPER_CH, 2 * S5_SUPER_ST), BF16),
            jax.ShapeDtypeStruct((S5_NSUPER, 2 * S5_SUPER_ST, S5_SUPER_CH), BF16),
            jax.ShapeDtypeStruct((1, S5_FLAT), F32),
            jax.ShapeDtypeStruct((1, S5_FLAT), F32),
        ],
        compiler_params=pltpu.CompilerParams(dimension_semantics=("arbitrary",)),
        name="s5_prep",
    )(rows_by_group(lam_re), rows_by_group(lam_im), rows_by_group(ldt_gp),
      b_layout(b_re), b_layout(b_im), c_layout(c_re), c_layout(c_im),
      lam_re.reshape(1, S5_FLAT), lam_im.reshape(1, S5_FLAT), ldt_gp.reshape(1, S5_FLAT))


def _s5_kernel(x_ref, g_ref, wb_ref, wc_ref, are_ref, aim_ref, d_ref, wglu_ref, bglu_ref,
               h0re_ref, h0im_ref, o_ref, hre_ref, him_ref,
               u_ref, sre_ref, sim_ref, z_ref, *, bsz, steps):
    i = pl.program_id(0)

    @pl.when(i == 0)
    def _():
        hre_ref[...] = h0re_ref[...]
        him_ref[...] = h0im_ref[...]

    x = x_ref[...]
    u = _rms(x, g_ref[0:1, :])
    u_ref[...] = u
    ub = u.astype(BF16)
    for j in range(S5_NSUPER):
        bu = _dot(ub[:, j * S5_SUPER_CH:(j + 1) * S5_SUPER_CH], wb_ref[j])
        sre_ref[:, j * S5_SUPER_ST:(j + 1) * S5_SUPER_ST] = bu[:, :S5_SUPER_ST]
        sim_ref[:, j * S5_SUPER_ST:(j + 1) * S5_SUPER_ST] = bu[:, S5_SUPER_ST:]

    for lc in range(S5_FLAT // S5_LANES):
        ls = slice(lc * S5_LANES, (lc + 1) * S5_LANES)
        a_re = jnp.broadcast_to(are_ref[:, ls], (SUBLANE, S5_LANES))
        a_im = jnp.broadcast_to(aim_ref[:, ls], (SUBLANE, S5_LANES))
        for rt in range(bsz // SUBLANE):
            rs = slice(rt * SUBLANE, (rt + 1) * SUBLANE)

            def step(t, carry, ls=ls, rt=rt, a_re=a_re, a_im=a_im):
                h_re, h_im = carry
                rows = pl.ds(pl.multiple_of(t * bsz + rt * SUBLANE, SUBLANE), SUBLANE)
                n_re = a_re * h_re - a_im * h_im + sre_ref[rows, ls]
                n_im = a_re * h_im + a_im * h_re + sim_ref[rows, ls]
                sre_ref[rows, ls] = n_re
                sim_ref[rows, ls] = n_im
                return n_re, n_im

            h_re, h_im = lax.fori_loop(0, steps, step, (hre_ref[rs, ls], him_ref[rs, ls]),
                                       unroll=True)
            hre_ref[rs, ls] = h_re
            him_ref[rs, ls] = h_im

    for j in range(S5_NSUPER):
        st = slice(j * S5_SUPER_ST, (j + 1) * S5_SUPER_ST)
        ch = slice(j * S5_SUPER_CH, (j + 1) * S5_SUPER_CH)
        y = (_dot(sre_ref[:, st].astype(BF16), wc_ref[j, 0:S5_SUPER_ST, :])
             + _dot(sim_ref[:, st].astype(BF16), wc_ref[j, S5_SUPER_ST:, :]))
        z_ref[:, ch] = (y + d_ref[:, ch] * u_ref[:, ch]).astype(BF16)
    zz = _dot(z_ref[...], wglu_ref[...]) + bglu_ref[...]
    out = zz[:, :D_MODEL] * _sigmoid(zz[:, D_MODEL:])
    o_ref[...] = x + _rms(out, g_ref[1:2, :])


def _s5(x_tm, bsz, steps_per_block, g2, prep, d_skip, w_glu, b_glu, h0_re, h0_im):
    wb, wc, a_re, a_im = prep
    n = x_tm.shape[0]
    rows = steps_per_block * bsz
    zero2 = lambda i: (0, 0)
    zero3 = lambda i: (0, 0, 0)
    resident = dict(pipeline_mode=pl.Buffered(1))
    state = pl.BlockSpec((bsz, S5_FLAT), zero2)
    return pl.pallas_call(
        functools.partial(_s5_kernel, bsz=bsz, steps=steps_per_block),
        grid=(n // rows,),
        in_specs=[
            pl.BlockSpec((rows, D_MODEL), lambda i: (i, 0)),
            pl.BlockSpec((2, D_MODEL), zero2),
            pl.BlockSpec(wb.shape, zero3, **resident),
            pl.BlockSpec(wc.shape, zero3, **resident),
            pl.BlockSpec((1, S5_FLAT), zero2),
            pl.BlockSpec((1, S5_FLAT), zero2),
            pl.BlockSpec((1, D_MODEL), zero2),
            pl.BlockSpec((D_MODEL, 2 * D_MODEL), zero2, **resident),
            pl.BlockSpec((1, 2 * D_MODEL), zero2),
            state, state,
        ],
        out_specs=[pl.BlockSpec((rows, D_MODEL), lambda i: (i, 0)), state, state],
        out_shape=[
            jax.ShapeDtypeStruct((n, D_MODEL), F32),
            jax.ShapeDtypeStruct((bsz, S5_FLAT), F32),
            jax.ShapeDtypeStruct((bsz, S5_FLAT), F32),
        ],
        scratch_shapes=[
            pltpu.VMEM((rows, D_MODEL), F32),
            pltpu.VMEM((rows, S5_FLAT), F32),
            pltpu.VMEM((rows, S5_FLAT), F32),
            pltpu.VMEM((rows, D_MODEL), BF16),
        ],
        compiler_params=pltpu.CompilerParams(
            dimension_semantics=("arbitrary",), vmem_limit_bytes=VMEM_LIMIT),
        name="s5",
    )(x_tm, g2, wb, wc, a_re, a_im, d_skip.reshape(1, D_MODEL), w_glu.astype(BF16),
      b_glu.reshape(1, 2 * D_MODEL), h0_re, h0_im)


def kernel(x_prompt, x_sample, state_gla, state_s5_re, state_s5_im, norm_g, w_ffn_gu, w_ffn_down,
           gla_w_in, gla_w_g2, gla_b_g, gla_g_onorm, gla_w_out,
           s5_lam_re, s5_lam_im, s5_log_dt, s5_b_re, s5_b_im, s5_c_re, s5_c_im, s5_d, s5_w_glu,
           s5_b_glu):
    pb, pt, _ = x_prompt.shape
    sb, st, _ = x_sample.shape
    wgu = w_ffn_gu.astype(BF16)
    wd = w_ffn_down.astype(BF16)

    def ffn(x, layer, which, bsz, t, src, dst):
        g2 = norm_g[layer, 4 * which:4 * which + 2]
        return _ffn(x, g2, wgu[layer, which], wd[layer, which], bsz, t, src, dst)

    gla_w = _gla_weights(gla_w_in[0], gla_w_g2[0], gla_b_g[0], gla_g_onorm[0], gla_w_out[0])
    xp = ffn(x_prompt, 0, 0, pb, pt, "batch", "batch")
    xs = ffn(x_sample.reshape(sb * st, D_MODEL), 0, 0, sb, st, "rows", "rows")
    xp, gla_p = _gla_prompt(xp, norm_g[0, 2:4], gla_w)
    xs, gla_s = _gla_sample(xs.reshape(sb, st, D_MODEL),
                            state_gla.reshape(sb, GLA_HEADS, GLA_DK, GLA_DV), norm_g[0, 2:4], gla_w)
    xp = ffn(xp, 0, 1, pb, pt, "batch", "time").reshape(pt * pb, D_MODEL)
    xs = ffn(xs.reshape(sb * st, D_MODEL), 0, 1, sb, st, "rows", "rows")

    xs = jnp.transpose(xs.reshape(sb, st, D_MODEL), (1, 0, 2)).reshape(st * sb, D_MODEL)
    xp = ffn(xp, 1, 0, pb, pt, "rows", "rows")
    xs = ffn(xs, 1, 0, sb, st, "rows", "rows")
    prep = _s5_prep(s5_lam_re[0], s5_lam_im[0], s5_log_dt[0], s5_b_re[0], s5_b_im[0],
                    s5_c_re[0], s5_c_im[0])
    zeros = jnp.zeros((pb, S5_FLAT), F32)
    s5_args = (norm_g[1, 2:4], prep, s5_d[0], s5_w_glu[0], s5_b_glu[0])
    xp, hre_p, him_p = _s5(xp, pb, 32, *s5_args, zeros, zeros)
    xs, hre_s, him_s = _s5(xs, sb, st, *s5_args,
                           state_s5_re.reshape(sb, S5_FLAT), state_s5_im.reshape(sb, S5_FLAT))
    y_prompt = ffn(xp.reshape(pt, pb * D_MODEL), 1, 1, pb, pt, "time", "batch")
    xs = ffn(xs, 1, 1, sb, st, "rows", "rows")
    y_sample = jnp.transpose(xs.reshape(st, sb, D_MODEL), (1, 0, 2))

    def s5_state(h, b):
        return h.reshape(1, b, S5_GROUPS, S5_STATE)

    return (y_prompt, y_sample, gla_p.reshape(1, pb, GLA_HEADS, GLA_DK, GLA_DV),
            s5_state(hre_p, pb), s5_state(him_p, pb),
            gla_s.reshape(1, sb, GLA_HEADS, GLA_DK, GLA_DV),
            s5_state(hre_s, sb), s5_state(him_s, sb))
```

```python
import functools
import math

import jax
import jax.numpy as jnp
import numpy as np
from jax import lax
from jax.experimental import pallas as pl
from jax.experimental.pallas import tpu as pltpu

F32 = jnp.float32
BF16 = jnp.bfloat16

D_MODEL = 1024
D_FF = 2816
GLA_HEADS = 4
GLA_DK = 128
GLA_DV = 256
GLA_KEY_DIM = GLA_HEADS * GLA_DK
GLA_VAL_DIM = GLA_HEADS * GLA_DV
GATE_RANK = 16
GATE_TAU = 16.0
S5_GROUP = 16
S5_GROUPS = 64
S5_STATE = 64
S5_FLAT = S5_GROUPS * S5_STATE
S5_SUPER = 8
S5_NSUPER = S5_GROUPS // S5_SUPER
S5_SUPER_CH = S5_SUPER * S5_GROUP
S5_SUPER_ST = S5_SUPER * S5_STATE
EPS = 1e-6
LANE = 128
SUBLANE = 8
VMEM_LIMIT = 56 * 1024 * 1024

FFN_ROWS = 512
FFN_CHUNK = 256
GLA_CHUNK = 64
GLA_TBLOCK = 256
GLA_PROJ_CHUNK = 256
LOG2E = math.log2(math.e)
GLA_SAMPLE_SEQS = 8
S5_LANES = 512
S5_STEPS = 32


def _dot(a, b):
    return jnp.dot(a, b, preferred_element_type=F32)


def _dot_nt(a, b):
    return lax.dot_general(a, b, (((1,), (1,)), ((), ())), preferred_element_type=F32)


def _rms(x, g):
    ms = jnp.mean(x * x, axis=-1, keepdims=True)
    return x * lax.rsqrt(ms + EPS) * g


def _sigmoid(x):
    return 1.0 / (1.0 + jnp.exp(-x))


def _split_bf16(x):
    hi = x.astype(BF16)
    lo = (x - hi.astype(F32)).astype(BF16)
    return hi, lo


def _ffn_kernel(x_ref, g_ref, wgu_ref, wd_ref, o_ref, act_ref):
    x = x_ref[...]
    xn = _rms(x, g_ref[0:1, :]).astype(BF16)
    for c in range(D_FF // FFN_CHUNK):
        lo = c * FFN_CHUNK
        gate = _dot(xn, wgu_ref[:, lo:lo + FFN_CHUNK])
        up = _dot(xn, wgu_ref[:, D_FF + lo:D_FF + lo + FFN_CHUNK])
        act_ref[:, lo:lo + FFN_CHUNK] = (gate * _sigmoid(gate) * up).astype(BF16)
    y = _dot(act_ref[...], wd_ref[...])
    o_ref[...] = x + 0.5 * _rms(y, g_ref[1:2, :])


def _ffn(x, g2, wgu, wd):
    shape = x.shape
    x = x.reshape(-1, D_MODEL)
    n = x.shape[0]
    tm = min(FFN_ROWS, n)
    assert n % tm == 0
    zero = lambda i: (0, 0)
    resident = dict(pipeline_mode=pl.Buffered(1))
    return pl.pallas_call(
        _ffn_kernel,
        grid=(n // tm,),
        in_specs=[
            pl.BlockSpec((tm, D_MODEL), lambda i: (i, 0)),
            pl.BlockSpec((2, D_MODEL), zero),
            pl.BlockSpec((D_MODEL, 2 * D_FF), zero, **resident),
            pl.BlockSpec((D_FF, D_MODEL), zero, **resident),
        ],
        out_specs=pl.BlockSpec((tm, D_MODEL), lambda i: (i, 0)),
        out_shape=jax.ShapeDtypeStruct((n, D_MODEL), F32),
        scratch_shapes=[pltpu.VMEM((tm, D_FF), BF16)],
        compiler_params=pltpu.CompilerParams(
            dimension_semantics=("arbitrary",), vmem_limit_bytes=VMEM_LIMIT),
        name="ffn",
    )(x, g2, wgu, wd).reshape(shape)


def _gla_tables(rows, seq):
    t = np.arange(rows)[:, None]
    r = np.arange(rows)[None, :]
    cum = ((t // seq) == (r // seq)) & (r <= t)
    masks = [t == r]
    m = seq // 2
    while m >= 1:
        masks.append(((t // (2 * m)) == (r // (2 * m))) & (t % (2 * m) >= m) & (r % (2 * m) < m))
        m //= 2
    return jnp.asarray(cum.astype(np.float32), BF16), jnp.asarray(np.stack(masks).astype(np.float32))


def _gla_project(x, g_ref, win_ref, wglr_ref, wg2_ref, bg_ref, cum_ref, proj_ref, lf_ref, b_ref):
    hn = _rms(x, g_ref[0:1, :]).astype(BF16)
    for c0 in range(0, proj_ref.shape[1], GLA_PROJ_CHUNK):
        proj_ref[:, c0:c0 + GLA_PROJ_CHUNK] = _dot(hn, win_ref[:, c0:c0 + GLA_PROJ_CHUNK])
    glr = _dot(hn, wglr_ref[...]).astype(BF16)
    gate = _dot(glr, wg2_ref[...]) + bg_ref[...]
    lf = (jnp.minimum(gate, 0.0) - jnp.log1p(jnp.exp(-jnp.abs(gate)))) * (1.0 / GATE_TAU)
    lf_ref[...] = lf
    hi, lo = _split_bf16(lf)
    b_ref[...] = _dot(cum_ref[...], hi) + _dot(cum_ref[...], lo)


def _block_rows(ref, r0, col, rows, period, offset):
    def bc(row, n):
        tile = ref[pl.ds(r0 + row // SUBLANE * SUBLANE, SUBLANE), col]
        return jnp.broadcast_to(tile[row % SUBLANE:row % SUBLANE + 1, :], (n, GLA_DK))

    if period >= SUBLANE:
        parts = [bc(p0 + offset, period) for p0 in range(0, rows, period)]
    else:
        assert 2 * period == SUBLANE
        low = lax.broadcasted_iota(jnp.int32, (SUBLANE, 1), 0) < period
        parts = [jnp.where(low, bc(t0 + offset, SUBLANE), bc(t0 + period + offset, SUBLANE))
                 for t0 in range(0, rows, SUBLANE)]
    return parts[0] if len(parts) == 1 else jnp.concatenate(parts, axis=0)


def _gla_attention(qs, k, b, lf, b_ref, r0, col, rows, seq, mask_ref):
    qb, kb = qs.astype(BF16), k.astype(BF16)
    att = _dot_nt(qb, kb) * mask_ref[0]
    lvl, m = 1, seq // 2
    while m >= 1:
        if m > 1:
            d = b - _block_rows(b_ref, r0, col, rows, 2 * m, m - 1)
            w = jnp.exp2(jnp.abs(d) * (-LOG2E))
        else:
            odd = lax.broadcasted_iota(jnp.int32, (rows, 1), 0) % 2 == 1
            w = jnp.where(odd, jnp.exp(lf), 1.0)
        wb = w.astype(BF16)
        att = att + _dot_nt(qb * wb, kb * wb) * mask_ref[lvl]
        lvl, m = lvl + 1, m // 2
    return att


def _gla_finish_head(o, r, gon_ref):
    on = _rms(o, gon_ref[...])
    return (on * (r * _sigmoid(r))).astype(BF16)


def _gla_prompt_kernel(x_ref, g_ref, win_ref, wglr_ref, wg2_ref, bg_ref, gon_ref, wout_ref,
                       cum_ref, mask_ref, o_ref, snew_ref,
                       proj_ref, lf_ref, b_ref, og_ref, s_ref):
    tb = pl.program_id(1)
    c = GLA_CHUNK

    @pl.when(tb == 0)
    def _():
        s_ref[...] = jnp.zeros_like(s_ref)

    x = x_ref[0]
    _gla_project(x, g_ref, win_ref, wglr_ref, wg2_ref, bg_ref, cum_ref, proj_ref, lf_ref, b_ref)

    for r0 in range(0, GLA_TBLOCK, c):
        rows = slice(r0, r0 + c)
        for h in range(GLA_HEADS):
            kcol = slice(h * GLA_DK, (h + 1) * GLA_DK)
            vcol = slice(h * GLA_DV, (h + 1) * GLA_DV)
            qs = proj_ref[rows, kcol] * (GLA_DK ** -0.5)
            k = proj_ref[rows, GLA_KEY_DIM + h * GLA_DK:GLA_KEY_DIM + (h + 1) * GLA_DK]
            v = proj_ref[rows, 2 * GLA_KEY_DIM + h * GLA_DV:2 * GLA_KEY_DIM + (h + 1) * GLA_DV].astype(BF16)
            r = proj_ref[rows, 2 * GLA_KEY_DIM + GLA_VAL_DIM + h * GLA_DV:
                         2 * GLA_KEY_DIM + GLA_VAL_DIM + (h + 1) * GLA_DV]
            b = b_ref[rows, kcol]
            att = _gla_attention(qs, k, b, lf_ref[rows, kcol], b_ref, r0, kcol, c, c, mask_ref)
            s = s_ref[h]
            o = _dot(att.astype(BF16), v) + _dot((qs * jnp.exp(b)).astype(BF16), s.astype(BF16))
            khat = k * jnp.exp(_block_rows(b_ref, r0, kcol, c, c, c - 1) - b)
            tile = jnp.concatenate(
                [khat, jnp.exp(b[c - SUBLANE:c, :]), jnp.zeros((LANE - c - SUBLANE, GLA_DK), F32)], axis=0)
            tt = jnp.transpose(tile)
            decay = tt[:, c + SUBLANE - 1:c + SUBLANE]
            s_ref[h] = s * decay + _dot(tt[:, 0:c].astype(BF16), v)
            og_ref[rows, vcol] = _gla_finish_head(o, r, gon_ref)
    out = _dot(og_ref[...], wout_ref[...])
    o_ref[0] = x + _rms(out, g_ref[1:2, :])

    @pl.when(tb == pl.num_programs(1) - 1)
    def _():
        snew_ref[0] = s_ref[...]


def _gla_sample_kernel(x_ref, g_ref, win_ref, wglr_ref, wg2_ref, bg_ref, gon_ref, wout_ref,
                       cum_ref, mask_ref, s0_ref, o_ref, snew_ref,
                       proj_ref, lf_ref, b_ref, og_ref, *, seq):
    rows = x_ref.shape[0]
    grp = 2 * SUBLANE
    x = x_ref[...]
    _gla_project(x, g_ref, win_ref, wglr_ref, wg2_ref, bg_ref, cum_ref, proj_ref, lf_ref, b_ref)
    rid = lax.broadcasted_iota(jnp.int32, (grp, 1), 0)
    for h in range(GLA_HEADS):
        kcol = slice(h * GLA_DK, (h + 1) * GLA_DK)
        vcol = slice(h * GLA_DV, (h + 1) * GLA_DV)
        qs = proj_ref[:, kcol] * (GLA_DK ** -0.5)
        k = proj_ref[:, GLA_KEY_DIM + h * GLA_DK:GLA_KEY_DIM + (h + 1) * GLA_DK]
        v = proj_ref[:, 2 * GLA_KEY_DIM + h * GLA_DV:2 * GLA_KEY_DIM + (h + 1) * GLA_DV]
        r = proj_ref[:, 2 * GLA_KEY_DIM + GLA_VAL_DIM + h * GLA_DV:
                     2 * GLA_KEY_DIM + GLA_VAL_DIM + (h + 1) * GLA_DV]
        b = b_ref[:, kcol]
        att = _gla_attention(qs, k, b, lf_ref[:, kcol], b_ref, 0, kcol, rows, seq, mask_ref)
        eb = jnp.exp(b)
        qb = (qs * eb).astype(BF16)
        khat = k * jnp.exp(_block_rows(b_ref, 0, kcol, rows, seq, seq - 1) - b)
        o_intra = _dot(att.astype(BF16), v.astype(BF16))
        o_parts = []
        for gi in range(rows // grp):
            gr = slice(gi * grp, (gi + 1) * grp)
            tile = jnp.concatenate(
                [khat[gr], eb[gr], jnp.zeros((LANE - 2 * grp, GLA_DK), F32)], axis=0)
            tt = jnp.transpose(tile)
            kt = tt[:, 0:grp].astype(BF16)
            qg = qb[gr]
            vg = v[gr]
            o_g = jnp.zeros((grp, GLA_DV), F32)
            for si in range(grp // seq):
                sq = gi * (grp // seq) + si
                mine = (rid >= si * seq) & (rid < (si + 1) * seq)
                s = s0_ref[sq, h]
                o_g = o_g + jnp.where(mine, _dot(qg, s.astype(BF16)), 0.0)
                vm = jnp.where(mine, vg, 0.0).astype(BF16)
                last = grp + (si + 1) * seq - 1
                snew_ref[sq, h] = s * tt[:, last:last + 1] + _dot(kt, vm)
            o_parts.append(o_g)
        o = o_intra + jnp.concatenate(o_parts, axis=0)
        og_ref[:, vcol] = _gla_finish_head(o, r, gon_ref)
    out = _dot(og_ref[...], wout_ref[...])
    o_ref[...] = x + _rms(out, g_ref[1:2, :])


def _gla_weight_specs(nidx):
    zero = (lambda *_: (0, 0))
    resident = dict(pipeline_mode=pl.Buffered(1))
    del nidx
    return [
        pl.BlockSpec((2, D_MODEL), zero),
        pl.BlockSpec((D_MODEL, 2 * GLA_KEY_DIM + 2 * GLA_VAL_DIM), zero, **resident),
        pl.BlockSpec((D_MODEL, LANE), zero, **resident),
        pl.BlockSpec((LANE, GLA_KEY_DIM), zero, **resident),
        pl.BlockSpec((1, GLA_KEY_DIM), zero),
        pl.BlockSpec((1, GLA_DV), zero),
        pl.BlockSpec((GLA_VAL_DIM, D_MODEL), zero, **resident),
    ]


def _gla_weights(w_in, w_g2, b_g, g_onorm, w_out):
    n_main = 2 * GLA_KEY_DIM + 2 * GLA_VAL_DIM
    w_main = w_in[:, :n_main].astype(BF16)
    w_glr = jnp.pad(w_in[:, n_main:], ((0, 0), (0, LANE - GATE_RANK))).astype(BF16)
    w_g2p = jnp.pad(w_g2, ((0, LANE - GATE_RANK), (0, 0))).astype(BF16)
    return (w_main, w_glr, w_g2p, b_g.reshape(1, GLA_KEY_DIM), g_onorm.reshape(1, GLA_DV),
            w_out.astype(BF16))


def _gla_prompt(x, g2, weights):
    bsz, t, _ = x.shape
    cum, _ = _gla_tables(GLA_TBLOCK, GLA_CHUNK)
    _, masks = _gla_tables(GLA_CHUNK, GLA_CHUNK)
    n_main = 2 * GLA_KEY_DIM + 2 * GLA_VAL_DIM
    const2 = lambda b, i: (0, 0)
    return pl.pallas_call(
        _gla_prompt_kernel,
        grid=(bsz, t // GLA_TBLOCK),
        in_specs=[pl.BlockSpec((1, GLA_TBLOCK, D_MODEL), lambda b, i: (b, i, 0))]
        + _gla_weight_specs(2)
        + [pl.BlockSpec(cum.shape, const2),
           pl.BlockSpec(masks.shape, lambda b, i: (0, 0, 0))],
        out_specs=[
            pl.BlockSpec((1, GLA_TBLOCK, D_MODEL), lambda b, i: (b, i, 0)),
            pl.BlockSpec((1, GLA_HEADS, GLA_DK, GLA_DV), lambda b, i: (b, 0, 0, 0)),
        ],
        out_shape=[
            jax.ShapeDtypeStruct(x.shape, F32),
            jax.ShapeDtypeStruct((bsz, GLA_HEADS, GLA_DK, GLA_DV), F32),
        ],
        scratch_shapes=[
            pltpu.VMEM((GLA_TBLOCK, n_main), F32),
            pltpu.VMEM((GLA_TBLOCK, GLA_KEY_DIM), F32),
            pltpu.VMEM((GLA_TBLOCK, GLA_KEY_DIM), F32),
            pltpu.VMEM((GLA_TBLOCK, GLA_VAL_DIM), BF16),
            pltpu.VMEM((GLA_HEADS, GLA_DK, GLA_DV), F32),
        ],
        compiler_params=pltpu.CompilerParams(
            dimension_semantics=("arbitrary", "arbitrary"), vmem_limit_bytes=VMEM_LIMIT),
        name="gla_prompt",
    )(x, g2, *weights, cum, masks)


def _gla_sample(x, s0, g2, weights):
    bsz, seq, _ = x.shape
    rows = GLA_SAMPLE_SEQS * seq
    cum, masks = _gla_tables(rows, seq)
    n_main = 2 * GLA_KEY_DIM + 2 * GLA_VAL_DIM
    state_spec = pl.BlockSpec((GLA_SAMPLE_SEQS, GLA_HEADS, GLA_DK, GLA_DV), lambda i: (i, 0, 0, 0))
    y, snew = pl.pallas_call(
        functools.partial(_gla_sample_kernel, seq=seq),
        grid=(bsz // GLA_SAMPLE_SEQS,),
        in_specs=[pl.BlockSpec((rows, D_MODEL), lambda i: (i, 0))]
        + _gla_weight_specs(1)
        + [pl.BlockSpec(cum.shape, lambda i: (0, 0)),
           pl.BlockSpec(masks.shape, lambda i: (0, 0, 0)),
           state_spec],
        out_specs=[pl.BlockSpec((rows, D_MODEL), lambda i: (i, 0)), state_spec],
        out_shape=[
            jax.ShapeDtypeStruct((bsz * seq, D_MODEL), F32),
            jax.ShapeDtypeStruct(s0.shape, F32),
        ],
        scratch_shapes=[
            pltpu.VMEM((rows, n_main), F32),
            pltpu.VMEM((rows, GLA_KEY_DIM), F32),
            pltpu.VMEM((rows, GLA_KEY_DIM), F32),
            pltpu.VMEM((rows, GLA_VAL_DIM), BF16),
        ],
        compiler_params=pltpu.CompilerParams(
            dimension_semantics=("arbitrary",), vmem_limit_bytes=VMEM_LIMIT),
        name="gla_sample",
    )(x.reshape(bsz * seq, D_MODEL), g2, *weights, cum, masks, s0)
    return y.reshape(x.shape), snew


def _s5_discretize(lam_re, lam_im, log_dt):
    dt = jnp.exp(log_dt)
    mag = jnp.exp(lam_re * dt)
    ang = lam_im * dt
    a_re, a_im = mag * jnp.cos(ang), mag * jnp.sin(ang)
    nr, ni = a_re - 1.0, a_im
    den = lam_re * lam_re + lam_im * lam_im
    f_re = (nr * lam_re + ni * lam_im) / den
    f_im = (ni * lam_re - nr * lam_im) / den
    return a_re, a_im, f_re, f_im


def _s5_prep_kernel(lre_ref, lim_ref, ldt_ref, bre_ref, bim_ref, cre_ref, cim_ref,
                    lref_ref, limf_ref, ldtf_ref, wb_ref, wc_ref, are_ref, aim_ref):
    _, _, f_re, f_im = _s5_discretize(lre_ref[...], lim_ref[...], ldt_ref[...])
    b_re, b_im = bre_ref[...], bim_ref[...]
    row = lax.broadcasted_iota(jnp.int32, (S5_SUPER_CH, S5_SUPER_ST), 0)
    col = lax.broadcasted_iota(jnp.int32, (S5_SUPER_CH, S5_SUPER_ST), 1)
    own = (row // S5_GROUP) == (col // S5_STATE)
    wb_ref[0, :, 0:S5_SUPER_ST] = jnp.where(own, f_re * b_re - f_im * b_im, 0.0).astype(BF16)
    wb_ref[0, :, S5_SUPER_ST:] = jnp.where(own, f_re * b_im + f_im * b_re, 0.0).astype(BF16)
    rowc = lax.broadcasted_iota(jnp.int32, (S5_SUPER_ST, S5_SUPER_CH), 0)
    colc = lax.broadcasted_iota(jnp.int32, (S5_SUPER_ST, S5_SUPER_CH), 1)
    ownc = (rowc // S5_STATE) == (colc // S5_GROUP)
    wc_ref[0, 0:S5_SUPER_ST, :] = jnp.where(ownc, cre_ref[...], 0.0).astype(BF16)
    wc_ref[0, S5_SUPER_ST:, :] = jnp.where(ownc, -cim_ref[...], 0.0).astype(BF16)
    a_re, a_im, _, _ = _s5_discretize(lref_ref[...], limf_ref[...], ldtf_ref[...])
    are_ref[...] = a_re
    aim_ref[...] = a_im


def _s5_prep(lam_re, lam_im, log_dt, b_re, b_im, c_re, c_im):
    g, p, ch, sg = S5_GROUPS, S5_STATE, S5_GROUP, S5_SUPER

    def rows_by_group(a_gp):
        return jnp.broadcast_to(a_gp[:, None, None, :], (g, ch, sg, p)).reshape(g * ch, sg * p)

    def b_layout(b):
        bt = jnp.transpose(b, (0, 2, 1))
        return jnp.broadcast_to(bt[:, :, None, :], (g, ch, sg, p)).reshape(g * ch, sg * p)

    def c_layout(cm):
        ct = jnp.transpose(cm.reshape(g // sg, sg, ch, p), (3, 0, 1, 2)).reshape(p, g * ch)
        return jnp.broadcast_to(ct[None], (sg, p, g * ch)).reshape(sg * p, g * ch)

    ldt_gp = jnp.broadcast_to(log_dt[:, None], (g, p))
    big = pl.BlockSpec((S5_SUPER_CH, S5_SUPER_ST), lambda j: (j, 0))
    cspec = pl.BlockSpec((S5_SUPER_ST, S5_SUPER_CH), lambda j: (0, j))
    flat = pl.BlockSpec((1, S5_SUPER_ST), lambda j: (0, j))
    return pl.pallas_call(
        _s5_prep_kernel,
        grid=(S5_NSUPER,),
        in_specs=[big] * 5 + [cspec] * 2 + [flat] * 3,
        out_specs=[
            pl.BlockSpec((1, S5_SUPER_CH, 2 * S5_SUPER_ST), lambda j: (j, 0, 0)),
            pl.BlockSpec((1, 2 * S5_SUPER_ST, S5_SUPER_CH), lambda j: (j, 0, 0)),
            flat, flat,
        ],
        out_shape=[
            jax.ShapeDtypeStruct((S5_NSUPER, S5_SUPER_CH, 2 * S5_SUPER_ST), BF16),
            jax.ShapeDtypeStruct((S5_NSUPER, 2 * S5_SUPER_ST, S5_SUPER_CH), BF16),
            jax.ShapeDtypeStruct((1, S5_FLAT), F32),
            jax.ShapeDtypeStruct((1, S5_FLAT), F32),
        ],
        compiler_params=pltpu.CompilerParams(dimension_semantics=("arbitrary",)),
        name="s5_prep",
    )(rows_by_group(lam_re), rows_by_group(lam_im), rows_by_group(ldt_gp),
      b_layout(b_re), b_layout(b_im), c_layout(c_re), c_layout(c_im),
      lam_re.reshape(1, S5_FLAT), lam_im.reshape(1, S5_FLAT), ldt_gp.reshape(1, S5_FLAT))


def _s5_kernel(x_ref, g_ref, wb_ref, wc_ref, are_ref, aim_ref, d_ref, wglu_ref, bglu_ref,
               h0re_ref, h0im_ref, perm_ref, unperm_ref, o_ref, hre_ref, him_ref,
               u_ref, sre_ref, sim_ref, z_ref, *, bsz, steps):
    i = pl.program_id(0)
    rows = bsz * steps

    @pl.when(i == 0)
    def _():
        hre_ref[...] = h0re_ref[...]
        him_ref[...] = h0im_ref[...]

    x = x_ref[...].reshape(rows, D_MODEL)
    u = _dot(perm_ref[...], _rms(x, g_ref[0:1, :]).astype(BF16))
    u_ref[...] = u
    ub = u.astype(BF16)
    for j in range(S5_NSUPER):
        bu = _dot(ub[:, j * S5_SUPER_CH:(j + 1) * S5_SUPER_CH], wb_ref[j])
        sre_ref[:, j * S5_SUPER_ST:(j + 1) * S5_SUPER_ST] = bu[:, :S5_SUPER_ST]
        sim_ref[:, j * S5_SUPER_ST:(j + 1) * S5_SUPER_ST] = bu[:, S5_SUPER_ST:]

    for lc in range(S5_FLAT // S5_LANES):
        ls = slice(lc * S5_LANES, (lc + 1) * S5_LANES)
        a_re = jnp.broadcast_to(are_ref[:, ls], (SUBLANE, S5_LANES))
        a_im = jnp.broadcast_to(aim_ref[:, ls], (SUBLANE, S5_LANES))
        for rt in range(bsz // SUBLANE):
            rs = slice(rt * SUBLANE, (rt + 1) * SUBLANE)

            def step(t, carry, ls=ls, rt=rt, a_re=a_re, a_im=a_im):
                h_re, h_im = carry
                rows = pl.ds(pl.multiple_of(t * bsz + rt * SUBLANE, SUBLANE), SUBLANE)
                n_re = a_re * h_re - a_im * h_im + sre_ref[rows, ls]
                n_im = a_re * h_im + a_im * h_re + sim_ref[rows, ls]
                sre_ref[rows, ls] = n_re
                sim_ref[rows, ls] = n_im
                return n_re, n_im

            h_re, h_im = lax.fori_loop(0, steps, step, (hre_ref[rs, ls], him_ref[rs, ls]),
                                       unroll=True)
            hre_ref[rs, ls] = h_re
            him_ref[rs, ls] = h_im

    for j in range(S5_NSUPER):
        st = slice(j * S5_SUPER_ST, (j + 1) * S5_SUPER_ST)
        ch = slice(j * S5_SUPER_CH, (j + 1) * S5_SUPER_CH)
        y = (_dot(sre_ref[:, st].astype(BF16), wc_ref[j, 0:S5_SUPER_ST, :])
             + _dot(sim_ref[:, st].astype(BF16), wc_ref[j, S5_SUPER_ST:, :]))
        z_ref[:, ch] = (y + d_ref[:, ch] * u_ref[:, ch]).astype(BF16)
    z = _dot(unperm_ref[...], z_ref[...]).astype(BF16)
    zz = _dot(z, wglu_ref[...]) + bglu_ref[...]
    out = zz[:, :D_MODEL] * _sigmoid(zz[:, D_MODEL:])
    o_ref[...] = (x + _rms(out, g_ref[1:2, :])).reshape(o_ref.shape)


def _s5(x, steps_per_block, g2, prep, d_skip, w_glu, b_glu, h0_re, h0_im):
    wb, wc, a_re, a_im = prep
    bsz, t, _ = x.shape
    rows = steps_per_block * bsz
    r = np.arange(rows)
    perm = np.zeros((rows, rows), np.float32)
    perm[(r % steps_per_block) * bsz + r // steps_per_block, r] = 1.0
    zero2 = lambda i: (0, 0)
    zero3 = lambda i: (0, 0, 0)
    resident = dict(pipeline_mode=pl.Buffered(1))
    state = pl.BlockSpec((bsz, S5_FLAT), zero2)
    if steps_per_block == t:
        x = x.reshape(1, rows, D_MODEL)
        xspec = pl.BlockSpec((1, rows, D_MODEL), lambda i: (0, 0, 0))
    else:
        assert steps_per_block % SUBLANE == 0
        xspec = pl.BlockSpec((bsz, steps_per_block, D_MODEL), lambda i: (0, i, 0))
    y, h_re, h_im = pl.pallas_call(
        functools.partial(_s5_kernel, bsz=bsz, steps=steps_per_block),
        grid=(t // steps_per_block,),
        in_specs=[
            xspec,
            pl.BlockSpec((2, D_MODEL), zero2),
            pl.BlockSpec(wb.shape, zero3, **resident),
            pl.BlockSpec(wc.shape, zero3, **resident),
            pl.BlockSpec((1, S5_FLAT), zero2),
            pl.BlockSpec((1, S5_FLAT), zero2),
            pl.BlockSpec((1, D_MODEL), zero2),
            pl.BlockSpec((D_MODEL, 2 * D_MODEL), zero2, **resident),
            pl.BlockSpec((1, 2 * D_MODEL), zero2),
            state, state,
            pl.BlockSpec((rows, rows), zero2),
            pl.BlockSpec((rows, rows), zero2),
        ],
        out_specs=[xspec, state, state],
        out_shape=[
            jax.ShapeDtypeStruct(x.shape, F32),
            jax.ShapeDtypeStruct((bsz, S5_FLAT), F32),
            jax.ShapeDtypeStruct((bsz, S5_FLAT), F32),
        ],
        scratch_shapes=[
            pltpu.VMEM((rows, D_MODEL), F32),
            pltpu.VMEM((rows, S5_FLAT), F32),
            pltpu.VMEM((rows, S5_FLAT), F32),
            pltpu.VMEM((rows, D_MODEL), BF16),
        ],
        compiler_params=pltpu.CompilerParams(
            dimension_semantics=("arbitrary",), vmem_limit_bytes=VMEM_LIMIT),
        name="s5",
    )(x, g2, wb, wc, a_re, a_im, d_skip.reshape(1, D_MODEL), w_glu.astype(BF16),
      b_glu.reshape(1, 2 * D_MODEL), h0_re, h0_im,
      jnp.asarray(perm, BF16), jnp.asarray(perm.T, BF16))
    return y.reshape(bsz, t, D_MODEL), h_re, h_im


def kernel(x_prompt, x_sample, state_gla, state_s5_re, state_s5_im, norm_g, w_ffn_gu, w_ffn_down,
           gla_w_in, gla_w_g2, gla_b_g, gla_g_onorm, gla_w_out,
           s5_lam_re, s5_lam_im, s5_log_dt, s5_b_re, s5_b_im, s5_c_re, s5_c_im, s5_d, s5_w_glu,
           s5_b_glu):
    pb, pt, _ = x_prompt.shape
    sb, st, _ = x_sample.shape

    def ffn(x, layer, which):
        g2 = norm_g[layer, 4 * which:4 * which + 2]
        return _ffn(x, g2, w_ffn_gu[layer, which].astype(BF16), w_ffn_down[layer, which].astype(BF16))

    gla_w = _gla_weights(gla_w_in[0], gla_w_g2[0], gla_b_g[0], gla_g_onorm[0], gla_w_out[0])
    xp = ffn(x_prompt, 0, 0)
    xs = ffn(x_sample, 0, 0)
    xp, gla_p = _gla_prompt(xp, norm_g[0, 2:4], gla_w)
    xs, gla_s = _gla_sample(xs, state_gla.reshape(sb, GLA_HEADS, GLA_DK, GLA_DV), norm_g[0, 2:4], gla_w)
    xp = ffn(xp, 0, 1)
    xs = ffn(xs, 0, 1)

    xp = ffn(xp, 1, 0)
    xs = ffn(xs, 1, 0)
    prep = _s5_prep(s5_lam_re[0], s5_lam_im[0], s5_log_dt[0], s5_b_re[0], s5_b_im[0],
                    s5_c_re[0], s5_c_im[0])
    zeros = jnp.zeros((pb, S5_FLAT), F32)
    s5_args = (norm_g[1, 2:4], prep, s5_d[0], s5_w_glu[0], s5_b_glu[0])
    xp, hre_p, him_p = _s5(xp, S5_STEPS, *s5_args, zeros, zeros)
    xs, hre_s, him_s = _s5(xs, st, *s5_args,
                           state_s5_re.reshape(sb, S5_FLAT), state_s5_im.reshape(sb, S5_FLAT))
    y_prompt = ffn(xp, 1, 1)
    y_sample = ffn(xs, 1, 1)

    def s5_state(h, b):
        return h.reshape(1, b, S5_GROUPS, S5_STATE)

    return (y_prompt, y_sample, gla_p.reshape(1, pb, GLA_HEADS, GLA_DK, GLA_DV),
            s5_state(hre_p, pb), s5_state(him_p, pb),
            gla_s.reshape(1, sb, GLA_HEADS, GLA_DK, GLA_DV),
            s5_state(hre_s, sb), s5_state(him_s, sb))
```

```python
import functools
import math

import jax
import jax.numpy as jnp
import numpy as np
from jax import lax
from jax.experimental import pallas as pl
from jax.experimental.pallas import tpu as pltpu

F32 = jnp.float32
BF16 = jnp.bfloat16

D_MODEL = 1024
D_FF = 2816
GLA_HEADS = 4
GLA_DK = 128
GLA_DV = 256
GLA_KEY_DIM = GLA_HEADS * GLA_DK
GLA_VAL_DIM = GLA_HEADS * GLA_DV
GATE_RANK = 16
GATE_TAU = 16.0
S5_GROUP = 16
S5_GROUPS = 64
S5_STATE = 64
S5_FLAT = S5_GROUPS * S5_STATE
S5_SUPER = 8
S5_NSUPER = S5_GROUPS // S5_SUPER
S5_SUPER_CH = S5_SUPER * S5_GROUP
S5_SUPER_ST = S5_SUPER * S5_STATE
EPS = 1e-6
LANE = 128
SUBLANE = 8
VMEM_LIMIT = 56 * 1024 * 1024

FFN_ROWS = 512
FFN_CHUNK = 256
GLA_CHUNK = 64
GLA_TBLOCK = 256
GLA_PROJ_CHUNK = 256
LOG2E = math.log2(math.e)
GLA_SAMPLE_SEQS = 8
S5_LANES = 512
S5_STEPS = 32


def _dot(a, b):
    return jnp.dot(a, b, preferred_element_type=F32)


def _dot_nt(a, b):
    return lax.dot_general(a, b, (((1,), (1,)), ((), ())), preferred_element_type=F32)


def _rms(x, g):
    ms = jnp.mean(x * x, axis=-1, keepdims=True)
    return x * lax.rsqrt(ms + EPS) * g


def _sigmoid(x):
    return 1.0 / (1.0 + jnp.exp(-x))


def _split_bf16(x):
    hi = x.astype(BF16)
    lo = (x - hi.astype(F32)).astype(BF16)
    return hi, lo


def _ffn_kernel(x_ref, g_ref, wgu_ref, wd_ref, o_ref, act_ref):
    x = x_ref[...]
    xn = _rms(x, g_ref[0:1, :]).astype(BF16)
    for c in range(D_FF // FFN_CHUNK):
        lo = c * FFN_CHUNK
        gate = _dot(xn, wgu_ref[:, lo:lo + FFN_CHUNK])
        up = _dot(xn, wgu_ref[:, D_FF + lo:D_FF + lo + FFN_CHUNK])
        act_ref[:, lo:lo + FFN_CHUNK] = (gate * _sigmoid(gate) * up).astype(BF16)
    y = _dot(act_ref[...], wd_ref[...])
    o_ref[...] = x + 0.5 * _rms(y, g_ref[1:2, :])


def _ffn(x, g2, wgu, wd, layer, which):
    shape = x.shape
    x = x.reshape(-1, D_MODEL)
    n = x.shape[0]
    tm = min(FFN_ROWS, n)
    assert n % tm == 0
    zero = lambda i: (0, 0)
    pick = lambda i: (layer, which, 0, 0)
    resident = dict(pipeline_mode=pl.Buffered(1))
    return pl.pallas_call(
        _ffn_kernel,
        grid=(n // tm,),
        in_specs=[
            pl.BlockSpec((tm, D_MODEL), lambda i: (i, 0)),
            pl.BlockSpec((2, D_MODEL), zero),
            pl.BlockSpec((None, None, D_MODEL, 2 * D_FF), pick, **resident),
            pl.BlockSpec((None, None, D_FF, D_MODEL), pick, **resident),
        ],
        out_specs=pl.BlockSpec((tm, D_MODEL), lambda i: (i, 0)),
        out_shape=jax.ShapeDtypeStruct((n, D_MODEL), F32),
        scratch_shapes=[pltpu.VMEM((tm, D_FF), BF16)],
        compiler_params=pltpu.CompilerParams(
            dimension_semantics=("arbitrary",), vmem_limit_bytes=VMEM_LIMIT),
        name="ffn",
    )(x, g2, wgu, wd).reshape(shape)


def _gla_tables(rows, seq):
    t = np.arange(rows)[:, None]
    r = np.arange(rows)[None, :]
    cum = ((t // seq) == (r // seq)) & (r <= t)
    masks = [t == r]
    m = seq // 2
    while m >= 1:
        masks.append(((t // (2 * m)) == (r // (2 * m))) & (t % (2 * m) >= m) & (r % (2 * m) < m))
        m //= 2
    return jnp.asarray(cum.astype(np.float32), BF16), jnp.asarray(np.stack(masks).astype(np.float32))


def _gla_project_cols(hn_ref, win_ref, proj_ref, c0):
    proj_ref[:, c0:c0 + GLA_PROJ_CHUNK] = _dot(hn_ref[...], win_ref[:, c0:c0 + GLA_PROJ_CHUNK])


def _gla_project(x, g_ref, win_ref, wglr_ref, wg2_ref, bg_ref, cum_ref, hn_ref, proj_ref, lf_ref, b_ref):
    hn = _rms(x, g_ref[0:1, :]).astype(BF16)
    hn_ref[...] = hn
    glr = _dot(hn, wglr_ref[...]).astype(BF16)
    gate = _dot(glr, wg2_ref[...]) + bg_ref[...]
    for c0 in range(0, 2 * GLA_KEY_DIM, GLA_PROJ_CHUNK):
        _gla_project_cols(hn_ref, win_ref, proj_ref, c0)
    lf = (jnp.minimum(gate, 0.0) - jnp.log1p(jnp.exp(-jnp.abs(gate)))) * (1.0 / GATE_TAU)
    lf_ref[...] = lf
    hi, lo = _split_bf16(lf)
    b_ref[...] = _dot(cum_ref[...], hi) + _dot(cum_ref[...], lo)


def _block_rows(ref, r0, col, rows, period, offset):
    def bc(row, n):
        tile = ref[pl.ds(r0 + row // SUBLANE * SUBLANE, SUBLANE), col]
        return jnp.broadcast_to(tile[row % SUBLANE:row % SUBLANE + 1, :], (n, GLA_DK))

    if period >= SUBLANE:
        parts = [bc(p0 + offset, period) for p0 in range(0, rows, period)]
    else:
        assert 2 * period == SUBLANE
        low = lax.broadcasted_iota(jnp.int32, (SUBLANE, 1), 0) < period
        parts = [jnp.where(low, bc(t0 + offset, SUBLANE), bc(t0 + period + offset, SUBLANE))
                 for t0 in range(0, rows, SUBLANE)]
    return parts[0] if len(parts) == 1 else jnp.concatenate(parts, axis=0)


def _gla_attention(qs, k, b, lf, b_ref, r0, col, rows, seq, mask_ref):
    qb, kb = qs.astype(BF16), k.astype(BF16)
    att = _dot_nt(qb, kb) * mask_ref[0]
    lvl, m = 1, seq // 2
    while m >= 1:
        if m > 1:
            d = b - _block_rows(b_ref, r0, col, rows, 2 * m, m - 1)
            w = jnp.exp2(jnp.abs(d) * (-LOG2E))
        else:
            odd = lax.broadcasted_iota(jnp.int32, (rows, 1), 0) % 2 == 1
            w = jnp.where(odd, jnp.exp(lf), 1.0)
        wb = w.astype(BF16)
        att = att + _dot_nt(qb * wb, kb * wb) * mask_ref[lvl]
        lvl, m = lvl + 1, m // 2
    return att


def _gla_finish_head(o, r, gon_ref):
    on = _rms(o, gon_ref[...])
    return (on * (r * _sigmoid(r))).astype(BF16)


def _gla_prompt_kernel(x_ref, g_ref, win_ref, wglr_ref, wg2_ref, bg_ref, gon_ref, wout_ref,
                       cum_ref, mask_ref, o_ref, snew_ref,
                       hn_ref, proj_ref, lf_ref, b_ref, og_ref, s_ref, att_ref, qb_ref, tt_ref):
    tb = pl.program_id(1)
    c = GLA_CHUNK

    @pl.when(tb == 0)
    def _():
        s_ref[...] = jnp.zeros_like(s_ref)

    x = x_ref[0]
    _gla_project(x, g_ref, win_ref, wglr_ref, wg2_ref, bg_ref, cum_ref, hn_ref, proj_ref, lf_ref, b_ref)

    late_cols = list(range(2 * GLA_KEY_DIM, proj_ref.shape[1], GLA_PROJ_CHUNK))
    n_heads_total = GLA_TBLOCK // c * GLA_HEADS
    for ci, r0 in enumerate(range(0, GLA_TBLOCK, c)):
        rows = slice(r0, r0 + c)
        for h in range(GLA_HEADS):
            done = ci * GLA_HEADS + h
            for c0 in late_cols[done * len(late_cols) // n_heads_total:
                                (done + 1) * len(late_cols) // n_heads_total]:
                _gla_project_cols(hn_ref, win_ref, proj_ref, c0)
            kcol = slice(h * GLA_DK, (h + 1) * GLA_DK)
            qs = proj_ref[rows, kcol] * (GLA_DK ** -0.5)
            k = proj_ref[rows, GLA_KEY_DIM + h * GLA_DK:GLA_KEY_DIM + (h + 1) * GLA_DK]
            b = b_ref[rows, kcol]
            att = _gla_attention(qs, k, b, lf_ref[rows, kcol], b_ref, r0, kcol, c, c, mask_ref)
            att_ref[ci * GLA_HEADS + h] = att.astype(BF16)
            qb_ref[rows, kcol] = (qs * jnp.exp(b)).astype(BF16)
            khat = k * jnp.exp(_block_rows(b_ref, r0, kcol, c, c, c - 1) - b)
            tile = jnp.concatenate(
                [khat, jnp.exp(b[c - SUBLANE:c, :]), jnp.zeros((LANE - c - SUBLANE, GLA_DK), F32)], axis=0)
            tt_ref[ci * GLA_HEADS + h] = jnp.transpose(tile)

    for ci, r0 in enumerate(range(0, GLA_TBLOCK, c)):
        rows = slice(r0, r0 + c)
        for h in range(GLA_HEADS):
            kcol = slice(h * GLA_DK, (h + 1) * GLA_DK)
            vcol = slice(h * GLA_DV, (h + 1) * GLA_DV)
            v = proj_ref[rows, 2 * GLA_KEY_DIM + h * GLA_DV:2 * GLA_KEY_DIM + (h + 1) * GLA_DV].astype(BF16)
            r = proj_ref[rows, 2 * GLA_KEY_DIM + GLA_VAL_DIM + h * GLA_DV:
                         2 * GLA_KEY_DIM + GLA_VAL_DIM + (h + 1) * GLA_DV]
            s = s_ref[h]
            tt = tt_ref[ci * GLA_HEADS + h]
            o = _dot(att_ref[ci * GLA_HEADS + h], v) + _dot(qb_ref[rows, kcol], s.astype(BF16))
            decay = tt[:, c + SUBLANE - 1:c + SUBLANE]
            s_ref[h] = s * decay + _dot(tt[:, 0:c].astype(BF16), v)
            og_ref[rows, vcol] = _gla_finish_head(o, r, gon_ref)
    out = _dot(og_ref[...], wout_ref[...])
    o_ref[0] = x + _rms(out, g_ref[1:2, :])

    @pl.when(tb == pl.num_programs(1) - 1)
    def _():
        snew_ref[0] = s_ref[...]


def _gla_sample_kernel(x_ref, g_ref, win_ref, wglr_ref, wg2_ref, bg_ref, gon_ref, wout_ref,
                       cum_ref, mask_ref, s0_ref, o_ref, snew_ref,
                       hn_ref, proj_ref, lf_ref, b_ref, og_ref, *, seq):
    rows = x_ref.shape[0]
    grp = 2 * SUBLANE
    x = x_ref[...]
    _gla_project(x, g_ref, win_ref, wglr_ref, wg2_ref, bg_ref, cum_ref, hn_ref, proj_ref, lf_ref, b_ref)
    for c0 in range(2 * GLA_KEY_DIM, proj_ref.shape[1], GLA_PROJ_CHUNK):
        _gla_project_cols(hn_ref, win_ref, proj_ref, c0)
    rid = lax.broadcasted_iota(jnp.int32, (grp, 1), 0)
    for h in range(GLA_HEADS):
        kcol = slice(h * GLA_DK, (h + 1) * GLA_DK)
        vcol = slice(h * GLA_DV, (h + 1) * GLA_DV)
        qs = proj_ref[:, kcol] * (GLA_DK ** -0.5)
        k = proj_ref[:, GLA_KEY_DIM + h * GLA_DK:GLA_KEY_DIM + (h + 1) * GLA_DK]
        v = proj_ref[:, 2 * GLA_KEY_DIM + h * GLA_DV:2 * GLA_KEY_DIM + (h + 1) * GLA_DV]
        r = proj_ref[:, 2 * GLA_KEY_DIM + GLA_VAL_DIM + h * GLA_DV:
                     2 * GLA_KEY_DIM + GLA_VAL_DIM + (h + 1) * GLA_DV]
        b = b_ref[:, kcol]
        att = _gla_attention(qs, k, b, lf_ref[:, kcol], b_ref, 0, kcol, rows, seq, mask_ref)
        eb = jnp.exp(b)
        qb = (qs * eb).astype(BF16)
        khat = k * jnp.exp(_block_rows(b_ref, 0, kcol, rows, seq, seq - 1) - b)
        o_intra = _dot(att.astype(BF16), v.astype(BF16))
        o_parts = []
        for gi in range(rows // grp):
            gr = slice(gi * grp, (gi + 1) * grp)
            tile = jnp.concatenate(
                [khat[gr], eb[gr], jnp.zeros((LANE - 2 * grp, GLA_DK), F32)], axis=0)
            tt = jnp.transpose(tile)
            kt = tt[:, 0:grp].astype(BF16)
            qg = qb[gr]
            vg = v[gr]
            o_g = jnp.zeros((grp, GLA_DV), F32)
            for si in range(grp // seq):
                sq = gi * (grp // seq) + si
                mine = (rid >= si * seq) & (rid < (si + 1) * seq)
                s = s0_ref[sq, h]
                o_g = o_g + jnp.where(mine, _dot(qg, s.astype(BF16)), 0.0)
                vm = jnp.where(mine, vg, 0.0).astype(BF16)
                last = grp + (si + 1) * seq - 1
                snew_ref[sq, h] = s * tt[:, last:last + 1] + _dot(kt, vm)
            o_parts.append(o_g)
        o = o_intra + jnp.concatenate(o_parts, axis=0)
        og_ref[:, vcol] = _gla_finish_head(o, r, gon_ref)
    out = _dot(og_ref[...], wout_ref[...])
    o_ref[...] = x + _rms(out, g_ref[1:2, :])


def _gla_weight_specs(nidx):
    zero = (lambda *_: (0, 0))
    resident = dict(pipeline_mode=pl.Buffered(1))
    del nidx
    return [
        pl.BlockSpec((2, D_MODEL), zero),
        pl.BlockSpec((D_MODEL, 2 * GLA_KEY_DIM + 2 * GLA_VAL_DIM), zero, **resident),
        pl.BlockSpec((D_MODEL, LANE), zero, **resident),
        pl.BlockSpec((LANE, GLA_KEY_DIM), zero, **resident),
        pl.BlockSpec((1, GLA_KEY_DIM), zero),
        pl.BlockSpec((1, GLA_DV), zero),
        pl.BlockSpec((GLA_VAL_DIM, D_MODEL), zero, **resident),
    ]


def _gla_weights(w_in, w_g2, b_g, g_onorm, w_out):
    n_main = 2 * GLA_KEY_DIM + 2 * GLA_VAL_DIM
    w_main = w_in[:, :n_main].astype(BF16)
    w_glr = jnp.pad(w_in[:, n_main:], ((0, 0), (0, LANE - GATE_RANK))).astype(BF16)
    w_g2p = jnp.pad(w_g2, ((0, LANE - GATE_RANK), (0, 0))).astype(BF16)
    return (w_main, w_glr, w_g2p, b_g.reshape(1, GLA_KEY_DIM), g_onorm.reshape(1, GLA_DV),
            w_out.astype(BF16))


def _gla_prompt(x, g2, weights):
    bsz, t, _ = x.shape
    cum, _ = _gla_tables(GLA_TBLOCK, GLA_CHUNK)
    _, masks = _gla_tables(GLA_CHUNK, GLA_CHUNK)
    n_main = 2 * GLA_KEY_DIM + 2 * GLA_VAL_DIM
    const2 = lambda b, i: (0, 0)
    return pl.pallas_call(
        _gla_prompt_kernel,
        grid=(bsz, t // GLA_TBLOCK),
        in_specs=[pl.BlockSpec((1, GLA_TBLOCK, D_MODEL), lambda b, i: (b, i, 0))]
        + _gla_weight_specs(2)
        + [pl.BlockSpec(cum.shape, const2),
           pl.BlockSpec(masks.shape, lambda b, i: (0, 0, 0))],
        out_specs=[
            pl.BlockSpec((1, GLA_TBLOCK, D_MODEL), lambda b, i: (b, i, 0)),
            pl.BlockSpec((1, GLA_HEADS, GLA_DK, GLA_DV), lambda b, i: (b, 0, 0, 0)),
        ],
        out_shape=[
            jax.ShapeDtypeStruct(x.shape, F32),
            jax.ShapeDtypeStruct((bsz, GLA_HEADS, GLA_DK, GLA_DV), F32),
        ],
        scratch_shapes=[
            pltpu.VMEM((GLA_TBLOCK, D_MODEL), BF16),
            pltpu.VMEM((GLA_TBLOCK, n_main), F32),
            pltpu.VMEM((GLA_TBLOCK, GLA_KEY_DIM), F32),
            pltpu.VMEM((GLA_TBLOCK, GLA_KEY_DIM), F32),
            pltpu.VMEM((GLA_TBLOCK, GLA_VAL_DIM), BF16),
            pltpu.VMEM((GLA_HEADS, GLA_DK, GLA_DV), F32),
            pltpu.VMEM((GLA_TBLOCK // GLA_CHUNK * GLA_HEADS, GLA_CHUNK, GLA_CHUNK), BF16),
            pltpu.VMEM((GLA_TBLOCK, GLA_KEY_DIM), BF16),
            pltpu.VMEM((GLA_TBLOCK // GLA_CHUNK * GLA_HEADS, LANE, GLA_DK), F32),
        ],
        compiler_params=pltpu.CompilerParams(
            dimension_semantics=("arbitrary", "arbitrary"), vmem_limit_bytes=VMEM_LIMIT),
        name="gla_prompt",
    )(x, g2, *weights, cum, masks)


def _gla_sample(x, s0, g2, weights):
    bsz, seq, _ = x.shape
    rows = GLA_SAMPLE_SEQS * seq
    cum, masks = _gla_tables(rows, seq)
    n_main = 2 * GLA_KEY_DIM + 2 * GLA_VAL_DIM
    state_spec = pl.BlockSpec((GLA_SAMPLE_SEQS, GLA_HEADS, GLA_DK, GLA_DV), lambda i: (i, 0, 0, 0))
    y, snew = pl.pallas_call(
        functools.partial(_gla_sample_kernel, seq=seq),
        grid=(bsz // GLA_SAMPLE_SEQS,),
        in_specs=[pl.BlockSpec((rows, D_MODEL), lambda i: (i, 0))]
        + _gla_weight_specs(1)
        + [pl.BlockSpec(cum.shape, lambda i: (0, 0)),
           pl.BlockSpec(masks.shape, lambda i: (0, 0, 0)),
           state_spec],
        out_specs=[pl.BlockSpec((rows, D_MODEL), lambda i: (i, 0)), state_spec],
        out_shape=[
            jax.ShapeDtypeStruct((bsz * seq, D_MODEL), F32),
            jax.ShapeDtypeStruct(s0.shape, F32),
        ],
        scratch_shapes=[
            pltpu.VMEM((rows, D_MODEL), BF16),
            pltpu.VMEM((rows, n_main), F32),
            pltpu.VMEM((rows, GLA_KEY_DIM), F32),
            pltpu.VMEM((rows, GLA_KEY_DIM), F32),
            pltpu.VMEM((rows, GLA_VAL_DIM), BF16),
        ],
        compiler_params=pltpu.CompilerParams(
            dimension_semantics=("arbitrary",), vmem_limit_bytes=VMEM_LIMIT),
        name="gla_sample",
    )(x.reshape(bsz * seq, D_MODEL), g2, *weights, cum, masks, s0)
    return y.reshape(x.shape), snew


def _s5_discretize(lam_re, lam_im, log_dt):
    dt = jnp.exp(log_dt)
    mag = jnp.exp(lam_re * dt)
    ang = lam_im * dt
    a_re, a_im = mag * jnp.cos(ang), mag * jnp.sin(ang)
    nr, ni = a_re - 1.0, a_im
    den = lam_re * lam_re + lam_im * lam_im
    f_re = (nr * lam_re + ni * lam_im) / den
    f_im = (ni * lam_re - nr * lam_im) / den
    return a_re, a_im, f_re, f_im


def _s5_prep_kernel(lre_ref, lim_ref, ldt_ref, bre_ref, bim_ref, cre_ref, cim_ref,
                    lref_ref, limf_ref, ldtf_ref, wb_ref, wc_ref, are_ref, aim_ref):
    _, _, f_re, f_im = _s5_discretize(lre_ref[...], lim_ref[...], ldt_ref[...])
    b_re, b_im = bre_ref[...], bim_ref[...]
    row = lax.broadcasted_iota(jnp.int32, (S5_SUPER_CH, S5_SUPER_ST), 0)
    col = lax.broadcasted_iota(jnp.int32, (S5_SUPER_CH, S5_SUPER_ST), 1)
    own = (row // S5_GROUP) == (col // S5_STATE)
    wb_ref[0, :, 0:S5_SUPER_ST] = jnp.where(own, f_re * b_re - f_im * b_im, 0.0).astype(BF16)
    wb_ref[0, :, S5_SUPER_ST:] = jnp.where(own, f_re * b_im + f_im * b_re, 0.0).astype(BF16)
    rowc = lax.broadcasted_iota(jnp.int32, (S5_SUPER_ST, S5_SUPER_CH), 0)
    colc = lax.broadcasted_iota(jnp.int32, (S5_SUPER_ST, S5_SUPER_CH), 1)
    ownc = (rowc // S5_STATE) == (colc // S5_GROUP)
    wc_ref[0, 0:S5_SUPER_ST, :] = jnp.where(ownc, cre_ref[...], 0.0).astype(BF16)
    wc_ref[0, S5_SUPER_ST:, :] = jnp.where(ownc, -cim_ref[...], 0.0).astype(BF16)
    a_re, a_im, _, _ = _s5_discretize(lref_ref[...], limf_ref[...], ldtf_ref[...])
    are_ref[...] = a_re
    aim_ref[...] = a_im


def _s5_prep(lam_re, lam_im, log_dt, b_re, b_im, c_re, c_im):
    g, p, ch, sg = S5_GROUPS, S5_STATE, S5_GROUP, S5_SUPER

    def rows_by_group(a_gp):
        return jnp.broadcast_to(a_gp[:, None, None, :], (g, ch, sg, p)).reshape(g * ch, sg * p)

    def b_layout(b):
        bt = jnp.transpose(b, (0, 2, 1))
        return jnp.broadcast_to(bt[:, :, None, :], (g, ch, sg, p)).reshape(g * ch, sg * p)

    def c_layout(cm):
        ct = jnp.transpose(cm.reshape(g // sg, sg, ch, p), (3, 0, 1, 2)).reshape(p, g * ch)
        return jnp.broadcast_to(ct[None], (sg, p, g * ch)).reshape(sg * p, g * ch)

    ldt_gp = jnp.broadcast_to(log_dt[:, None], (g, p))
    big = pl.BlockSpec((S5_SUPER_CH, S5_SUPER_ST), lambda j: (j, 0))
    cspec = pl.BlockSpec((S5_SUPER_ST, S5_SUPER_CH), lambda j: (0, j))
    flat = pl.BlockSpec((1, S5_SUPER_ST), lambda j: (0, j))
    return pl.pallas_call(
        _s5_prep_kernel,
        grid=(S5_NSUPER,),
        in_specs=[big] * 5 + [cspec] * 2 + [flat] * 3,
        out_specs=[
            pl.BlockSpec((1, S5_SUPER_CH, 2 * S5_SUPER_ST), lambda j: (j, 0, 0)),
            pl.BlockSpec((1, 2 * S5_SUPER_ST, S5_SUPER_CH), lambda j: (j, 0, 0)),
            flat, flat,
        ],
        out_shape=[
            jax.ShapeDtypeStruct((S5_NSUPER, S5_SUPER_CH, 2 * S5_SUPER_ST), BF16),
            jax.ShapeDtypeStruct((S5_NSUPER, 2 * S5_SUPER_ST, S5_SUPER_CH), BF16),
            jax.ShapeDtypeStruct((1, S5_FLAT), F32),
            jax.ShapeDtypeStruct((1, S5_FLAT), F32),
        ],
        compiler_params=pltpu.CompilerParams(dimension_semantics=("arbitrary",)),
        name="s5_prep",
    )(rows_by_group(lam_re), rows_by_group(lam_im), rows_by_group(ldt_gp),
      b_layout(b_re), b_layout(b_im), c_layout(c_re), c_layout(c_im),
      lam_re.reshape(1, S5_FLAT), lam_im.reshape(1, S5_FLAT), ldt_gp.reshape(1, S5_FLAT))


def _s5_kernel(x_ref, g_ref, wb_ref, wc_ref, are_ref, aim_ref, d_ref, wglu_ref, bglu_ref,
               h0re_ref, h0im_ref, perm_ref, unperm_ref, o_ref, hre_ref, him_ref,
               u_ref, sre_ref, sim_ref, z_ref, *, bsz, steps):
    i = pl.program_id(0)
    rows = bsz * steps

    @pl.when(i == 0)
    def _():
        hre_ref[...] = h0re_ref[...]
        him_ref[...] = h0im_ref[...]

    x = x_ref[...].reshape(rows, D_MODEL)
    u = _dot(perm_ref[...], _rms(x, g_ref[0:1, :]).astype(BF16))
    u_ref[...] = u
    ub = u.astype(BF16)
    for j in range(S5_NSUPER):
        bu = _dot(ub[:, j * S5_SUPER_CH:(j + 1) * S5_SUPER_CH], wb_ref[j])
        sre_ref[:, j * S5_SUPER_ST:(j + 1) * S5_SUPER_ST] = bu[:, :S5_SUPER_ST]
        sim_ref[:, j * S5_SUPER_ST:(j + 1) * S5_SUPER_ST] = bu[:, S5_SUPER_ST:]

    for lc in range(S5_FLAT // S5_LANES):
        ls = slice(lc * S5_LANES, (lc + 1) * S5_LANES)
        a_re = jnp.broadcast_to(are_ref[:, ls], (SUBLANE, S5_LANES))
        a_im = jnp.broadcast_to(aim_ref[:, ls], (SUBLANE, S5_LANES))
        for rt in range(bsz // SUBLANE):
            rs = slice(rt * SUBLANE, (rt + 1) * SUBLANE)

            def step(t, carry, ls=ls, rt=rt, a_re=a_re, a_im=a_im):
                h_re, h_im = carry
                rows = pl.ds(pl.multiple_of(t * bsz + rt * SUBLANE, SUBLANE), SUBLANE)
                n_re = a_re * h_re - a_im * h_im + sre_ref[rows, ls]
                n_im = a_re * h_im + a_im * h_re + sim_ref[rows, ls]
                sre_ref[rows, ls] = n_re
                sim_ref[rows, ls] = n_im
                return n_re, n_im

            h_re, h_im = lax.fori_loop(0, steps, step, (hre_ref[rs, ls], him_ref[rs, ls]),
                                       unroll=True)
            hre_ref[rs, ls] = h_re
            him_ref[rs, ls] = h_im

    for j in range(S5_NSUPER):
        st = slice(j * S5_SUPER_ST, (j + 1) * S5_SUPER_ST)
        ch = slice(j * S5_SUPER_CH, (j + 1) * S5_SUPER_CH)
        y = (_dot(sre_ref[:, st].astype(BF16), wc_ref[j, 0:S5_SUPER_ST, :])
             + _dot(sim_ref[:, st].astype(BF16), wc_ref[j, S5_SUPER_ST:, :]))
        z_ref[:, ch] = (y + d_ref[:, ch] * u_ref[:, ch]).astype(BF16)
    z = _dot(unperm_ref[...], z_ref[...]).astype(BF16)
    zz = _dot(z, wglu_ref[...]) + bglu_ref[...]
    out = zz[:, :D_MODEL] * _sigmoid(zz[:, D_MODEL:])
    o_ref[...] = (x + _rms(out, g_ref[1:2, :])).reshape(o_ref.shape)


def _s5(x, steps_per_block, g2, prep, d_skip, w_glu, b_glu, h0_re, h0_im):
    wb, wc, a_re, a_im = prep
    bsz, t, _ = x.shape
    rows = steps_per_block * bsz
    r = np.arange(rows)
    perm = np.zeros((rows, rows), np.float32)
    perm[(r % steps_per_block) * bsz + r // steps_per_block, r] = 1.0
    zero2 = lambda i: (0, 0)
    zero3 = lambda i: (0, 0, 0)
    resident = dict(pipeline_mode=pl.Buffered(1))
    state = pl.BlockSpec((bsz, S5_FLAT), zero2)
    if steps_per_block == t:
        x = x.reshape(1, rows, D_MODEL)
        xspec = pl.BlockSpec((1, rows, D_MODEL), lambda i: (0, 0, 0))
    else:
        assert steps_per_block % SUBLANE == 0
        xspec = pl.BlockSpec((bsz, steps_per_block, D_MODEL), lambda i: (0, i, 0))
    y, h_re, h_im = pl.pallas_call(
        functools.partial(_s5_kernel, bsz=bsz, steps=steps_per_block),
        grid=(t // steps_per_block,),
        in_specs=[
            xspec,
            pl.BlockSpec((2, D_MODEL), zero2),
            pl.BlockSpec(wb.shape, zero3, **resident),
            pl.BlockSpec(wc.shape, zero3, **resident),
            pl.BlockSpec((1, S5_FLAT), zero2),
            pl.BlockSpec((1, S5_FLAT), zero2),
            pl.BlockSpec((1, D_MODEL), zero2),
            pl.BlockSpec((D_MODEL, 2 * D_MODEL), zero2, **resident),
            pl.BlockSpec((1, 2 * D_MODEL), zero2),
            state, state,
            pl.BlockSpec((rows, rows), zero2),
            pl.BlockSpec((rows, rows), zero2),
        ],
        out_specs=[xspec, state, state],
        out_shape=[
            jax.ShapeDtypeStruct(x.shape, F32),
            jax.ShapeDtypeStruct((bsz, S5_FLAT), F32),
            jax.ShapeDtypeStruct((bsz, S5_FLAT), F32),
        ],
        scratch_shapes=[
            pltpu.VMEM((rows, D_MODEL), F32),
            pltpu.VMEM((rows, S5_FLAT), F32),
            pltpu.VMEM((rows, S5_FLAT), F32),
            pltpu.VMEM((rows, D_MODEL), BF16),
        ],
        compiler_params=pltpu.CompilerParams(
            dimension_semantics=("arbitrary",), vmem_limit_bytes=VMEM_LIMIT),
        name="s5",
    )(x, g2, wb, wc, a_re, a_im, d_skip.reshape(1, D_MODEL), w_glu.astype(BF16),
      b_glu.reshape(1, 2 * D_MODEL), h0_re, h0_im,
      jnp.asarray(perm, BF16), jnp.asarray(perm.T, BF16))
    return y.reshape(bsz, t, D_MODEL), h_re, h_im


def kernel(x_prompt, x_sample, state_gla, state_s5_re, state_s5_im, norm_g, w_ffn_gu, w_ffn_down,
           gla_w_in, gla_w_g2, gla_b_g, gla_g_onorm, gla_w_out,
           s5_lam_re, s5_lam_im, s5_log_dt, s5_b_re, s5_b_im, s5_c_re, s5_c_im, s5_d, s5_w_glu,
           s5_b_glu):
    pb, pt, _ = x_prompt.shape
    sb, st, _ = x_sample.shape
    wgu = w_ffn_gu.astype(BF16)
    wd = w_ffn_down.astype(BF16)

    def ffn(x, layer, which):
        g2 = norm_g[layer, 4 * which:4 * which + 2]
        return _ffn(x, g2, wgu, wd, layer, which)

    gla_w = _gla_weights(gla_w_in[0], gla_w_g2[0], gla_b_g[0], gla_g_onorm[0], gla_w_out[0])
    xp = ffn(x_prompt, 0, 0)
    xs = ffn(x_sample, 0, 0)
    xp, gla_p = _gla_prompt(xp, norm_g[0, 2:4], gla_w)
    xs, gla_s = _gla_sample(xs, state_gla.reshape(sb, GLA_HEADS, GLA_DK, GLA_DV), norm_g[0, 2:4], gla_w)
    xp = ffn(xp, 0, 1)
    xs = ffn(xs, 0, 1)

    xp = ffn(xp, 1, 0)
    xs = ffn(xs, 1, 0)
    prep = _s5_prep(s5_lam_re[0], s5_lam_im[0], s5_log_dt[0], s5_b_re[0], s5_b_im[0],
                    s5_c_re[0], s5_c_im[0])
    zeros = jnp.zeros((pb, S5_FLAT), F32)
    s5_args = (norm_g[1, 2:4], prep, s5_d[0], s5_w_glu[0], s5_b_glu[0])
    xp, hre_p, him_p = _s5(xp, S5_STEPS, *s5_args, zeros, zeros)
    xs, hre_s, him_s = _s5(xs, st, *s5_args,
                           state_s5_re.reshape(sb, S5_FLAT), state_s5_im.reshape(sb, S5_FLAT))
    y_prompt = ffn(xp, 1, 1)
    y_sample = ffn(xs, 1, 1)

    def s5_state(h, b):
        return h.reshape(1, b, S5_GROUPS, S5_STATE)

    return (y_prompt, y_sample, gla_p.reshape(1, pb, GLA_HEADS, GLA_DK, GLA_DV),
            s5_state(hre_p, pb), s5_state(him_p, pb),
            gla_s.reshape(1, sb, GLA_HEADS, GLA_DK, GLA_DV),
            s5_state(hre_s, sb), s5_state(him_s, sb))
```

```python
import functools
import math

import jax
import jax.numpy as jnp
import numpy as np
from jax import lax
from jax.experimental import pallas as pl
from jax.experimental.pallas import tpu as pltpu

F32 = jnp.float32
BF16 = jnp.bfloat16

D_MODEL = 1024
D_FF = 2816
GLA_HEADS = 4
GLA_DK = 128
GLA_DV = 256
GLA_KEY_DIM = GLA_HEADS * GLA_DK
GLA_VAL_DIM = GLA_HEADS * GLA_DV
GATE_RANK = 16
GATE_TAU = 16.0
S5_GROUP = 16
S5_GROUPS = 64
S5_STATE = 64
S5_FLAT = S5_GROUPS * S5_STATE
S5_SUPER = 8
S5_NSUPER = S5_GROUPS // S5_SUPER
S5_SUPER_CH = S5_SUPER * S5_GROUP
S5_SUPER_ST = S5_SUPER * S5_STATE
EPS = 1e-6
LANE = 128
SUBLANE = 8
VMEM_LIMIT = 56 * 1024 * 1024

FFN_ROWS = 1024
FFN_GROUP = 512
FFN_CHUNK = 256
GLA_CHUNK = 64
GLA_TBLOCK = 256
GLA_PROJ_CHUNK = 256
LOG2E = math.log2(math.e)
GLA_SAMPLE_SEQS = 8
S5_LANES = 512
S5_STEPS = 32


def _dot(a, b):
    return jnp.dot(a, b, preferred_element_type=F32)


def _dot_nt(a, b):
    return lax.dot_general(a, b, (((1,), (1,)), ((), ())), preferred_element_type=F32)


def _rms(x, g):
    ms = jnp.mean(x * x, axis=-1, keepdims=True)
    return x * lax.rsqrt(ms + EPS) * g


def _sigmoid(x):
    return 1.0 / (1.0 + jnp.exp(-x))


def _split_bf16(x):
    hi = x.astype(BF16)
    lo = (x - hi.astype(F32)).astype(BF16)
    return hi, lo


def _ffn_row_groups(n_rows):
    return [slice(r0, min(r0 + FFN_GROUP, n_rows)) for r0 in range(0, n_rows, FFN_GROUP)]


def _ffn_hidden_chunk(xn_ref, wgu_ref, act_ref, rows, c):
    lo = c * FFN_CHUNK
    gate = _dot(xn_ref[rows, :], wgu_ref[:, lo:lo + FFN_CHUNK])
    up = _dot(xn_ref[rows, :], wgu_ref[:, D_FF + lo:D_FF + lo + FFN_CHUNK])
    act_ref[rows, lo:lo + FFN_CHUNK] = (gate * _sigmoid(gate) * up).astype(BF16)


def _ffn_finish(x_ref, g_ref, wd_ref, o_ref, act_ref, rows):
    y = _dot(act_ref[rows, :], wd_ref[...])
    o_ref[rows, :] = x_ref[rows, :] + 0.5 * _rms(y, g_ref[1:2, :])


def _ffn_kernel(x_ref, g_ref, wgu_hbm, wd_hbm, o_ref,
                xn_ref, act_ref, wgu_ref, wd_ref, gu_stage, d_stage, sem, *, layer, which):
    first = pl.program_id(0) == 0
    n_chunks = D_FF // FFN_CHUNK
    groups = _ffn_row_groups(x_ref.shape[0])
    xn_ref[...] = _rms(x_ref[...], g_ref[0:1, :]).astype(BF16)

    def chunk_copies(c):
        slot, lo = c % 2, c * FFN_CHUNK
        return (
            pltpu.make_async_copy(wgu_hbm.at[layer, which, :, pl.ds(lo, FFN_CHUNK)],
                                  gu_stage.at[slot, 0], sem.at[slot, 0]),
            pltpu.make_async_copy(wgu_hbm.at[layer, which, :, pl.ds(D_FF + lo, FFN_CHUNK)],
                                  gu_stage.at[slot, 1], sem.at[slot, 1]),
            pltpu.make_async_copy(wd_hbm.at[layer, which, pl.ds(lo, FFN_CHUNK), :],
                                  d_stage.at[slot], sem.at[slot, 2]),
        )

    @pl.when(first)
    def _():
        for c in range(min(2, n_chunks)):
            for cp in chunk_copies(c):
                cp.start()
        for c in range(n_chunks):
            slot, lo = c % 2, c * FFN_CHUNK
            for cp in chunk_copies(c):
                cp.wait()
            wgu_ref[:, lo:lo + FFN_CHUNK] = gu_stage[slot, 0].astype(BF16)
            wgu_ref[:, D_FF + lo:D_FF + lo + FFN_CHUNK] = gu_stage[slot, 1].astype(BF16)
            wd_ref[lo:lo + FFN_CHUNK, :] = d_stage[slot].astype(BF16)
            if c + 2 < n_chunks:
                for cp in chunk_copies(c + 2):
                    cp.start()
            for rows in groups:
                _ffn_hidden_chunk(xn_ref, wgu_ref, act_ref, rows, c)
        for rows in groups:
            _ffn_finish(x_ref, g_ref, wd_ref, o_ref, act_ref, rows)

    @pl.when(jnp.logical_not(first))
    def _():
        for rows in groups:
            for c in range(n_chunks):
                _ffn_hidden_chunk(xn_ref, wgu_ref, act_ref, rows, c)
            _ffn_finish(x_ref, g_ref, wd_ref, o_ref, act_ref, rows)


def _ffn(x, g2, wgu, wd, layer, which):
    shape = x.shape
    x = x.reshape(-1, D_MODEL)
    n = x.shape[0]
    tm = min(FFN_ROWS, n)
    assert n % tm == 0
    zero = lambda i: (0, 0)
    return pl.pallas_call(
        functools.partial(_ffn_kernel, layer=layer, which=which),
        grid=(n // tm,),
        in_specs=[
            pl.BlockSpec((tm, D_MODEL), lambda i: (i, 0)),
            pl.BlockSpec((2, D_MODEL), zero),
            pl.BlockSpec(memory_space=pl.ANY),
            pl.BlockSpec(memory_space=pl.ANY),
        ],
        out_specs=pl.BlockSpec((tm, D_MODEL), lambda i: (i, 0)),
        out_shape=jax.ShapeDtypeStruct((n, D_MODEL), F32),
        scratch_shapes=[
            pltpu.VMEM((tm, D_MODEL), BF16),
            pltpu.VMEM((tm, D_FF), BF16),
            pltpu.VMEM((D_MODEL, 2 * D_FF), BF16),
            pltpu.VMEM((D_FF, D_MODEL), BF16),
            pltpu.VMEM((2, 2, D_MODEL, FFN_CHUNK), F32),
            pltpu.VMEM((2, FFN_CHUNK, D_MODEL), F32),
            pltpu.SemaphoreType.DMA((2, 3)),
        ],
        compiler_params=pltpu.CompilerParams(
            dimension_semantics=("arbitrary",), vmem_limit_bytes=VMEM_LIMIT),
        name="ffn",
    )(x, g2, wgu, wd).reshape(shape)


def _gla_tables(rows, seq):
    t = np.arange(rows)[:, None]
    r = np.arange(rows)[None, :]
    cum = ((t // seq) == (r // seq)) & (r <= t)
    masks = [t == r]
    m = seq // 2
    while m >= 1:
        masks.append(((t // (2 * m)) == (r // (2 * m))) & (t % (2 * m) >= m) & (r % (2 * m) < m))
        m //= 2
    return jnp.asarray(cum.astype(np.float32), BF16), jnp.asarray(np.stack(masks).astype(np.float32))


def _gla_project_cols(hn_ref, win_ref, proj_ref, c0):
    proj_ref[:, c0:c0 + GLA_PROJ_CHUNK] = _dot(hn_ref[...], win_ref[:, c0:c0 + GLA_PROJ_CHUNK])


def _gla_project(x, g_ref, win_ref, wglr_ref, wg2_ref, bg_ref, cum_ref, hn_ref, proj_ref, lf_ref, b_ref):
    hn = _rms(x, g_ref[0:1, :]).astype(BF16)
    hn_ref[...] = hn
    glr = _dot(hn, wglr_ref[...]).astype(BF16)
    gate = _dot(glr, wg2_ref[...]) + bg_ref[...]
    for c0 in range(0, 2 * GLA_KEY_DIM, GLA_PROJ_CHUNK):
        _gla_project_cols(hn_ref, win_ref, proj_ref, c0)
    lf = (jnp.minimum(gate, 0.0) - jnp.log1p(jnp.exp(-jnp.abs(gate)))) * (1.0 / GATE_TAU)
    lf_ref[...] = lf
    hi, lo = _split_bf16(lf)
    b_ref[...] = _dot(cum_ref[...], hi) + _dot(cum_ref[...], lo)


def _block_rows(ref, r0, col, rows, period, offset):
    def bc(row, n):
        tile = ref[pl.ds(r0 + row // SUBLANE * SUBLANE, SUBLANE), col]
        return jnp.broadcast_to(tile[row % SUBLANE:row % SUBLANE + 1, :], (n, GLA_DK))

    if period >= SUBLANE:
        parts = [bc(p0 + offset, period) for p0 in range(0, rows, period)]
    else:
        assert 2 * period == SUBLANE
        low = lax.broadcasted_iota(jnp.int32, (SUBLANE, 1), 0) < period
        parts = [jnp.where(low, bc(t0 + offset, SUBLANE), bc(t0 + period + offset, SUBLANE))
                 for t0 in range(0, rows, SUBLANE)]
    return parts[0] if len(parts) == 1 else jnp.concatenate(parts, axis=0)


def _gla_attention(qs, k, b, lf, b_ref, r0, col, rows, seq, mask_ref):
    qb, kb = qs.astype(BF16), k.astype(BF16)
    att = _dot_nt(qb, kb) * mask_ref[0]
    lvl, m = 1, seq // 2
    while m >= 1:
        if m > 1:
            d = b - _block_rows(b_ref, r0, col, rows, 2 * m, m - 1)
            w = jnp.exp2(jnp.abs(d) * (-LOG2E))
        else:
            odd = lax.broadcasted_iota(jnp.int32, (rows, 1), 0) % 2 == 1
            w = jnp.where(odd, jnp.exp(lf), 1.0)
        wb = w.astype(BF16)
        att = att + _dot_nt(qb * wb, kb * wb) * mask_ref[lvl]
        lvl, m = lvl + 1, m // 2
    return att


def _gla_finish_head(o, r, gon_ref):
    on = _rms(o, gon_ref[...])
    return (on * (r * _sigmoid(r))).astype(BF16)


def _gla_prompt_kernel(x_ref, g_ref, win_ref, wglr_ref, wg2_ref, bg_ref, gon_ref, wout_ref,
                       cum_ref, mask_ref, o_ref, snew_ref,
                       hn_ref, proj_ref, lf_ref, b_ref, og_ref, s_ref, att_ref, qb_ref, tt_ref):
    tb = pl.program_id(1)
    c = GLA_CHUNK

    @pl.when(tb == 0)
    def _():
        s_ref[...] = jnp.zeros_like(s_ref)

    x = x_ref[0]
    _gla_project(x, g_ref, win_ref, wglr_ref, wg2_ref, bg_ref, cum_ref, hn_ref, proj_ref, lf_ref, b_ref)

    late_cols = list(range(2 * GLA_KEY_DIM, proj_ref.shape[1], GLA_PROJ_CHUNK))
    n_heads_total = GLA_TBLOCK // c * GLA_HEADS
    for ci, r0 in enumerate(range(0, GLA_TBLOCK, c)):
        rows = slice(r0, r0 + c)
        for h in range(GLA_HEADS):
            done = ci * GLA_HEADS + h
            for c0 in late_cols[done * len(late_cols) // n_heads_total:
                                (done + 1) * len(late_cols) // n_heads_total]:
                _gla_project_cols(hn_ref, win_ref, proj_ref, c0)
            kcol = slice(h * GLA_DK, (h + 1) * GLA_DK)
            qs = proj_ref[rows, kcol] * (GLA_DK ** -0.5)
            k = proj_ref[rows, GLA_KEY_DIM + h * GLA_DK:GLA_KEY_DIM + (h + 1) * GLA_DK]
            b = b_ref[rows, kcol]
            att = _gla_attention(qs, k, b, lf_ref[rows, kcol], b_ref, r0, kcol, c, c, mask_ref)
            att_ref[ci * GLA_HEADS + h] = att.astype(BF16)
            qb_ref[rows, kcol] = (qs * jnp.exp(b)).astype(BF16)
            khat = k * jnp.exp(_block_rows(b_ref, r0, kcol, c, c, c - 1) - b)
            tile = jnp.concatenate(
                [khat, jnp.exp(b[c - SUBLANE:c, :]), jnp.zeros((LANE - c - SUBLANE, GLA_DK), F32)], axis=0)
            tt_ref[ci * GLA_HEADS + h] = jnp.transpose(tile)

    for ci, r0 in enumerate(range(0, GLA_TBLOCK, c)):
        rows = slice(r0, r0 + c)
        for h in range(GLA_HEADS):
            kcol = slice(h * GLA_DK, (h + 1) * GLA_DK)
            vcol = slice(h * GLA_DV, (h + 1) * GLA_DV)
            v = proj_ref[rows, 2 * GLA_KEY_DIM + h * GLA_DV:2 * GLA_KEY_DIM + (h + 1) * GLA_DV].astype(BF16)
            r = proj_ref[rows, 2 * GLA_KEY_DIM + GLA_VAL_DIM + h * GLA_DV:
                         2 * GLA_KEY_DIM + GLA_VAL_DIM + (h + 1) * GLA_DV]
            s = s_ref[h]
            tt = tt_ref[ci * GLA_HEADS + h]
            o = _dot(att_ref[ci * GLA_HEADS + h], v) + _dot(qb_ref[rows, kcol], s.astype(BF16))
            decay = tt[:, c + SUBLANE - 1:c + SUBLANE]
            s_ref[h] = s * decay + _dot(tt[:, 0:c].astype(BF16), v)
            og_ref[rows, vcol] = _gla_finish_head(o, r, gon_ref)
    out = _dot(og_ref[...], wout_ref[...])
    o_ref[0] = x + _rms(out, g_ref[1:2, :])

    @pl.when(tb == pl.num_programs(1) - 1)
    def _():
        snew_ref[0] = s_ref[...]


def _gla_sample_kernel(x_ref, g_ref, win_ref, wglr_ref, wg2_ref, bg_ref, gon_ref, wout_ref,
                       cum_ref, mask_ref, s0_ref, o_ref, snew_ref,
                       hn_ref, proj_ref, lf_ref, b_ref, og_ref, *, seq):
    rows = x_ref.shape[0]
    grp = 2 * SUBLANE
    x = x_ref[...]
    _gla_project(x, g_ref, win_ref, wglr_ref, wg2_ref, bg_ref, cum_ref, hn_ref, proj_ref, lf_ref, b_ref)
    for c0 in range(2 * GLA_KEY_DIM, proj_ref.shape[1], GLA_PROJ_CHUNK):
        _gla_project_cols(hn_ref, win_ref, proj_ref, c0)
    rid = lax.broadcasted_iota(jnp.int32, (grp, 1), 0)
    for h in range(GLA_HEADS):
        kcol = slice(h * GLA_DK, (h + 1) * GLA_DK)
        vcol = slice(h * GLA_DV, (h + 1) * GLA_DV)
        qs = proj_ref[:, kcol] * (GLA_DK ** -0.5)
        k = proj_ref[:, GLA_KEY_DIM + h * GLA_DK:GLA_KEY_DIM + (h + 1) * GLA_DK]
        v = proj_ref[:, 2 * GLA_KEY_DIM + h * GLA_DV:2 * GLA_KEY_DIM + (h + 1) * GLA_DV]
        r = proj_ref[:, 2 * GLA_KEY_DIM + GLA_VAL_DIM + h * GLA_DV:
                     2 * GLA_KEY_DIM + GLA_VAL_DIM + (h + 1) * GLA_DV]
        b = b_ref[:, kcol]
        att = _gla_attention(qs, k, b, lf_ref[:, kcol], b_ref, 0, kcol, rows, seq, mask_ref)
        eb = jnp.exp(b)
        qb = (qs * eb).astype(BF16)
        khat = k * jnp.exp(_block_rows(b_ref, 0, kcol, rows, seq, seq - 1) - b)
        o_intra = _dot(att.astype(BF16), v.astype(BF16))
        o_parts = []
        for gi in range(rows // grp):
            gr = slice(gi * grp, (gi + 1) * grp)
            tile = jnp.concatenate(
                [khat[gr], eb[gr], jnp.zeros((LANE - 2 * grp, GLA_DK), F32)], axis=0)
            tt = jnp.transpose(tile)
            kt = tt[:, 0:grp].astype(BF16)
            qg = qb[gr]
            vg = v[gr]
            o_g = jnp.zeros((grp, GLA_DV), F32)
            for si in range(grp // seq):
                sq = gi * (grp // seq) + si
                mine = (rid >= si * seq) & (rid < (si + 1) * seq)
                s = s0_ref[sq, h]
                o_g = o_g + jnp.where(mine, _dot(qg, s.astype(BF16)), 0.0)
                vm = jnp.where(mine, vg, 0.0).astype(BF16)
                last = grp + (si + 1) * seq - 1
                snew_ref[sq, h] = s * tt[:, last:last + 1] + _dot(kt, vm)
            o_parts.append(o_g)
        o = o_intra + jnp.concatenate(o_parts, axis=0)
        og_ref[:, vcol] = _gla_finish_head(o, r, gon_ref)
    out = _dot(og_ref[...], wout_ref[...])
    o_ref[...] = x + _rms(out, g_ref[1:2, :])


def _gla_weight_specs(nidx):
    zero = (lambda *_: (0, 0))
    resident = dict(pipeline_mode=pl.Buffered(1))
    del nidx
    return [
        pl.BlockSpec((2, D_MODEL), zero),
        pl.BlockSpec((D_MODEL, 2 * GLA_KEY_DIM + 2 * GLA_VAL_DIM), zero, **resident),
        pl.BlockSpec((D_MODEL, LANE), zero, **resident),
        pl.BlockSpec((LANE, GLA_KEY_DIM), zero, **resident),
        pl.BlockSpec((1, GLA_KEY_DIM), zero),
        pl.BlockSpec((1, GLA_DV), zero),
        pl.BlockSpec((GLA_VAL_DIM, D_MODEL), zero, **resident),
    ]


def _gla_weights(w_in, w_g2, b_g, g_onorm, w_out):
    n_main = 2 * GLA_KEY_DIM + 2 * GLA_VAL_DIM
    w_main = w_in[:, :n_main].astype(BF16)
    w_glr = jnp.pad(w_in[:, n_main:], ((0, 0), (0, LANE - GATE_RANK))).astype(BF16)
    w_g2p = jnp.pad(w_g2, ((0, LANE - GATE_RANK), (0, 0))).astype(BF16)
    return (w_main, w_glr, w_g2p, b_g.reshape(1, GLA_KEY_DIM), g_onorm.reshape(1, GLA_DV),
            w_out.astype(BF16))


def _gla_prompt(x, g2, weights):
    bsz, t, _ = x.shape
    cum, _ = _gla_tables(GLA_TBLOCK, GLA_CHUNK)
    _, masks = _gla_tables(GLA_CHUNK, GLA_CHUNK)
    n_main = 2 * GLA_KEY_DIM + 2 * GLA_VAL_DIM
    const2 = lambda b, i: (0, 0)
    return pl.pallas_call(
        _gla_prompt_kernel,
        grid=(bsz, t // GLA_TBLOCK),
        in_specs=[pl.BlockSpec((1, GLA_TBLOCK, D_MODEL), lambda b, i: (b, i, 0))]
        + _gla_weight_specs(2)
        + [pl.BlockSpec(cum.shape, const2),
           pl.BlockSpec(masks.shape, lambda b, i: (0, 0, 0))],
        out_specs=[
            pl.BlockSpec((1, GLA_TBLOCK, D_MODEL), lambda b, i: (b, i, 0)),
            pl.BlockSpec((1, GLA_HEADS, GLA_DK, GLA_DV), lambda b, i: (b, 0, 0, 0)),
        ],
        out_shape=[
            jax.ShapeDtypeStruct(x.shape, F32),
            jax.ShapeDtypeStruct((bsz, GLA_HEADS, GLA_DK, GLA_DV), F32),
        ],
        scratch_shapes=[
            pltpu.VMEM((GLA_TBLOCK, D_MODEL), BF16),
            pltpu.VMEM((GLA_TBLOCK, n_main), F32),
            pltpu.VMEM((GLA_TBLOCK, GLA_KEY_DIM), F32),
            pltpu.VMEM((GLA_TBLOCK, GLA_KEY_DIM), F32),
            pltpu.VMEM((GLA_TBLOCK, GLA_VAL_DIM), BF16),
            pltpu.VMEM((GLA_HEADS, GLA_DK, GLA_DV), F32),
            pltpu.VMEM((GLA_TBLOCK // GLA_CHUNK * GLA_HEADS, GLA_CHUNK, GLA_CHUNK), BF16),
            pltpu.VMEM((GLA_TBLOCK, GLA_KEY_DIM), BF16),
            pltpu.VMEM((GLA_TBLOCK // GLA_CHUNK * GLA_HEADS, LANE, GLA_DK), F32),
        ],
        compiler_params=pltpu.CompilerParams(
            dimension_semantics=("arbitrary", "arbitrary"), vmem_limit_bytes=VMEM_LIMIT),
        name="gla_prompt",
    )(x, g2, *weights, cum, masks)


def _gla_sample(x, s0, g2, weights):
    bsz, seq, _ = x.shape
    rows = GLA_SAMPLE_SEQS * seq
    cum, masks = _gla_tables(rows, seq)
    n_main = 2 * GLA_KEY_DIM + 2 * GLA_VAL_DIM
    state_spec = pl.BlockSpec((GLA_SAMPLE_SEQS, GLA_HEADS, GLA_DK, GLA_DV), lambda i: (i, 0, 0, 0))
    y, snew = pl.pallas_call(
        functools.partial(_gla_sample_kernel, seq=seq),
        grid=(bsz // GLA_SAMPLE_SEQS,),
        in_specs=[pl.BlockSpec((rows, D_MODEL), lambda i: (i, 0))]
        + _gla_weight_specs(1)
        + [pl.BlockSpec(cum.shape, lambda i: (0, 0)),
           pl.BlockSpec(masks.shape, lambda i: (0, 0, 0)),
           state_spec],
        out_specs=[pl.BlockSpec((rows, D_MODEL), lambda i: (i, 0)), state_spec],
        out_shape=[
            jax.ShapeDtypeStruct((bsz * seq, D_MODEL), F32),
            jax.ShapeDtypeStruct(s0.shape, F32),
        ],
        scratch_shapes=[
            pltpu.VMEM((rows, D_MODEL), BF16),
            pltpu.VMEM((rows, n_main), F32),
            pltpu.VMEM((rows, GLA_KEY_DIM), F32),
            pltpu.VMEM((rows, GLA_KEY_DIM), F32),
            pltpu.VMEM((rows, GLA_VAL_DIM), BF16),
        ],
        compiler_params=pltpu.CompilerParams(
            dimension_semantics=("arbitrary",), vmem_limit_bytes=VMEM_LIMIT),
        name="gla_sample",
    )(x.reshape(bsz * seq, D_MODEL), g2, *weights, cum, masks, s0)
    return y.reshape(x.shape), snew


def _s5_discretize(lam_re, lam_im, log_dt):
    dt = jnp.exp(log_dt)
    mag = jnp.exp(lam_re * dt)
    ang = lam_im * dt
    a_re, a_im = mag * jnp.cos(ang), mag * jnp.sin(ang)
    nr, ni = a_re - 1.0, a_im
    den = lam_re * lam_re + lam_im * lam_im
    f_re = (nr * lam_re + ni * lam_im) / den
    f_im = (ni * lam_re - nr * lam_im) / den
    return a_re, a_im, f_re, f_im


def _s5_prep_kernel(lre_ref, lim_ref, ldt_ref, bre_ref, bim_ref, cre_ref, cim_ref,
                    lref_ref, limf_ref, ldtf_ref, wb_ref, wc_ref, are_ref, aim_ref):
    _, _, f_re, f_im = _s5_discretize(lre_ref[...], lim_ref[...], ldt_ref[...])
    b_re, b_im = bre_ref[...], bim_ref[...]
    row = lax.broadcasted_iota(jnp.int32, (S5_SUPER_CH, S5_SUPER_ST), 0)
    col = lax.broadcasted_iota(jnp.int32, (S5_SUPER_CH, S5_SUPER_ST), 1)
    own = (row // S5_GROUP) == (col // S5_STATE)
    wb_ref[0, :, 0:S5_SUPER_ST] = jnp.where(own, f_re * b_re - f_im * b_im, 0.0).astype(BF16)
    wb_ref[0, :, S5_SUPER_ST:] = jnp.where(own, f_re * b_im + f_im * b_re, 0.0).astype(BF16)
    rowc = lax.broadcasted_iota(jnp.int32, (S5_SUPER_ST, S5_SUPER_CH), 0)
    colc = lax.broadcasted_iota(jnp.int32, (S5_SUPER_ST, S5_SUPER_CH), 1)
    ownc = (rowc // S5_STATE) == (colc // S5_GROUP)
    wc_ref[0, 0:S5_SUPER_ST, :] = jnp.where(ownc, cre_ref[...], 0.0).astype(BF16)
    wc_ref[0, S5_SUPER_ST:, :] = jnp.where(ownc, -cim_ref[...], 0.0).astype(BF16)
    a_re, a_im, _, _ = _s5_discretize(lref_ref[...], limf_ref[...], ldtf_ref[...])
    are_ref[...] = a_re
    aim_ref[...] = a_im


def _s5_prep(lam_re, lam_im, log_dt, b_re, b_im, c_re, c_im):
    g, p, ch, sg = S5_GROUPS, S5_STATE, S5_GROUP, S5_SUPER

    def rows_by_group(a_gp):
        return jnp.broadcast_to(a_gp[:, None, None, :], (g, ch, sg, p)).reshape(g * ch, sg * p)

    def b_layout(b):
        bt = jnp.transpose(b, (0, 2, 1))
        return jnp.broadcast_to(bt[:, :, None, :], (g, ch, sg, p)).reshape(g * ch, sg * p)

    def c_layout(cm):
        ct = jnp.transpose(cm.reshape(g // sg, sg, ch, p), (3, 0, 1, 2)).reshape(p, g * ch)
        return jnp.broadcast_to(ct[None], (sg, p, g * ch)).reshape(sg * p, g * ch)

    ldt_gp = jnp.broadcast_to(log_dt[:, None], (g, p))
    big = pl.BlockSpec((S5_SUPER_CH, S5_SUPER_ST), lambda j: (j, 0))
    cspec = pl.BlockSpec((S5_SUPER_ST, S5_SUPER_CH), lambda j: (0, j))
    flat = pl.BlockSpec((1, S5_SUPER_ST), lambda j: (0, j))
    return pl.pallas_call(
        _s5_prep_kernel,
        grid=(S5_NSUPER,),
        in_specs=[big] * 5 + [cspec] * 2 + [flat] * 3,
        out_specs=[
            pl.BlockSpec((1, S5_SUPER_CH, 2 * S5_SUPER_ST), lambda j: (j, 0, 0)),
            pl.BlockSpec((1, 2 * S5_SUPER_ST, S5_SUPER_CH), lambda j: (j, 0, 0)),
            flat, flat,
        ],
        out_shape=[
            jax.ShapeDtypeStruct((S5_NSUPER, S5_SUPER_CH, 2 * S5_SUPER_ST), BF16),
            jax.ShapeDtypeStruct((S5_NSUPER, 2 * S5_SUPER_ST, S5_SUPER_CH), BF16),
            jax.ShapeDtypeStruct((1, S5_FLAT), F32),
            jax.ShapeDtypeStruct((1, S5_FLAT), F32),
        ],
        compiler_params=pltpu.CompilerParams(dimension_semantics=("arbitrary",)),
        name="s5_prep",
    )(rows_by_group(lam_re), rows_by_group(lam_im), rows_by_group(ldt_gp),
      b_layout(b_re), b_layout(b_im), c_layout(c_re), c_layout(c_im),
      lam_re.reshape(1, S5_FLAT), lam_im.reshape(1, S5_FLAT), ldt_gp.reshape(1, S5_FLAT))


def _s5_kernel(x_ref, g_ref, wb_ref, wc_ref, are_ref, aim_ref, d_ref, wglu_ref, bglu_ref,
               h0re_ref, h0im_ref, perm_ref, unperm_ref, o_ref, hre_ref, him_ref,
               u_ref, sre_ref, sim_ref, z_ref, *, bsz, steps):
    i = pl.program_id(0)
    rows = bsz * steps

    @pl.when(i == 0)
    def _():
        hre_ref[...] = h0re_ref[...]
        him_ref[...] = h0im_ref[...]

    x = x_ref[...].reshape(rows, D_MODEL)
    u = _dot(perm_ref[...], _rms(x, g_ref[0:1, :]).astype(BF16))
    u_ref[...] = u
    ub = u.astype(BF16)
    for j in range(S5_NSUPER):
        bu = _dot(ub[:, j * S5_SUPER_CH:(j + 1) * S5_SUPER_CH], wb_ref[j])
        sre_ref[:, j * S5_SUPER_ST:(j + 1) * S5_SUPER_ST] = bu[:, :S5_SUPER_ST]
        sim_ref[:, j * S5_SUPER_ST:(j + 1) * S5_SUPER_ST] = bu[:, S5_SUPER_ST:]

    for lc in range(S5_FLAT // S5_LANES):
        ls = slice(lc * S5_LANES, (lc + 1) * S5_LANES)
        a_re = jnp.broadcast_to(are_ref[:, ls], (SUBLANE, S5_LANES))
        a_im = jnp.broadcast_to(aim_ref[:, ls], (SUBLANE, S5_LANES))
        for rt in range(bsz // SUBLANE):
            rs = slice(rt * SUBLANE, (rt + 1) * SUBLANE)

            def step(t, carry, ls=ls, rt=rt, a_re=a_re, a_im=a_im):
                h_re, h_im = carry
                rows = pl.ds(pl.multiple_of(t * bsz + rt * SUBLANE, SUBLANE), SUBLANE)
                n_re = a_re * h_re - a_im * h_im + sre_ref[rows, ls]
                n_im = a_re * h_im + a_im * h_re + sim_ref[rows, ls]
                sre_ref[rows, ls] = n_re
                sim_ref[rows, ls] = n_im
                return n_re, n_im

            h_re, h_im = lax.fori_loop(0, steps, step, (hre_ref[rs, ls], him_ref[rs, ls]),
                                       unroll=True)
            hre_ref[rs, ls] = h_re
            him_ref[rs, ls] = h_im

    for j in range(S5_NSUPER):
        st = slice(j * S5_SUPER_ST, (j + 1) * S5_SUPER_ST)
        ch = slice(j * S5_SUPER_CH, (j + 1) * S5_SUPER_CH)
        y = (_dot(sre_ref[:, st].astype(BF16), wc_ref[j, 0:S5_SUPER_ST, :])
             + _dot(sim_ref[:, st].astype(BF16), wc_ref[j, S5_SUPER_ST:, :]))
        z_ref[:, ch] = (y + d_ref[:, ch] * u_ref[:, ch]).astype(BF16)
    z = _dot(unperm_ref[...], z_ref[...]).astype(BF16)
    zz = _dot(z, wglu_ref[...]) + bglu_ref[...]
    out = zz[:, :D_MODEL] * _sigmoid(zz[:, D_MODEL:])
    o_ref[...] = (x + _rms(out, g_ref[1:2, :])).reshape(o_ref.shape)


def _s5(x, steps_per_block, g2, prep, d_skip, w_glu, b_glu, h0_re, h0_im):
    wb, wc, a_re, a_im = prep
    bsz, t, _ = x.shape
    rows = steps_per_block * bsz
    r = np.arange(rows)
    perm = np.zeros((rows, rows), np.float32)
    perm[(r % steps_per_block) * bsz + r // steps_per_block, r] = 1.0
    zero2 = lambda i: (0, 0)
    zero3 = lambda i: (0, 0, 0)
    resident = dict(pipeline_mode=pl.Buffered(1))
    state = pl.BlockSpec((bsz, S5_FLAT), zero2)
    if steps_per_block == t:
        x = x.reshape(1, rows, D_MODEL)
        xspec = pl.BlockSpec((1, rows, D_MODEL), lambda i: (0, 0, 0))
    else:
        assert steps_per_block % SUBLANE == 0
        xspec = pl.BlockSpec((bsz, steps_per_block, D_MODEL), lambda i: (0, i, 0))
    y, h_re, h_im = pl.pallas_call(
        functools.partial(_s5_kernel, bsz=bsz, steps=steps_per_block),
        grid=(t // steps_per_block,),
        in_specs=[
            xspec,
            pl.BlockSpec((2, D_MODEL), zero2),
            pl.BlockSpec(wb.shape, zero3, **resident),
            pl.BlockSpec(wc.shape, zero3, **resident),
            pl.BlockSpec((1, S5_FLAT), zero2),
            pl.BlockSpec((1, S5_FLAT), zero2),
            pl.BlockSpec((1, D_MODEL), zero2),
            pl.BlockSpec((D_MODEL, 2 * D_MODEL), zero2, **resident),
            pl.BlockSpec((1, 2 * D_MODEL), zero2),
            state, state,
            pl.BlockSpec((rows, rows), zero2),
            pl.BlockSpec((rows, rows), zero2),
        ],
        out_specs=[xspec, state, state],
        out_shape=[
            jax.ShapeDtypeStruct(x.shape, F32),
            jax.ShapeDtypeStruct((bsz, S5_FLAT), F32),
            jax.ShapeDtypeStruct((bsz, S5_FLAT), F32),
        ],
        scratch_shapes=[
            pltpu.VMEM((rows, D_MODEL), F32),
            pltpu.VMEM((rows, S5_FLAT), F32),
            pltpu.VMEM((rows, S5_FLAT), F32),
            pltpu.VMEM((rows, D_MODEL), BF16),
        ],
        compiler_params=pltpu.CompilerParams(
            dimension_semantics=("arbitrary",), vmem_limit_bytes=VMEM_LIMIT),
        name="s5",
    )(x, g2, wb, wc, a_re, a_im, d_skip.reshape(1, D_MODEL), w_glu.astype(BF16),
      b_glu.reshape(1, 2 * D_MODEL), h0_re, h0_im,
      jnp.asarray(perm, BF16), jnp.asarray(perm.T, BF16))
    return y.reshape(bsz, t, D_MODEL), h_re, h_im


def kernel(x_prompt, x_sample, state_gla, state_s5_re, state_s5_im, norm_g, w_ffn_gu, w_ffn_down,
           gla_w_in, gla_w_g2, gla_b_g, gla_g_onorm, gla_w_out,
           s5_lam_re, s5_lam_im, s5_log_dt, s5_b_re, s5_b_im, s5_c_re, s5_c_im, s5_d, s5_w_glu,
           s5_b_glu):
    pb, pt, _ = x_prompt.shape
    sb, st, _ = x_sample.shape

    def ffn(x, layer, which):
        g2 = norm_g[layer, 4 * which:4 * which + 2]
        return _ffn(x, g2, w_ffn_gu, w_ffn_down, layer, which)

    gla_w = _gla_weights(gla_w_in[0], gla_w_g2[0], gla_b_g[0], gla_g_onorm[0], gla_w_out[0])
    xp = ffn(x_prompt, 0, 0)
    xs = ffn(x_sample, 0, 0)
    xp, gla_p = _gla_prompt(xp, norm_g[0, 2:4], gla_w)
    xs, gla_s = _gla_sample(xs, state_gla.reshape(sb, GLA_HEADS, GLA_DK, GLA_DV), norm_g[0, 2:4], gla_w)
    xp = ffn(xp, 0, 1)
    xs = ffn(xs, 0, 1)

    xp = ffn(xp, 1, 0)
    xs = ffn(xs, 1, 0)
    prep = _s5_prep(s5_lam_re[0], s5_lam_im[0], s5_log_dt[0], s5_b_re[0], s5_b_im[0],
                    s5_c_re[0], s5_c_im[0])
    zeros = jnp.zeros((pb, S5_FLAT), F32)
    s5_args = (norm_g[1, 2:4], prep, s5_d[0], s5_w_glu[0], s5_b_glu[0])
    xp, hre_p, him_p = _s5(xp, S5_STEPS, *s5_args, zeros, zeros)
    xs, hre_s, him_s = _s5(xs, st, *s5_args,
                           state_s5_re.reshape(sb, S5_FLAT), state_s5_im.reshape(sb, S5_FLAT))
    y_prompt = ffn(xp, 1, 1)
    y_sample = ffn(xs, 1, 1)

    def s5_state(h, b):
        return h.reshape(1, b, S5_GROUPS, S5_STATE)

    return (y_prompt, y_sample, gla_p.reshape(1, pb, GLA_HEADS, GLA_DK, GLA_DV),
            s5_state(hre_p, pb), s5_state(him_p, pb),
            gla_s.reshape(1, sb, GLA_HEADS, GLA_DK, GLA_DV),
            s5_state(hre_s, sb), s5_state(him_s, sb))
```

```python
import functools
import math

import jax
import jax.numpy as jnp
import numpy as np
from jax import lax
from jax.experimental import pallas as pl
from jax.experimental.pallas import tpu as pltpu

F32 = jnp.float32
BF16 = jnp.bfloat16

D_MODEL = 1024
D_FF = 2816
GLA_HEADS = 4
GLA_DK = 128
GLA_DV = 256
GLA_KEY_DIM = GLA_HEADS * GLA_DK
GLA_VAL_DIM = GLA_HEADS * GLA_DV
GATE_RANK = 16
GATE_TAU = 16.0
S5_GROUP = 16
S5_GROUPS = 64
S5_STATE = 64
S5_FLAT = S5_GROUPS * S5_STATE
S5_SUPER = 8
S5_NSUPER = S5_GROUPS // S5_SUPER
S5_SUPER_CH = S5_SUPER * S5_GROUP
S5_SUPER_ST = S5_SUPER * S5_STATE
EPS = 1e-6
LANE = 128
SUBLANE = 8
VMEM_LIMIT = 56 * 1024 * 1024

FFN_ROWS = 1024
FFN_GROUP = 512
FFN_CHUNK = 256
FFN_SLOTS = 4
GLA_CHUNK = 64
GLA_TBLOCK = 256
GLA_PROJ_CHUNK = 256
LOG2E = math.log2(math.e)
GLA_SAMPLE_SEQS = 8
S5_LANES = 512
S5_STEPS = 32


def _dot(a, b):
    return jnp.dot(a, b, preferred_element_type=F32)


def _dot_nt(a, b):
    return lax.dot_general(a, b, (((1,), (1,)), ((), ())), preferred_element_type=F32)


def _rms(x, g):
    ms = jnp.mean(x * x, axis=-1, keepdims=True)
    return x * lax.rsqrt(ms + EPS) * g


def _sigmoid(x):
    return 1.0 / (1.0 + jnp.exp(-x))


def _split_bf16(x):
    hi = x.astype(BF16)
    lo = (x - hi.astype(F32)).astype(BF16)
    return hi, lo


def _ffn_row_groups(n_rows):
    return [slice(r0, min(r0 + FFN_GROUP, n_rows)) for r0 in range(0, n_rows, FFN_GROUP)]


def _ffn_hidden_chunk(xn_ref, wgu_ref, act_ref, rows, c):
    lo = c * FFN_CHUNK
    gate = _dot(xn_ref[rows, :], wgu_ref[:, lo:lo + FFN_CHUNK])
    up = _dot(xn_ref[rows, :], wgu_ref[:, D_FF + lo:D_FF + lo + FFN_CHUNK])
    act_ref[rows, lo:lo + FFN_CHUNK] = (gate * _sigmoid(gate) * up).astype(BF16)


def _ffn_finish(x_ref, g_ref, wd_ref, o_ref, act_ref, rows):
    y = _dot(act_ref[rows, :], wd_ref[...])
    o_ref[rows, :] = x_ref[rows, :] + 0.5 * _rms(y, g_ref[1:2, :])


def _ffn_stream_kernel(x_ref, g_ref, wgu_hbm, wd_hbm, o_ref, wgu_ref, wd_ref,
                       xn_ref, act_ref, gu_stage, d_stage, sem, *, layer, which):
    n_chunks = D_FF // FFN_CHUNK
    groups = _ffn_row_groups(x_ref.shape[0])
    xn_ref[...] = _rms(x_ref[...], g_ref[0:1, :]).astype(BF16)

    def chunk_copies(c):
        slot, lo = c % FFN_SLOTS, c * FFN_CHUNK
        return (
            pltpu.make_async_copy(wgu_hbm.at[layer, which, :, pl.ds(lo, FFN_CHUNK)],
                                  gu_stage.at[slot, 0], sem.at[slot, 0]),
            pltpu.make_async_copy(wgu_hbm.at[layer, which, :, pl.ds(D_FF + lo, FFN_CHUNK)],
                                  gu_stage.at[slot, 1], sem.at[slot, 1]),
            pltpu.make_async_copy(wd_hbm.at[layer, which, pl.ds(lo, FFN_CHUNK), :],
                                  d_stage.at[slot], sem.at[slot, 2]),
        )

    ahead = FFN_SLOTS - 1
    for c in range(min(ahead, n_chunks)):
        for cp in chunk_copies(c):
            cp.start()
    for c in range(n_chunks):
        slot, lo = c % FFN_SLOTS, c * FFN_CHUNK
        for cp in chunk_copies(c):
            cp.wait()
        if c + ahead < n_chunks:
            for cp in chunk_copies(c + ahead):
                cp.start()
        wgu_ref[:, lo:lo + FFN_CHUNK] = gu_stage[slot, 0].astype(BF16)
        wgu_ref[:, D_FF + lo:D_FF + lo + FFN_CHUNK] = gu_stage[slot, 1].astype(BF16)
        wd_ref[lo:lo + FFN_CHUNK, :] = d_stage[slot].astype(BF16)
        for rows in groups:
            _ffn_hidden_chunk(xn_ref, wgu_ref, act_ref, rows, c)
    for rows in groups:
        _ffn_finish(x_ref, g_ref, wd_ref, o_ref, act_ref, rows)


def _ffn_resident_kernel(x_ref, g_ref, wgu_ref, wd_ref, o_ref, xn_ref, act_ref):
    xn_ref[...] = _rms(x_ref[...], g_ref[0:1, :]).astype(BF16)
    for rows in _ffn_row_groups(x_ref.shape[0]):
        for c in range(D_FF // FFN_CHUNK):
            _ffn_hidden_chunk(xn_ref, wgu_ref, act_ref, rows, c)
        _ffn_finish(x_ref, g_ref, wd_ref, o_ref, act_ref, rows)


def _ffn_stream(x, g2, wgu, wd, layer, which):
    shape = x.shape
    x = x.reshape(-1, D_MODEL)
    n = x.shape[0]
    zero = lambda i: (0, 0)
    y, wgu_bf, wd_bf = pl.pallas_call(
        functools.partial(_ffn_stream_kernel, layer=layer, which=which),
        grid=(1,),
        in_specs=[
            pl.BlockSpec((n, D_MODEL), zero),
            pl.BlockSpec((2, D_MODEL), zero),
            pl.BlockSpec(memory_space=pl.ANY),
            pl.BlockSpec(memory_space=pl.ANY),
        ],
        out_specs=[
            pl.BlockSpec((n, D_MODEL), zero),
            pl.BlockSpec((D_MODEL, 2 * D_FF), zero, pipeline_mode=pl.Buffered(1)),
            pl.BlockSpec((D_FF, D_MODEL), zero, pipeline_mode=pl.Buffered(1)),
        ],
        out_shape=[
            jax.ShapeDtypeStruct((n, D_MODEL), F32),
            jax.ShapeDtypeStruct((D_MODEL, 2 * D_FF), BF16),
            jax.ShapeDtypeStruct((D_FF, D_MODEL), BF16),
        ],
        scratch_shapes=[
            pltpu.VMEM((n, D_MODEL), BF16),
            pltpu.VMEM((n, D_FF), BF16),
            pltpu.VMEM((FFN_SLOTS, 2, D_MODEL, FFN_CHUNK), F32),
            pltpu.VMEM((FFN_SLOTS, FFN_CHUNK, D_MODEL), F32),
            pltpu.SemaphoreType.DMA((FFN_SLOTS, 3)),
        ],
        compiler_params=pltpu.CompilerParams(
            dimension_semantics=("arbitrary",), vmem_limit_bytes=VMEM_LIMIT),
        name="ffn_stream",
    )(x, g2, wgu, wd)
    return y.reshape(shape), wgu_bf, wd_bf


def _ffn_resident(x, g2, wgu_bf, wd_bf):
    shape = x.shape
    x = x.reshape(-1, D_MODEL)
    n = x.shape[0]
    tm = min(FFN_ROWS, n)
    assert n % tm == 0
    zero = lambda i: (0, 0)
    resident = dict(pipeline_mode=pl.Buffered(1))
    return pl.pallas_call(
        _ffn_resident_kernel,
        grid=(n // tm,),
        in_specs=[
            pl.BlockSpec((tm, D_MODEL), lambda i: (i, 0)),
            pl.BlockSpec((2, D_MODEL), zero),
            pl.BlockSpec((D_MODEL, 2 * D_FF), zero, **resident),
            pl.BlockSpec((D_FF, D_MODEL), zero, **resident),
        ],
        out_specs=pl.BlockSpec((tm, D_MODEL), lambda i: (i, 0)),
        out_shape=jax.ShapeDtypeStruct((n, D_MODEL), F32),
        scratch_shapes=[pltpu.VMEM((tm, D_MODEL), BF16), pltpu.VMEM((tm, D_FF), BF16)],
        compiler_params=pltpu.CompilerParams(
            dimension_semantics=("arbitrary",), vmem_limit_bytes=VMEM_LIMIT),
        name="ffn",
    )(x, g2, wgu_bf, wd_bf).reshape(shape)


def _gla_tables(rows, seq):
    t = np.arange(rows)[:, None]
    r = np.arange(rows)[None, :]
    cum = ((t // seq) == (r // seq)) & (r <= t)
    masks = [t == r]
    m = seq // 2
    while m >= 1:
        masks.append(((t // (2 * m)) == (r // (2 * m))) & (t % (2 * m) >= m) & (r % (2 * m) < m))
        m //= 2
    return jnp.asarray(cum.astype(np.float32), BF16), jnp.asarray(np.stack(masks).astype(np.float32))


def _gla_project_cols(hn_ref, win_ref, proj_ref, c0):
    proj_ref[:, c0:c0 + GLA_PROJ_CHUNK] = _dot(hn_ref[...], win_ref[:, c0:c0 + GLA_PROJ_CHUNK])


def _gla_project(x, g_ref, win_ref, wglr_ref, wg2_ref, bg_ref, cum_ref, hn_ref, proj_ref, lf_ref, b_ref):
    hn = _rms(x, g_ref[0:1, :]).astype(BF16)
    hn_ref[...] = hn
    glr = _dot(hn, wglr_ref[...]).astype(BF16)
    gate = _dot(glr, wg2_ref[...]) + bg_ref[...]
    for c0 in range(0, 2 * GLA_KEY_DIM, GLA_PROJ_CHUNK):
        _gla_project_cols(hn_ref, win_ref, proj_ref, c0)
    lf = (jnp.minimum(gate, 0.0) - jnp.log1p(jnp.exp(-jnp.abs(gate)))) * (1.0 / GATE_TAU)
    lf_ref[...] = lf
    hi, lo = _split_bf16(lf)
    b_ref[...] = _dot(cum_ref[...], hi) + _dot(cum_ref[...], lo)


def _block_rows(ref, r0, col, rows, period, offset):
    def bc(row, n):
        tile = ref[pl.ds(r0 + row // SUBLANE * SUBLANE, SUBLANE), col]
        return jnp.broadcast_to(tile[row % SUBLANE:row % SUBLANE + 1, :], (n, GLA_DK))

    if period >= SUBLANE:
        parts = [bc(p0 + offset, period) for p0 in range(0, rows, period)]
    else:
        assert 2 * period == SUBLANE
        low = lax.broadcasted_iota(jnp.int32, (SUBLANE, 1), 0) < period
        parts = [jnp.where(low, bc(t0 + offset, SUBLANE), bc(t0 + period + offset, SUBLANE))
                 for t0 in range(0, rows, SUBLANE)]
    return parts[0] if len(parts) == 1 else jnp.concatenate(parts, axis=0)


def _gla_attention(qs, k, b, lf, b_ref, r0, col, rows, seq, mask_ref):
    qb, kb = qs.astype(BF16), k.astype(BF16)
    att = _dot_nt(qb, kb) * mask_ref[0]
    lvl, m = 1, seq // 2
    while m >= 1:
        if m > 1:
            d = b - _block_rows(b_ref, r0, col, rows, 2 * m, m - 1)
            w = jnp.exp2(jnp.abs(d) * (-LOG2E))
        else:
            odd = lax.broadcasted_iota(jnp.int32, (rows, 1), 0) % 2 == 1
            w = jnp.where(odd, jnp.exp(lf), 1.0)
        wb = w.astype(BF16)
        att = att + _dot_nt(qb * wb, kb * wb) * mask_ref[lvl]
        lvl, m = lvl + 1, m // 2
    return att


def _gla_finish_head(o, r, gon_ref):
    on = _rms(o, gon_ref[...])
    return (on * (r * _sigmoid(r))).astype(BF16)


def _gla_prompt_kernel(x_ref, g_ref, win_ref, wglr_ref, wg2_ref, bg_ref, gon_ref, wout_ref,
                       cum_ref, mask_ref, o_ref, snew_ref,
                       hn_ref, proj_ref, lf_ref, b_ref, og_ref, s_ref, att_ref, qb_ref, tt_ref):
    tb = pl.program_id(1)
    c = GLA_CHUNK

    @pl.when(tb == 0)
    def _():
        s_ref[...] = jnp.zeros_like(s_ref)

    x = x_ref[0]
    _gla_project(x, g_ref, win_ref, wglr_ref, wg2_ref, bg_ref, cum_ref, hn_ref, proj_ref, lf_ref, b_ref)

    late_cols = list(range(2 * GLA_KEY_DIM, proj_ref.shape[1], GLA_PROJ_CHUNK))
    n_heads_total = GLA_TBLOCK // c * GLA_HEADS
    for ci, r0 in enumerate(range(0, GLA_TBLOCK, c)):
        rows = slice(r0, r0 + c)
        for h in range(GLA_HEADS):
            done = ci * GLA_HEADS + h
            for c0 in late_cols[done * len(late_cols) // n_heads_total:
                                (done + 1) * len(late_cols) // n_heads_total]:
                _gla_project_cols(hn_ref, win_ref, proj_ref, c0)
            kcol = slice(h * GLA_DK, (h + 1) * GLA_DK)
            qs = proj_ref[rows, kcol] * (GLA_DK ** -0.5)
            k = proj_ref[rows, GLA_KEY_DIM + h * GLA_DK:GLA_KEY_DIM + (h + 1) * GLA_DK]
            b = b_ref[rows, kcol]
            att = _gla_attention(qs, k, b, lf_ref[rows, kcol], b_ref, r0, kcol, c, c, mask_ref)
            att_ref[ci * GLA_HEADS + h] = att.astype(BF16)
            qb_ref[rows, kcol] = (qs * jnp.exp(b)).astype(BF16)
            khat = k * jnp.exp(_block_rows(b_ref, r0, kcol, c, c, c - 1) - b)
            tile = jnp.concatenate(
                [khat, jnp.exp(b[c - SUBLANE:c, :]), jnp.zeros((LANE - c - SUBLANE, GLA_DK), F32)], axis=0)
            tt_ref[ci * GLA_HEADS + h] = jnp.transpose(tile)

    for ci, r0 in enumerate(range(0, GLA_TBLOCK, c)):
        rows = slice(r0, r0 + c)
        for h in range(GLA_HEADS):
            kcol = slice(h * GLA_DK, (h + 1) * GLA_DK)
            vcol = slice(h * GLA_DV, (h + 1) * GLA_DV)
            v = proj_ref[rows, 2 * GLA_KEY_DIM + h * GLA_DV:2 * GLA_KEY_DIM + (h + 1) * GLA_DV].astype(BF16)
            r = proj_ref[rows, 2 * GLA_KEY_DIM + GLA_VAL_DIM + h * GLA_DV:
                         2 * GLA_KEY_DIM + GLA_VAL_DIM + (h + 1) * GLA_DV]
            s = s_ref[h]
            tt = tt_ref[ci * GLA_HEADS + h]
            o = _dot(att_ref[ci * GLA_HEADS + h], v) + _dot(qb_ref[rows, kcol], s.astype(BF16))
            decay = tt[:, c + SUBLANE - 1:c + SUBLANE]
            s_ref[h] = s * decay + _dot(tt[:, 0:c].astype(BF16), v)
            og_ref[rows, vcol] = _gla_finish_head(o, r, gon_ref)
    out = _dot(og_ref[...], wout_ref[...])
    o_ref[0] = x + _rms(out, g_ref[1:2, :])

    @pl.when(tb == pl.num_programs(1) - 1)
    def _():
        snew_ref[0] = s_ref[...]


def _gla_sample_kernel(x_ref, g_ref, win_ref, wglr_ref, wg2_ref, bg_ref, gon_ref, wout_ref,
                       cum_ref, mask_ref, s0_ref, o_ref, snew_ref,
                       hn_ref, proj_ref, lf_ref, b_ref, og_ref, *, seq):
    rows = x_ref.shape[0]
    grp = 2 * SUBLANE
    x = x_ref[...]
    _gla_project(x, g_ref, win_ref, wglr_ref, wg2_ref, bg_ref, cum_ref, hn_ref, proj_ref, lf_ref, b_ref)
    for c0 in range(2 * GLA_KEY_DIM, proj_ref.shape[1], GLA_PROJ_CHUNK):
        _gla_project_cols(hn_ref, win_ref, proj_ref, c0)
    rid = lax.broadcasted_iota(jnp.int32, (grp, 1), 0)
    for h in range(GLA_HEADS):
        kcol = slice(h * GLA_DK, (h + 1) * GLA_DK)
        vcol = slice(h * GLA_DV, (h + 1) * GLA_DV)
        qs = proj_ref[:, kcol] * (GLA_DK ** -0.5)
        k = proj_ref[:, GLA_KEY_DIM + h * GLA_DK:GLA_KEY_DIM + (h + 1) * GLA_DK]
        v = proj_ref[:, 2 * GLA_KEY_DIM + h * GLA_DV:2 * GLA_KEY_DIM + (h + 1) * GLA_DV]
        r = proj_ref[:, 2 * GLA_KEY_DIM + GLA_VAL_DIM + h * GLA_DV:
                     2 * GLA_KEY_DIM + GLA_VAL_DIM + (h + 1) * GLA_DV]
        b = b_ref[:, kcol]
        att = _gla_attention(qs, k, b, lf_ref[:, kcol], b_ref, 0, kcol, rows, seq, mask_ref)
        eb = jnp.exp(b)
        qb = (qs * eb).astype(BF16)
        khat = k * jnp.exp(_block_rows(b_ref, 0, kcol, rows, seq, seq - 1) - b)
        o_intra = _dot(att.astype(BF16), v.astype(BF16))
        o_parts = []
        for gi in range(rows // grp):
            gr = slice(gi * grp, (gi + 1) * grp)
            tile = jnp.concatenate(
                [khat[gr], eb[gr], jnp.zeros((LANE - 2 * grp, GLA_DK), F32)], axis=0)
            tt = jnp.transpose(tile)
            kt = tt[:, 0:grp].astype(BF16)
            qg = qb[gr]
            vg = v[gr]
            o_g = jnp.zeros((grp, GLA_DV), F32)
            for si in range(grp // seq):
                sq = gi * (grp // seq) + si
                mine = (rid >= si * seq) & (rid < (si + 1) * seq)
                s = s0_ref[sq, h]
                o_g = o_g + jnp.where(mine, _dot(qg, s.astype(BF16)), 0.0)
                vm = jnp.where(mine, vg, 0.0).astype(BF16)
                last = grp + (si + 1) * seq - 1
                snew_ref[sq, h] = s * tt[:, last:last + 1] + _dot(kt, vm)
            o_parts.append(o_g)
        o = o_intra + jnp.concatenate(o_parts, axis=0)
        og_ref[:, vcol] = _gla_finish_head(o, r, gon_ref)
    out = _dot(og_ref[...], wout_ref[...])
    o_ref[...] = x + _rms(out, g_ref[1:2, :])


def _gla_weight_specs(nidx):
    zero = (lambda *_: (0, 0))
    resident = dict(pipeline_mode=pl.Buffered(1))
    del nidx
    return [
        pl.BlockSpec((2, D_MODEL), zero),
        pl.BlockSpec((D_MODEL, 2 * GLA_KEY_DIM + 2 * GLA_VAL_DIM), zero, **resident),
        pl.BlockSpec((D_MODEL, LANE), zero, **resident),
        pl.BlockSpec((LANE, GLA_KEY_DIM), zero, **resident),
        pl.BlockSpec((1, GLA_KEY_DIM), zero),
        pl.BlockSpec((1, GLA_DV), zero),
        pl.BlockSpec((GLA_VAL_DIM, D_MODEL), zero, **resident),
    ]


def _gla_weights(w_in, w_g2, b_g, g_onorm, w_out):
    n_main = 2 * GLA_KEY_DIM + 2 * GLA_VAL_DIM
    w_main = w_in[:, :n_main].astype(BF16)
    w_glr = jnp.pad(w_in[:, n_main:], ((0, 0), (0, LANE - GATE_RANK))).astype(BF16)
    w_g2p = jnp.pad(w_g2, ((0, LANE - GATE_RANK), (0, 0))).astype(BF16)
    return (w_main, w_glr, w_g2p, b_g.reshape(1, GLA_KEY_DIM), g_onorm.reshape(1, GLA_DV),
            w_out.astype(BF16))


def _gla_prompt(x, g2, weights):
    bsz, t, _ = x.shape
    cum, _ = _gla_tables(GLA_TBLOCK, GLA_CHUNK)
    _, masks = _gla_tables(GLA_CHUNK, GLA_CHUNK)
    n_main = 2 * GLA_KEY_DIM + 2 * GLA_VAL_DIM
    const2 = lambda b, i: (0, 0)
    return pl.pallas_call(
        _gla_prompt_kernel,
        grid=(bsz, t // GLA_TBLOCK),
        in_specs=[pl.BlockSpec((1, GLA_TBLOCK, D_MODEL), lambda b, i: (b, i, 0))]
        + _gla_weight_specs(2)
        + [pl.BlockSpec(cum.shape, const2),
           pl.BlockSpec(masks.shape, lambda b, i: (0, 0, 0))],
        out_specs=[
            pl.BlockSpec((1, GLA_TBLOCK, D_MODEL), lambda b, i: (b, i, 0)),
            pl.BlockSpec((1, GLA_HEADS, GLA_DK, GLA_DV), lambda b, i: (b, 0, 0, 0)),
        ],
        out_shape=[
            jax.ShapeDtypeStruct(x.shape, F32),
            jax.ShapeDtypeStruct((bsz, GLA_HEADS, GLA_DK, GLA_DV), F32),
        ],
        scratch_shapes=[
            pltpu.VMEM((GLA_TBLOCK, D_MODEL), BF16),
            pltpu.VMEM((GLA_TBLOCK, n_main), F32),
            pltpu.VMEM((GLA_TBLOCK, GLA_KEY_DIM), F32),
            pltpu.VMEM((GLA_TBLOCK, GLA_KEY_DIM), F32),
            pltpu.VMEM((GLA_TBLOCK, GLA_VAL_DIM), BF16),
            pltpu.VMEM((GLA_HEADS, GLA_DK, GLA_DV), F32),
            pltpu.VMEM((GLA_TBLOCK // GLA_CHUNK * GLA_HEADS, GLA_CHUNK, GLA_CHUNK), BF16),
            pltpu.VMEM((GLA_TBLOCK, GLA_KEY_DIM), BF16),
            pltpu.VMEM((GLA_TBLOCK // GLA_CHUNK * GLA_HEADS, LANE, GLA_DK), F32),
        ],
        compiler_params=pltpu.CompilerParams(
            dimension_semantics=("arbitrary", "arbitrary"), vmem_limit_bytes=VMEM_LIMIT),
        name="gla_prompt",
    )(x, g2, *weights, cum, masks)


def _gla_sample(x, s0, g2, weights):
    bsz, seq, _ = x.shape
    rows = GLA_SAMPLE_SEQS * seq
    cum, masks = _gla_tables(rows, seq)
    n_main = 2 * GLA_KEY_DIM + 2 * GLA_VAL_DIM
    state_spec = pl.BlockSpec((GLA_SAMPLE_SEQS, GLA_HEADS, GLA_DK, GLA_DV), lambda i: (i, 0, 0, 0))
    y, snew = pl.pallas_call(
        functools.partial(_gla_sample_kernel, seq=seq),
        grid=(bsz // GLA_SAMPLE_SEQS,),
        in_specs=[pl.BlockSpec((rows, D_MODEL), lambda i: (i, 0))]
        + _gla_weight_specs(1)
        + [pl.BlockSpec(cum.shape, lambda i: (0, 0)),
           pl.BlockSpec(masks.shape, lambda i: (0, 0, 0)),
           state_spec],
        out_specs=[pl.BlockSpec((rows, D_MODEL), lambda i: (i, 0)), state_spec],
        out_shape=[
            jax.ShapeDtypeStruct((bsz * seq, D_MODEL), F32),
            jax.ShapeDtypeStruct(s0.shape, F32),
        ],
        scratch_shapes=[
            pltpu.VMEM((rows, D_MODEL), BF16),
            pltpu.VMEM((rows, n_main), F32),
            pltpu.VMEM((rows, GLA_KEY_DIM), F32),
            pltpu.VMEM((rows, GLA_KEY_DIM), F32),
            pltpu.VMEM((rows, GLA_VAL_DIM), BF16),
        ],
        compiler_params=pltpu.CompilerParams(
            dimension_semantics=("arbitrary",), vmem_limit_bytes=VMEM_LIMIT),
        name="gla_sample",
    )(x.reshape(bsz * seq, D_MODEL), g2, *weights, cum, masks, s0)
    return y.reshape(x.shape), snew


def _s5_discretize(lam_re, lam_im, log_dt):
    dt = jnp.exp(log_dt)
    mag = jnp.exp(lam_re * dt)
    ang = lam_im * dt
    a_re, a_im = mag * jnp.cos(ang), mag * jnp.sin(ang)
    nr, ni = a_re - 1.0, a_im
    den = lam_re * lam_re + lam_im * lam_im
    f_re = (nr * lam_re + ni * lam_im) / den
    f_im = (ni * lam_re - nr * lam_im) / den
    return a_re, a_im, f_re, f_im


def _s5_prep_kernel(lre_ref, lim_ref, ldt_ref, bre_ref, bim_ref, cre_ref, cim_ref,
                    lref_ref, limf_ref, ldtf_ref, wb_ref, wc_ref, are_ref, aim_ref):
    _, _, f_re, f_im = _s5_discretize(lre_ref[...], lim_ref[...], ldt_ref[...])
    b_re, b_im = bre_ref[...], bim_ref[...]
    row = lax.broadcasted_iota(jnp.int32, (S5_SUPER_CH, S5_SUPER_ST), 0)
    col = lax.broadcasted_iota(jnp.int32, (S5_SUPER_CH, S5_SUPER_ST), 1)
    own = (row // S5_GROUP) == (col // S5_STATE)
    wb_ref[0, :, 0:S5_SUPER_ST] = jnp.where(own, f_re * b_re - f_im * b_im, 0.0).astype(BF16)
    wb_ref[0, :, S5_SUPER_ST:] = jnp.where(own, f_re * b_im + f_im * b_re, 0.0).astype(BF16)
    rowc = lax.broadcasted_iota(jnp.int32, (S5_SUPER_ST, S5_SUPER_CH), 0)
    colc = lax.broadcasted_iota(jnp.int32, (S5_SUPER_ST, S5_SUPER_CH), 1)
    ownc = (rowc // S5_STATE) == (colc // S5_GROUP)
    wc_ref[0, 0:S5_SUPER_ST, :] = jnp.where(ownc, cre_ref[...], 0.0).astype(BF16)
    wc_ref[0, S5_SUPER_ST:, :] = jnp.where(ownc, -cim_ref[...], 0.0).astype(BF16)
    a_re, a_im, _, _ = _s5_discretize(lref_ref[...], limf_ref[...], ldtf_ref[...])
    are_ref[...] = a_re
    aim_ref[...] = a_im


def _s5_prep(lam_re, lam_im, log_dt, b_re, b_im, c_re, c_im):
    g, p, ch, sg = S5_GROUPS, S5_STATE, S5_GROUP, S5_SUPER

    def rows_by_group(a_gp):
        return jnp.broadcast_to(a_gp[:, None, None, :], (g, ch, sg, p)).reshape(g * ch, sg * p)

    def b_layout(b):
        bt = jnp.transpose(b, (0, 2, 1))
        return jnp.broadcast_to(bt[:, :, None, :], (g, ch, sg, p)).reshape(g * ch, sg * p)

    def c_layout(cm):
        ct = jnp.transpose(cm.reshape(g // sg, sg, ch, p), (3, 0, 1, 2)).reshape(p, g * ch)
        return jnp.broadcast_to(ct[None], (sg, p, g * ch)).reshape(sg * p, g * ch)

    ldt_gp = jnp.broadcast_to(log_dt[:, None], (g, p))
    big = pl.BlockSpec((S5_SUPER_CH, S5_SUPER_ST), lambda j: (j, 0))
    cspec = pl.BlockSpec((S5_SUPER_ST, S5_SUPER_CH), lambda j: (0, j))
    flat = pl.BlockSpec((1, S5_SUPER_ST), lambda j: (0, j))
    return pl.pallas_call(
        _s5_prep_kernel,
        grid=(S5_NSUPER,),
        in_specs=[big] * 5 + [cspec] * 2 + [flat] * 3,
        out_specs=[
            pl.BlockSpec((1, S5_SUPER_CH, 2 * S5_SUPER_ST), lambda j: (j, 0, 0)),
            pl.BlockSpec((1, 2 * S5_SUPER_ST, S5_SUPER_CH), lambda j: (j, 0, 0)),
            flat, flat,
        ],
        out_shape=[
            jax.ShapeDtypeStruct((S5_NSUPER, S5_SUPER_CH, 2 * S5_SUPER_ST), BF16),
            jax.ShapeDtypeStruct((S5_NSUPER, 2 * S5_SUPER_ST, S5_SUPER_CH), BF16),
            jax.ShapeDtypeStruct((1, S5_FLAT), F32),
            jax.ShapeDtypeStruct((1, S5_FLAT), F32),
        ],
        compiler_params=pltpu.CompilerParams(dimension_semantics=("arbitrary",)),
        name="s5_prep",
    )(rows_by_group(lam_re), rows_by_group(lam_im), rows_by_group(ldt_gp),
      b_layout(b_re), b_layout(b_im), c_layout(c_re), c_layout(c_im),
      lam_re.reshape(1, S5_FLAT), lam_im.reshape(1, S5_FLAT), ldt_gp.reshape(1, S5_FLAT))


def _s5_kernel(x_ref, g_ref, wb_ref, wc_ref, are_ref, aim_ref, d_ref, wglu_ref, bglu_ref,
               h0re_ref, h0im_ref, perm_ref, unperm_ref, o_ref, hre_ref, him_ref,
               u_ref, sre_ref, sim_ref, z_ref, *, bsz, steps):
    i = pl.program_id(0)
    rows = bsz * steps

    @pl.when(i == 0)
    def _():
        hre_ref[...] = h0re_ref[...]
        him_ref[...] = h0im_ref[...]

    x = x_ref[...].reshape(rows, D_MODEL)
    u = _dot(perm_ref[...], _rms(x, g_ref[0:1, :]).astype(BF16))
    u_ref[...] = u
    ub = u.astype(BF16)
    for j in range(S5_NSUPER):
        bu = _dot(ub[:, j * S5_SUPER_CH:(j + 1) * S5_SUPER_CH], wb_ref[j])
        sre_ref[:, j * S5_SUPER_ST:(j + 1) * S5_SUPER_ST] = bu[:, :S5_SUPER_ST]
        sim_ref[:, j * S5_SUPER_ST:(j + 1) * S5_SUPER_ST] = bu[:, S5_SUPER_ST:]

    for lc in range(S5_FLAT // S5_LANES):
        ls = slice(lc * S5_LANES, (lc + 1) * S5_LANES)
        a_re = jnp.broadcast_to(are_ref[:, ls], (SUBLANE, S5_LANES))
        a_im = jnp.broadcast_to(aim_ref[:, ls], (SUBLANE, S5_LANES))
        for rt in range(bsz // SUBLANE):
            rs = slice(rt * SUBLANE, (rt + 1) * SUBLANE)

            def step(t, carry, ls=ls, rt=rt, a_re=a_re, a_im=a_im):
                h_re, h_im = carry
                rows = pl.ds(pl.multiple_of(t * bsz + rt * SUBLANE, SUBLANE), SUBLANE)
                n_re = a_re * h_re - a_im * h_im + sre_ref[rows, ls]
                n_im = a_re * h_im + a_im * h_re + sim_ref[rows, ls]
                sre_ref[rows, ls] = n_re
                sim_ref[rows, ls] = n_im
                return n_re, n_im

            h_re, h_im = lax.fori_loop(0, steps, step, (hre_ref[rs, ls], him_ref[rs, ls]),
                                       unroll=True)
            hre_ref[rs, ls] = h_re
            him_ref[rs, ls] = h_im

    for j in range(S5_NSUPER):
        st = slice(j * S5_SUPER_ST, (j + 1) * S5_SUPER_ST)
        ch = slice(j * S5_SUPER_CH, (j + 1) * S5_SUPER_CH)
        y = (_dot(sre_ref[:, st].astype(BF16), wc_ref[j, 0:S5_SUPER_ST, :])
             + _dot(sim_ref[:, st].astype(BF16), wc_ref[j, S5_SUPER_ST:, :]))
        z_ref[:, ch] = (y + d_ref[:, ch] * u_ref[:, ch]).astype(BF16)
    z = _dot(unperm_ref[...], z_ref[...]).astype(BF16)
    zz = _dot(z, wglu_ref[...]) + bglu_ref[...]
    out = zz[:, :D_MODEL] * _sigmoid(zz[:, D_MODEL:])
    o_ref[...] = (x + _rms(out, g_ref[1:2, :])).reshape(o_ref.shape)


def _s5(x, steps_per_block, g2, prep, d_skip, w_glu, b_glu, h0_re, h0_im):
    wb, wc, a_re, a_im = prep
    bsz, t, _ = x.shape
    rows = steps_per_block * bsz
    r = np.arange(rows)
    perm = np.zeros((rows, rows), np.float32)
    perm[(r % steps_per_block) * bsz + r // steps_per_block, r] = 1.0
    zero2 = lambda i: (0, 0)
    zero3 = lambda i: (0, 0, 0)
    resident = dict(pipeline_mode=pl.Buffered(1))
    state = pl.BlockSpec((bsz, S5_FLAT), zero2)
    if steps_per_block == t:
        x = x.reshape(1, rows, D_MODEL)
        xspec = pl.BlockSpec((1, rows, D_MODEL), lambda i: (0, 0, 0))
    else:
        assert steps_per_block % SUBLANE == 0
        xspec = pl.BlockSpec((bsz, steps_per_block, D_MODEL), lambda i: (0, i, 0))
    y, h_re, h_im = pl.pallas_call(
        functools.partial(_s5_kernel, bsz=bsz, steps=steps_per_block),
        grid=(t // steps_per_block,),
        in_specs=[
            xspec,
            pl.BlockSpec((2, D_MODEL), zero2),
            pl.BlockSpec(wb.shape, zero3, **resident),
            pl.BlockSpec(wc.shape, zero3, **resident),
            pl.BlockSpec((1, S5_FLAT), zero2),
            pl.BlockSpec((1, S5_FLAT), zero2),
            pl.BlockSpec((1, D_MODEL), zero2),
            pl.BlockSpec((D_MODEL, 2 * D_MODEL), zero2, **resident),
            pl.BlockSpec((1, 2 * D_MODEL), zero2),
            state, state,
            pl.BlockSpec((rows, rows), zero2),
            pl.BlockSpec((rows, rows), zero2),
        ],
        out_specs=[xspec, state, state],
        out_shape=[
            jax.ShapeDtypeStruct(x.shape, F32),
            jax.ShapeDtypeStruct((bsz, S5_FLAT), F32),
            jax.ShapeDtypeStruct((bsz, S5_FLAT), F32),
        ],
        scratch_shapes=[
            pltpu.VMEM((rows, D_MODEL), F32),
            pltpu.VMEM((rows, S5_FLAT), F32),
            pltpu.VMEM((rows, S5_FLAT), F32),
            pltpu.VMEM((rows, D_MODEL), BF16),
        ],
        compiler_params=pltpu.CompilerParams(
            dimension_semantics=("arbitrary",), vmem_limit_bytes=VMEM_LIMIT),
        name="s5",
    )(x, g2, wb, wc, a_re, a_im, d_skip.reshape(1, D_MODEL), w_glu.astype(BF16),
      b_glu.reshape(1, 2 * D_MODEL), h0_re, h0_im,
      jnp.asarray(perm, BF16), jnp.asarray(perm.T, BF16))
    return y.reshape(bsz, t, D_MODEL), h_re, h_im


def kernel(x_prompt, x_sample, state_gla, state_s5_re, state_s5_im, norm_g, w_ffn_gu, w_ffn_down,
           gla_w_in, gla_w_g2, gla_b_g, gla_g_onorm, gla_w_out,
           s5_lam_re, s5_lam_im, s5_log_dt, s5_b_re, s5_b_im, s5_c_re, s5_c_im, s5_d, s5_w_glu,
           s5_b_glu):
    pb, pt, _ = x_prompt.shape
    sb, st, _ = x_sample.shape

    def ffn(xp, xs, layer, which):
        g2 = norm_g[layer, 4 * which:4 * which + 2]
        xs, wgu_bf, wd_bf = _ffn_stream(xs, g2, w_ffn_gu, w_ffn_down, layer, which)
        return _ffn_resident(xp, g2, wgu_bf, wd_bf), xs

    gla_w = _gla_weights(gla_w_in[0], gla_w_g2[0], gla_b_g[0], gla_g_onorm[0], gla_w_out[0])
    xp, xs = ffn(x_prompt, x_sample, 0, 0)
    xp, gla_p = _gla_prompt(xp, norm_g[0, 2:4], gla_w)
    xs, gla_s = _gla_sample(xs, state_gla.reshape(sb, GLA_HEADS, GLA_DK, GLA_DV), norm_g[0, 2:4], gla_w)
    xp, xs = ffn(xp, xs, 0, 1)

    xp, xs = ffn(xp, xs, 1, 0)
    prep = _s5_prep(s5_lam_re[0], s5_lam_im[0], s5_log_dt[0], s5_b_re[0], s5_b_im[0],
                    s5_c_re[0], s5_c_im[0])
    zeros = jnp.zeros((pb, S5_FLAT), F32)
    s5_args = (norm_g[1, 2:4], prep, s5_d[0], s5_w_glu[0], s5_b_glu[0])
    xp, hre_p, him_p = _s5(xp, S5_STEPS, *s5_args, zeros, zeros)
    xs, hre_s, him_s = _s5(xs, st, *s5_args,
                           state_s5_re.reshape(sb, S5_FLAT), state_s5_im.reshape(sb, S5_FLAT))
    y_prompt, y_sample = ffn(xp, xs, 1, 1)

    def s5_state(h, b):
        return h.reshape(1, b, S5_GROUPS, S5_STATE)

    return (y_prompt, y_sample, gla_p.reshape(1, pb, GLA_HEADS, GLA_DK, GLA_DV),
            s5_state(hre_p, pb), s5_state(him_p, pb),
            gla_s.reshape(1, sb, GLA_HEADS, GLA_DK, GLA_DV),
            s5_state(hre_s, sb), s5_state(him_s, sb))
```

```python
import functools
import math

import jax
import jax.numpy as jnp
import numpy as np
from jax import lax
from jax.experimental import pallas as pl
from jax.experimental.pallas import tpu as pltpu

F32 = jnp.float32
BF16 = jnp.bfloat16

D_MODEL = 1024
D_FF = 2816
GLA_HEADS = 4
GLA_DK = 128
GLA_DV = 256
GLA_KEY_DIM = GLA_HEADS * GLA_DK
GLA_VAL_DIM = GLA_HEADS * GLA_DV
GATE_RANK = 16
GATE_TAU = 16.0
S5_GROUP = 16
S5_GROUPS = 64
S5_STATE = 64
S5_FLAT = S5_GROUPS * S5_STATE
S5_SUPER = 8
S5_NSUPER = S5_GROUPS // S5_SUPER
S5_SUPER_CH = S5_SUPER * S5_GROUP
S5_SUPER_ST = S5_SUPER * S5_STATE
EPS = 1e-6
LANE = 128
SUBLANE = 8
VMEM_LIMIT = 56 * 1024 * 1024

FFN_ROWS = 1024
FFN_GROUP = 512
FFN_CHUNK = 256
FFN_SLOTS = 4
GLA_CHUNK = 64
GLA_TBLOCK = 512
GLA_PROJ_CHUNK = 256
GLA_CUM_ROWS = 256
LOG2E = math.log2(math.e)
GLA_SAMPLE_SEQS = 16
S5_LANES = 512
S5_STEPS = 32


def _dot(a, b):
    return jnp.dot(a, b, preferred_element_type=F32)


def _dot_nt(a, b):
    return lax.dot_general(a, b, (((1,), (1,)), ((), ())), preferred_element_type=F32)


def _rms(x, g):
    ms = jnp.mean(x * x, axis=-1, keepdims=True)
    return x * lax.rsqrt(ms + EPS) * g


def _sigmoid(x):
    return 1.0 / (1.0 + jnp.exp(-x))


def _split_bf16(x):
    hi = x.astype(BF16)
    lo = (x - hi.astype(F32)).astype(BF16)
    return hi, lo


def _ffn_row_groups(n_rows):
    return [slice(r0, min(r0 + FFN_GROUP, n_rows)) for r0 in range(0, n_rows, FFN_GROUP)]


def _ffn_hidden_chunk(xn_ref, wgu_ref, act_ref, rows, c):
    lo = c * FFN_CHUNK
    gate = _dot(xn_ref[rows, :], wgu_ref[:, lo:lo + FFN_CHUNK])
    up = _dot(xn_ref[rows, :], wgu_ref[:, D_FF + lo:D_FF + lo + FFN_CHUNK])
    act_ref[rows, lo:lo + FFN_CHUNK] = (gate * _sigmoid(gate) * up).astype(BF16)


def _ffn_finish(x_ref, g_ref, wd_ref, o_ref, act_ref, rows):
    y = _dot(act_ref[rows, :], wd_ref[...])
    o_ref[rows, :] = x_ref[rows, :] + 0.5 * _rms(y, g_ref[1:2, :])


def _ffn_stream_kernel(x_ref, g_ref, wgu_hbm, wd_hbm, o_ref, wgu_out, wd_out,
                       xn_ref, act_ref, wgu_ref, wd_ref, gu_stage, d_stage, sem, out_sem,
                       *, layer, which):
    n_chunks = D_FF // FFN_CHUNK
    groups = _ffn_row_groups(x_ref.shape[0])
    xn_ref[...] = _rms(x_ref[...], g_ref[0:1, :]).astype(BF16)

    def chunk_slices(c):
        lo = c * FFN_CHUNK
        return ((slice(None), pl.ds(lo, FFN_CHUNK)), (slice(None), pl.ds(D_FF + lo, FFN_CHUNK)),
                (pl.ds(lo, FFN_CHUNK), slice(None)))

    def fetches(c):
        slot = c % FFN_SLOTS
        gate, up, down = chunk_slices(c)
        return (
            pltpu.make_async_copy(wgu_hbm.at[(layer, which) + gate], gu_stage.at[slot, 0], sem.at[slot, 0]),
            pltpu.make_async_copy(wgu_hbm.at[(layer, which) + up], gu_stage.at[slot, 1], sem.at[slot, 1]),
            pltpu.make_async_copy(wd_hbm.at[(layer, which) + down], d_stage.at[slot], sem.at[slot, 2]),
        )

    def writebacks(c):
        gate, up, down = chunk_slices(c)
        return (
            pltpu.make_async_copy(wgu_ref.at[gate], wgu_out.at[gate], out_sem.at[c, 0]),
            pltpu.make_async_copy(wgu_ref.at[up], wgu_out.at[up], out_sem.at[c, 1]),
            pltpu.make_async_copy(wd_ref.at[down], wd_out.at[down], out_sem.at[c, 2]),
        )

    ahead = FFN_SLOTS - 1
    for c in range(min(ahead, n_chunks)):
        for cp in fetches(c):
            cp.start()
    for c in range(n_chunks):
        slot, lo = c % FFN_SLOTS, c * FFN_CHUNK
        for cp in fetches(c):
            cp.wait()
        if c + ahead < n_chunks:
            for cp in fetches(c + ahead):
                cp.start()
        wgu_ref[:, lo:lo + FFN_CHUNK] = gu_stage[slot, 0].astype(BF16)
        wgu_ref[:, D_FF + lo:D_FF + lo + FFN_CHUNK] = gu_stage[slot, 1].astype(BF16)
        wd_ref[lo:lo + FFN_CHUNK, :] = d_stage[slot].astype(BF16)
        for cp in writebacks(c):
            cp.start()
        for rows in groups:
            _ffn_hidden_chunk(xn_ref, wgu_ref, act_ref, rows, c)
    for rows in groups:
        _ffn_finish(x_ref, g_ref, wd_ref, o_ref, act_ref, rows)
    for c in range(n_chunks):
        for cp in writebacks(c):
            cp.wait()


def _ffn_resident_kernel(x_ref, g_ref, wgu_ref, wd_ref, o_ref, xn_ref, act_ref):
    xn_ref[...] = _rms(x_ref[...], g_ref[0:1, :]).astype(BF16)
    for rows in _ffn_row_groups(x_ref.shape[0]):
        for c in range(D_FF // FFN_CHUNK):
            _ffn_hidden_chunk(xn_ref, wgu_ref, act_ref, rows, c)
        _ffn_finish(x_ref, g_ref, wd_ref, o_ref, act_ref, rows)


def _ffn_stream(x, g2, wgu, wd, layer, which):
    shape = x.shape
    x = x.reshape(-1, D_MODEL)
    n = x.shape[0]
    zero = lambda i: (0, 0)
    y, wgu_bf, wd_bf = pl.pallas_call(
        functools.partial(_ffn_stream_kernel, layer=layer, which=which),
        grid=(1,),
        in_specs=[
            pl.BlockSpec((n, D_MODEL), zero),
            pl.BlockSpec((2, D_MODEL), zero),
            pl.BlockSpec(memory_space=pl.ANY),
            pl.BlockSpec(memory_space=pl.ANY),
        ],
        out_specs=[
            pl.BlockSpec((n, D_MODEL), zero),
            pl.BlockSpec(memory_space=pl.ANY),
            pl.BlockSpec(memory_space=pl.ANY),
        ],
        out_shape=[
            jax.ShapeDtypeStruct((n, D_MODEL), F32),
            jax.ShapeDtypeStruct((D_MODEL, 2 * D_FF), BF16),
            jax.ShapeDtypeStruct((D_FF, D_MODEL), BF16),
        ],
        scratch_shapes=[
            pltpu.VMEM((n, D_MODEL), BF16),
            pltpu.VMEM((n, D_FF), BF16),
            pltpu.VMEM((D_MODEL, 2 * D_FF), BF16),
            pltpu.VMEM((D_FF, D_MODEL), BF16),
            pltpu.VMEM((FFN_SLOTS, 2, D_MODEL, FFN_CHUNK), F32),
            pltpu.VMEM((FFN_SLOTS, FFN_CHUNK, D_MODEL), F32),
            pltpu.SemaphoreType.DMA((FFN_SLOTS, 3)),
            pltpu.SemaphoreType.DMA((D_FF // FFN_CHUNK, 3)),
        ],
        compiler_params=pltpu.CompilerParams(
            dimension_semantics=("arbitrary",), vmem_limit_bytes=VMEM_LIMIT),
        name="ffn_stream",
    )(x, g2, wgu, wd)
    return y.reshape(shape), wgu_bf, wd_bf


def _ffn_resident(x, g2, wgu_bf, wd_bf):
    shape = x.shape
    x = x.reshape(-1, D_MODEL)
    n = x.shape[0]
    tm = min(FFN_ROWS, n)
    assert n % tm == 0
    zero = lambda i: (0, 0)
    resident = dict(pipeline_mode=pl.Buffered(1))
    return pl.pallas_call(
        _ffn_resident_kernel,
        grid=(n // tm,),
        in_specs=[
            pl.BlockSpec((tm, D_MODEL), lambda i: (i, 0)),
            pl.BlockSpec((2, D_MODEL), zero),
            pl.BlockSpec((D_MODEL, 2 * D_FF), zero, **resident),
            pl.BlockSpec((D_FF, D_MODEL), zero, **resident),
        ],
        out_specs=pl.BlockSpec((tm, D_MODEL), lambda i: (i, 0)),
        out_shape=jax.ShapeDtypeStruct((n, D_MODEL), F32),
        scratch_shapes=[pltpu.VMEM((tm, D_MODEL), BF16), pltpu.VMEM((tm, D_FF), BF16)],
        compiler_params=pltpu.CompilerParams(
            dimension_semantics=("arbitrary",), vmem_limit_bytes=VMEM_LIMIT),
        name="ffn",
    )(x, g2, wgu_bf, wd_bf).reshape(shape)


def _gla_tables(rows, seq):
    t = np.arange(rows)[:, None]
    r = np.arange(rows)[None, :]
    cum = ((t // seq) == (r // seq)) & (r <= t)
    masks = [t == r]
    m = seq // 2
    while m >= 1:
        masks.append(((t // (2 * m)) == (r // (2 * m))) & (t % (2 * m) >= m) & (r % (2 * m) < m))
        m //= 2
    return jnp.asarray(cum.astype(np.float32), BF16), jnp.asarray(np.stack(masks).astype(np.float32))


def _gla_project_cols(hn_ref, win_ref, proj_ref, c0):
    proj_ref[:, c0:c0 + GLA_PROJ_CHUNK] = _dot(hn_ref[...], win_ref[:, c0:c0 + GLA_PROJ_CHUNK])


def _gla_project(x, g_ref, win_ref, wglr_ref, wg2_ref, bg_ref, cum_ref, hn_ref, proj_ref, lf_ref, b_ref):
    hn = _rms(x, g_ref[0:1, :]).astype(BF16)
    hn_ref[...] = hn
    glr = _dot(hn, wglr_ref[...]).astype(BF16)
    gate = _dot(glr, wg2_ref[...]) + bg_ref[...]
    for c0 in range(0, 2 * GLA_KEY_DIM, GLA_PROJ_CHUNK):
        _gla_project_cols(hn_ref, win_ref, proj_ref, c0)
    lf = (jnp.minimum(gate, 0.0) - jnp.log1p(jnp.exp(-jnp.abs(gate)))) * (1.0 / GATE_TAU)
    lf_ref[...] = lf
    hi, lo = _split_bf16(lf)
    span = cum_ref.shape[0]
    for r0 in range(0, lf.shape[0], span):
        b_ref[r0:r0 + span, :] = (_dot(cum_ref[...], hi[r0:r0 + span]) + _dot(cum_ref[...], lo[r0:r0 + span]))


def _block_rows(ref, r0, col, rows, period, offset):
    def bc(row, n):
        tile = ref[pl.ds(r0 + row // SUBLANE * SUBLANE, SUBLANE), col]
        return jnp.broadcast_to(tile[row % SUBLANE:row % SUBLANE + 1, :], (n, GLA_DK))

    if period >= SUBLANE:
        parts = [bc(p0 + offset, period) for p0 in range(0, rows, period)]
    else:
        assert 2 * period == SUBLANE
        low = lax.broadcasted_iota(jnp.int32, (SUBLANE, 1), 0) < period
        parts = [jnp.where(low, bc(t0 + offset, SUBLANE), bc(t0 + period + offset, SUBLANE))
                 for t0 in range(0, rows, SUBLANE)]
    return parts[0] if len(parts) == 1 else jnp.concatenate(parts, axis=0)


def _gla_attention(qs, k, b, lf, b_ref, r0, col, rows, seq, mask_ref):
    qb, kb = qs.astype(BF16), k.astype(BF16)
    att = _dot_nt(qb, kb) * mask_ref[0]
    lvl, m = 1, seq // 2
    while m >= 1:
        if m > 1:
            d = b - _block_rows(b_ref, r0, col, rows, 2 * m, m - 1)
            w = jnp.exp2(jnp.abs(d) * (-LOG2E))
        else:
            odd = lax.broadcasted_iota(jnp.int32, (rows, 1), 0) % 2 == 1
            w = jnp.where(odd, jnp.exp(lf), 1.0)
        wb = w.astype(BF16)
        att = att + _dot_nt(qb * wb, kb * wb) * mask_ref[lvl]
        lvl, m = lvl + 1, m // 2
    return att


def _gla_finish_head(o, r, gon_ref):
    on = _rms(o, gon_ref[...])
    return (on * (r * _sigmoid(r))).astype(BF16)


def _gla_prompt_kernel(x_ref, g_ref, win_ref, wglr_ref, wg2_ref, bg_ref, gon_ref, wout_ref,
                       cum_ref, mask_ref, o_ref, snew_ref,
                       hn_ref, proj_ref, lf_ref, b_ref, og_ref, s_ref, att_ref, qb_ref, tt_ref):
    tb = pl.program_id(1)
    c = GLA_CHUNK

    @pl.when(tb == 0)
    def _():
        s_ref[...] = jnp.zeros_like(s_ref)

    x = x_ref[0]
    _gla_project(x, g_ref, win_ref, wglr_ref, wg2_ref, bg_ref, cum_ref, hn_ref, proj_ref, lf_ref, b_ref)

    late_cols = list(range(2 * GLA_KEY_DIM, proj_ref.shape[1], GLA_PROJ_CHUNK))
    n_heads_total = GLA_TBLOCK // c * GLA_HEADS
    for ci, r0 in enumerate(range(0, GLA_TBLOCK, c)):
        rows = slice(r0, r0 + c)
        for h in range(GLA_HEADS):
            done = ci * GLA_HEADS + h
            for c0 in late_cols[done * len(late_cols) // n_heads_total:
                                (done + 1) * len(late_cols) // n_heads_total]:
                _gla_project_cols(hn_ref, win_ref, proj_ref, c0)
            kcol = slice(h * GLA_DK, (h + 1) * GLA_DK)
            qs = proj_ref[rows, kcol] * (GLA_DK ** -0.5)
            k = proj_ref[rows, GLA_KEY_DIM + h * GLA_DK:GLA_KEY_DIM + (h + 1) * GLA_DK]
            b = b_ref[rows, kcol]
            att = _gla_attention(qs, k, b, lf_ref[rows, kcol], b_ref, r0, kcol, c, c, mask_ref)
            att_ref[ci * GLA_HEADS + h] = att.astype(BF16)
            qb_ref[rows, kcol] = (qs * jnp.exp(b)).astype(BF16)
            khat = k * jnp.exp(_block_rows(b_ref, r0, kcol, c, c, c - 1) - b)
            tile = jnp.concatenate(
                [khat, jnp.exp(b[c - SUBLANE:c, :]), jnp.zeros((LANE - c - SUBLANE, GLA_DK), F32)], axis=0)
            tt_ref[ci * GLA_HEADS + h] = jnp.transpose(tile)

    for ci, r0 in enumerate(range(0, GLA_TBLOCK, c)):
        rows = slice(r0, r0 + c)
        for h in range(GLA_HEADS):
            kcol = slice(h * GLA_DK, (h + 1) * GLA_DK)
            vcol = slice(h * GLA_DV, (h + 1) * GLA_DV)
            v = proj_ref[rows, 2 * GLA_KEY_DIM + h * GLA_DV:2 * GLA_KEY_DIM + (h + 1) * GLA_DV].astype(BF16)
            r = proj_ref[rows, 2 * GLA_KEY_DIM + GLA_VAL_DIM + h * GLA_DV:
                         2 * GLA_KEY_DIM + GLA_VAL_DIM + (h + 1) * GLA_DV]
            s = s_ref[h]
            tt = tt_ref[ci * GLA_HEADS + h]
            o = _dot(att_ref[ci * GLA_HEADS + h], v) + _dot(qb_ref[rows, kcol], s.astype(BF16))
            decay = tt[:, c + SUBLANE - 1:c + SUBLANE]
            s_ref[h] = s * decay + _dot(tt[:, 0:c].astype(BF16), v)
            og_ref[rows, vcol] = _gla_finish_head(o, r, gon_ref)
    out = _dot(og_ref[...], wout_ref[...])
    o_ref[0] = x + _rms(out, g_ref[1:2, :])

    @pl.when(tb == pl.num_programs(1) - 1)
    def _():
        snew_ref[0] = s_ref[...]


def _gla_sample_kernel(x_ref, g_ref, win_ref, wglr_ref, wg2_ref, bg_ref, gon_ref, wout_ref,
                       cum_ref, mask_ref, s0_ref, o_ref, snew_ref,
                       hn_ref, proj_ref, lf_ref, b_ref, og_ref, *, seq):
    rows = x_ref.shape[0]
    grp = 2 * SUBLANE
    x = x_ref[...]
    _gla_project(x, g_ref, win_ref, wglr_ref, wg2_ref, bg_ref, cum_ref, hn_ref, proj_ref, lf_ref, b_ref)
    for c0 in range(2 * GLA_KEY_DIM, proj_ref.shape[1], GLA_PROJ_CHUNK):
        _gla_project_cols(hn_ref, win_ref, proj_ref, c0)
    rid = lax.broadcasted_iota(jnp.int32, (grp, 1), 0)
    for h in range(GLA_HEADS):
        kcol = slice(h * GLA_DK, (h + 1) * GLA_DK)
        vcol = slice(h * GLA_DV, (h + 1) * GLA_DV)
        qs = proj_ref[:, kcol] * (GLA_DK ** -0.5)
        k = proj_ref[:, GLA_KEY_DIM + h * GLA_DK:GLA_KEY_DIM + (h + 1) * GLA_DK]
        v = proj_ref[:, 2 * GLA_KEY_DIM + h * GLA_DV:2 * GLA_KEY_DIM + (h + 1) * GLA_DV]
        r = proj_ref[:, 2 * GLA_KEY_DIM + GLA_VAL_DIM + h * GLA_DV:
                     2 * GLA_KEY_DIM + GLA_VAL_DIM + (h + 1) * GLA_DV]
        b = b_ref[:, kcol]
        att = _gla_attention(qs, k, b, lf_ref[:, kcol], b_ref, 0, kcol, rows, seq, mask_ref)
        eb = jnp.exp(b)
        qb = (qs * eb).astype(BF16)
        khat = k * jnp.exp(_block_rows(b_ref, 0, kcol, rows, seq, seq - 1) - b)
        o_intra = _dot(att.astype(BF16), v.astype(BF16))
        o_parts = []
        for gi in range(rows // grp):
            gr = slice(gi * grp, (gi + 1) * grp)
            tile = jnp.concatenate(
                [khat[gr], eb[gr], jnp.zeros((LANE - 2 * grp, GLA_DK), F32)], axis=0)
            tt = jnp.transpose(tile)
            kt = tt[:, 0:grp].astype(BF16)
            qg = qb[gr]
            vg = v[gr]
            o_g = jnp.zeros((grp, GLA_DV), F32)
            for si in range(grp // seq):
                sq = gi * (grp // seq) + si
                mine = (rid >= si * seq) & (rid < (si + 1) * seq)
                s = s0_ref[sq, h]
                o_g = o_g + jnp.where(mine, _dot(qg, s.astype(BF16)), 0.0)
                vm = jnp.where(mine, vg, 0.0).astype(BF16)
                last = grp + (si + 1) * seq - 1
                snew_ref[sq, h] = s * tt[:, last:last + 1] + _dot(kt, vm)
            o_parts.append(o_g)
        o = o_intra + jnp.concatenate(o_parts, axis=0)
        og_ref[:, vcol] = _gla_finish_head(o, r, gon_ref)
    out = _dot(og_ref[...], wout_ref[...])
    o_ref[...] = x + _rms(out, g_ref[1:2, :])


def _gla_weight_specs(nidx):
    zero = (lambda *_: (0, 0))
    resident = dict(pipeline_mode=pl.Buffered(1))
    del nidx
    return [
        pl.BlockSpec((2, D_MODEL), zero),
        pl.BlockSpec((D_MODEL, 2 * GLA_KEY_DIM + 2 * GLA_VAL_DIM), zero, **resident),
        pl.BlockSpec((D_MODEL, LANE), zero, **resident),
        pl.BlockSpec((LANE, GLA_KEY_DIM), zero, **resident),
        pl.BlockSpec((1, GLA_KEY_DIM), zero),
        pl.BlockSpec((1, GLA_DV), zero),
        pl.BlockSpec((GLA_VAL_DIM, D_MODEL), zero, **resident),
    ]


def _gla_weights(w_in, w_g2, b_g, g_onorm, w_out):
    n_main = 2 * GLA_KEY_DIM + 2 * GLA_VAL_DIM
    w_main = w_in[:, :n_main].astype(BF16)
    w_glr = jnp.pad(w_in[:, n_main:], ((0, 0), (0, LANE - GATE_RANK))).astype(BF16)
    w_g2p = jnp.pad(w_g2, ((0, LANE - GATE_RANK), (0, 0))).astype(BF16)
    return (w_main, w_glr, w_g2p, b_g.reshape(1, GLA_KEY_DIM), g_onorm.reshape(1, GLA_DV),
            w_out.astype(BF16))


def _gla_prompt(x, g2, weights):
    bsz, t, _ = x.shape
    cum, _ = _gla_tables(GLA_CUM_ROWS, GLA_CHUNK)
    _, masks = _gla_tables(GLA_CHUNK, GLA_CHUNK)
    n_main = 2 * GLA_KEY_DIM + 2 * GLA_VAL_DIM
    const2 = lambda b, i: (0, 0)
    return pl.pallas_call(
        _gla_prompt_kernel,
        grid=(bsz, t // GLA_TBLOCK),
        in_specs=[pl.BlockSpec((1, GLA_TBLOCK, D_MODEL), lambda b, i: (b, i, 0))]
        + _gla_weight_specs(2)
        + [pl.BlockSpec(cum.shape, const2),
           pl.BlockSpec(masks.shape, lambda b, i: (0, 0, 0))],
        out_specs=[
            pl.BlockSpec((1, GLA_TBLOCK, D_MODEL), lambda b, i: (b, i, 0)),
            pl.BlockSpec((1, GLA_HEADS, GLA_DK, GLA_DV), lambda b, i: (b, 0, 0, 0)),
        ],
        out_shape=[
            jax.ShapeDtypeStruct(x.shape, F32),
            jax.ShapeDtypeStruct((bsz, GLA_HEADS, GLA_DK, GLA_DV), F32),
        ],
        scratch_shapes=[
            pltpu.VMEM((GLA_TBLOCK, D_MODEL), BF16),
            pltpu.VMEM((GLA_TBLOCK, n_main), F32),
            pltpu.VMEM((GLA_TBLOCK, GLA_KEY_DIM), F32),
            pltpu.VMEM((GLA_TBLOCK, GLA_KEY_DIM), F32),
            pltpu.VMEM((GLA_TBLOCK, GLA_VAL_DIM), BF16),
            pltpu.VMEM((GLA_HEADS, GLA_DK, GLA_DV), F32),
            pltpu.VMEM((GLA_TBLOCK // GLA_CHUNK * GLA_HEADS, GLA_CHUNK, GLA_CHUNK), BF16),
            pltpu.VMEM((GLA_TBLOCK, GLA_KEY_DIM), BF16),
            pltpu.VMEM((GLA_TBLOCK // GLA_CHUNK * GLA_HEADS, LANE, GLA_DK), F32),
        ],
        compiler_params=pltpu.CompilerParams(
            dimension_semantics=("arbitrary", "arbitrary"), vmem_limit_bytes=VMEM_LIMIT),
        name="gla_prompt",
    )(x, g2, *weights, cum, masks)


def _gla_sample(x, s0, g2, weights):
    bsz, seq, _ = x.shape
    rows = GLA_SAMPLE_SEQS * seq
    cum, masks = _gla_tables(rows, seq)
    n_main = 2 * GLA_KEY_DIM + 2 * GLA_VAL_DIM
    state_spec = pl.BlockSpec((GLA_SAMPLE_SEQS, GLA_HEADS, GLA_DK, GLA_DV), lambda i: (i, 0, 0, 0))
    y, snew = pl.pallas_call(
        functools.partial(_gla_sample_kernel, seq=seq),
        grid=(bsz // GLA_SAMPLE_SEQS,),
        in_specs=[pl.BlockSpec((rows, D_MODEL), lambda i: (i, 0))]
        + _gla_weight_specs(1)
        + [pl.BlockSpec(cum.shape, lambda i: (0, 0)),
           pl.BlockSpec(masks.shape, lambda i: (0, 0, 0)),
           state_spec],
        out_specs=[pl.BlockSpec((rows, D_MODEL), lambda i: (i, 0)), state_spec],
        out_shape=[
            jax.ShapeDtypeStruct((bsz * seq, D_MODEL), F32),
            jax.ShapeDtypeStruct(s0.shape, F32),
        ],
        scratch_shapes=[
            pltpu.VMEM((rows, D_MODEL), BF16),
            pltpu.VMEM((rows, n_main), F32),
            pltpu.VMEM((rows, GLA_KEY_DIM), F32),
            pltpu.VMEM((rows, GLA_KEY_DIM), F32),
            pltpu.VMEM((rows, GLA_VAL_DIM), BF16),
        ],
        compiler_params=pltpu.CompilerParams(
            dimension_semantics=("arbitrary",), vmem_limit_bytes=VMEM_LIMIT),
        name="gla_sample",
    )(x.reshape(bsz * seq, D_MODEL), g2, *weights, cum, masks, s0)
    return y.reshape(x.shape), snew


def _s5_discretize(lam_re, lam_im, log_dt):
    dt = jnp.exp(log_dt)
    mag = jnp.exp(lam_re * dt)
    ang = lam_im * dt
    a_re, a_im = mag * jnp.cos(ang), mag * jnp.sin(ang)
    nr, ni = a_re - 1.0, a_im
    den = lam_re * lam_re + lam_im * lam_im
    f_re = (nr * lam_re + ni * lam_im) / den
    f_im = (ni * lam_re - nr * lam_im) / den
    return a_re, a_im, f_re, f_im


def _s5_prep_kernel(lre_ref, lim_ref, ldt_ref, bre_ref, bim_ref, cre_ref, cim_ref,
                    lref_ref, limf_ref, ldtf_ref, wb_ref, wc_ref, are_ref, aim_ref):
    _, _, f_re, f_im = _s5_discretize(lre_ref[...], lim_ref[...], ldt_ref[...])
    b_re, b_im = bre_ref[...], bim_ref[...]
    row = lax.broadcasted_iota(jnp.int32, (S5_SUPER_CH, S5_SUPER_ST), 0)
    col = lax.broadcasted_iota(jnp.int32, (S5_SUPER_CH, S5_SUPER_ST), 1)
    own = (row // S5_GROUP) == (col // S5_STATE)
    wb_ref[0, :, 0:S5_SUPER_ST] = jnp.where(own, f_re * b_re - f_im * b_im, 0.0).astype(BF16)
    wb_ref[0, :, S5_SUPER_ST:] = jnp.where(own, f_re * b_im + f_im * b_re, 0.0).astype(BF16)
    rowc = lax.broadcasted_iota(jnp.int32, (S5_SUPER_ST, S5_SUPER_CH), 0)
    colc = lax.broadcasted_iota(jnp.int32, (S5_SUPER_ST, S5_SUPER_CH), 1)
    ownc = (rowc // S5_STATE) == (colc // S5_GROUP)
    wc_ref[0, 0:S5_SUPER_ST, :] = jnp.where(ownc, cre_ref[...], 0.0).astype(BF16)
    wc_ref[0, S5_SUPER_ST:, :] = jnp.where(ownc, -cim_ref[...], 0.0).astype(BF16)
    a_re, a_im, _, _ = _s5_discretize(lref_ref[...], limf_ref[...], ldtf_ref[...])
    are_ref[...] = a_re
    aim_ref[...] = a_im


def _s5_prep(lam_re, lam_im, log_dt, b_re, b_im, c_re, c_im):
    g, p, ch, sg = S5_GROUPS, S5_STATE, S5_GROUP, S5_SUPER

    def rows_by_group(a_gp):
        return jnp.broadcast_to(a_gp[:, None, None, :], (g, ch, sg, p)).reshape(g * ch, sg * p)

    def b_layout(b):
        bt = jnp.transpose(b, (0, 2, 1))
        return jnp.broadcast_to(bt[:, :, None, :], (g, ch, sg, p)).reshape(g * ch, sg * p)

    def c_layout(cm):
        ct = jnp.transpose(cm.reshape(g // sg, sg, ch, p), (3, 0, 1, 2)).reshape(p, g * ch)
        return jnp.broadcast_to(ct[None], (sg, p, g * ch)).reshape(sg * p, g * ch)

    ldt_gp = jnp.broadcast_to(log_dt[:, None], (g, p))
    big = pl.BlockSpec((S5_SUPER_CH, S5_SUPER_ST), lambda j: (j, 0))
    cspec = pl.BlockSpec((S5_SUPER_ST, S5_SUPER_CH), lambda j: (0, j))
    flat = pl.BlockSpec((1, S5_SUPER_ST), lambda j: (0, j))
    return pl.pallas_call(
        _s5_prep_kernel,
        grid=(S5_NSUPER,),
        in_specs=[big] * 5 + [cspec] * 2 + [flat] * 3,
        out_specs=[
            pl.BlockSpec((1, S5_SUPER_CH, 2 * S5_SUPER_ST), lambda j: (j, 0, 0)),
            pl.BlockSpec((1, 2 * S5_SUPER_ST, S5_SUPER_CH), lambda j: (j, 0, 0)),
            flat, flat,
        ],
        out_shape=[
            jax.ShapeDtypeStruct((S5_NSUPER, S5_SUPER_CH, 2 * S5_SUPER_ST), BF16),
            jax.ShapeDtypeStruct((S5_NSUPER, 2 * S5_SUPER_ST, S5_SUPER_CH), BF16),
            jax.ShapeDtypeStruct((1, S5_FLAT), F32),
            jax.ShapeDtypeStruct((1, S5_FLAT), F32),
        ],
        compiler_params=pltpu.CompilerParams(dimension_semantics=("arbitrary",)),
        name="s5_prep",
    )(rows_by_group(lam_re), rows_by_group(lam_im), rows_by_group(ldt_gp),
      b_layout(b_re), b_layout(b_im), c_layout(c_re), c_layout(c_im),
      lam_re.reshape(1, S5_FLAT), lam_im.reshape(1, S5_FLAT), ldt_gp.reshape(1, S5_FLAT))


def _s5_kernel(x_ref, g_ref, wb_ref, wc_ref, are_ref, aim_ref, d_ref, wglu_ref, bglu_ref,
               h0re_ref, h0im_ref, perm_ref, unperm_ref, o_ref, hre_ref, him_ref,
               u_ref, sre_ref, sim_ref, z_ref, *, bsz, steps):
    i = pl.program_id(0)
    rows = bsz * steps

    @pl.when(i == 0)
    def _():
        hre_ref[...] = h0re_ref[...]
        him_ref[...] = h0im_ref[...]

    x = x_ref[...].reshape(rows, D_MODEL)
    u = _dot(perm_ref[...], _rms(x, g_ref[0:1, :]).astype(BF16))
    u_ref[...] = u
    ub = u.astype(BF16)
    for j in range(S5_NSUPER):
        bu = _dot(ub[:, j * S5_SUPER_CH:(j + 1) * S5_SUPER_CH], wb_ref[j])
        sre_ref[:, j * S5_SUPER_ST:(j + 1) * S5_SUPER_ST] = bu[:, :S5_SUPER_ST]
        sim_ref[:, j * S5_SUPER_ST:(j + 1) * S5_SUPER_ST] = bu[:, S5_SUPER_ST:]

    for lc in range(S5_FLAT // S5_LANES):
        ls = slice(lc * S5_LANES, (lc + 1) * S5_LANES)
        a_re = jnp.broadcast_to(are_ref[:, ls], (SUBLANE, S5_LANES))
        a_im = jnp.broadcast_to(aim_ref[:, ls], (SUBLANE, S5_LANES))
        for rt in range(bsz // SUBLANE):
            rs = slice(rt * SUBLANE, (rt + 1) * SUBLANE)

            def step(t, carry, ls=ls, rt=rt, a_re=a_re, a_im=a_im):
                h_re, h_im = carry
                rows = pl.ds(pl.multiple_of(t * bsz + rt * SUBLANE, SUBLANE), SUBLANE)
                n_re = a_re * h_re - a_im * h_im + sre_ref[rows, ls]
                n_im = a_re * h_im + a_im * h_re + sim_ref[rows, ls]
                sre_ref[rows, ls] = n_re
                sim_ref[rows, ls] = n_im
                return n_re, n_im

            h_re, h_im = lax.fori_loop(0, steps, step, (hre_ref[rs, ls], him_ref[rs, ls]),
                                       unroll=True)
            hre_ref[rs, ls] = h_re
            him_ref[rs, ls] = h_im

    for j in range(S5_NSUPER):
        st = slice(j * S5_SUPER_ST, (j + 1) * S5_SUPER_ST)
        ch = slice(j * S5_SUPER_CH, (j + 1) * S5_SUPER_CH)
        y = (_dot(sre_ref[:, st].astype(BF16), wc_ref[j, 0:S5_SUPER_ST, :])
             + _dot(sim_ref[:, st].astype(BF16), wc_ref[j, S5_SUPER_ST:, :]))
        z_ref[:, ch] = (y + d_ref[:, ch] * u_ref[:, ch]).astype(BF16)
    z = _dot(unperm_ref[...], z_ref[...]).astype(BF16)
    zz = _dot(z, wglu_ref[...]) + bglu_ref[...]
    out = zz[:, :D_MODEL] * _sigmoid(zz[:, D_MODEL:])
    o_ref[...] = (x + _rms(out, g_ref[1:2, :])).reshape(o_ref.shape)


def _s5(x, steps_per_block, g2, prep, d_skip, w_glu, b_glu, h0_re, h0_im):
    wb, wc, a_re, a_im = prep
    bsz, t, _ = x.shape
    rows = steps_per_block * bsz
    r = np.arange(rows)
    perm = np.zeros((rows, rows), np.float32)
    perm[(r % steps_per_block) * bsz + r // steps_per_block, r] = 1.0
    zero2 = lambda i: (0, 0)
    zero3 = lambda i: (0, 0, 0)
    resident = dict(pipeline_mode=pl.Buffered(1))
    state = pl.BlockSpec((bsz, S5_FLAT), zero2)
    if steps_per_block == t:
        x = x.reshape(1, rows, D_MODEL)
        xspec = pl.BlockSpec((1, rows, D_MODEL), lambda i: (0, 0, 0))
    else:
        assert steps_per_block % SUBLANE == 0
        xspec = pl.BlockSpec((bsz, steps_per_block, D_MODEL), lambda i: (0, i, 0))
    y, h_re, h_im = pl.pallas_call(
        functools.partial(_s5_kernel, bsz=bsz, steps=steps_per_block),
        grid=(t // steps_per_block,),
        in_specs=[
            xspec,
            pl.BlockSpec((2, D_MODEL), zero2),
            pl.BlockSpec(wb.shape, zero3, **resident),
            pl.BlockSpec(wc.shape, zero3, **resident),
            pl.BlockSpec((1, S5_FLAT), zero2),
            pl.BlockSpec((1, S5_FLAT), zero2),
            pl.BlockSpec((1, D_MODEL), zero2),
            pl.BlockSpec((D_MODEL, 2 * D_MODEL), zero2, **resident),
            pl.BlockSpec((1, 2 * D_MODEL), zero2),
            state, state,
            pl.BlockSpec((rows, rows), zero2),
            pl.BlockSpec((rows, rows), zero2),
        ],
        out_specs=[xspec, state, state],
        out_shape=[
            jax.ShapeDtypeStruct(x.shape, F32),
            jax.ShapeDtypeStruct((bsz, S5_FLAT), F32),
            jax.ShapeDtypeStruct((bsz, S5_FLAT), F32),
        ],
        scratch_shapes=[
            pltpu.VMEM((rows, D_MODEL), F32),
            pltpu.VMEM((rows, S5_FLAT), F32),
            pltpu.VMEM((rows, S5_FLAT), F32),
            pltpu.VMEM((rows, D_MODEL), BF16),
        ],
        compiler_params=pltpu.CompilerParams(
            dimension_semantics=("arbitrary",), vmem_limit_bytes=VMEM_LIMIT),
        name="s5",
    )(x, g2, wb, wc, a_re, a_im, d_skip.reshape(1, D_MODEL), w_glu.astype(BF16),
      b_glu.reshape(1, 2 * D_MODEL), h0_re, h0_im,
      jnp.asarray(perm, BF16), jnp.asarray(perm.T, BF16))
    return y.reshape(bsz, t, D_MODEL), h_re, h_im


def kernel(x_prompt, x_sample, state_gla, state_s5_re, state_s5_im, norm_g, w_ffn_gu, w_ffn_down,
           gla_w_in, gla_w_g2, gla_b_g, gla_g_onorm, gla_w_out,
           s5_lam_re, s5_lam_im, s5_log_dt, s5_b_re, s5_b_im, s5_c_re, s5_c_im, s5_d, s5_w_glu,
           s5_b_glu):
    pb, pt, _ = x_prompt.shape
    sb, st, _ = x_sample.shape

    def ffn(xp, xs, layer, which):
        g2 = norm_g[layer, 4 * which:4 * which + 2]
        xs, wgu_bf, wd_bf = _ffn_stream(xs, g2, w_ffn_gu, w_ffn_down, layer, which)
        return _ffn_resident(xp, g2, wgu_bf, wd_bf), xs

    gla_w = _gla_weights(gla_w_in[0], gla_w_g2[0], gla_b_g[0], gla_g_onorm[0], gla_w_out[0])
    xp, xs = ffn(x_prompt, x_sample, 0, 0)
    xp, gla_p = _gla_prompt(xp, norm_g[0, 2:4], gla_w)
    xs, gla_s = _gla_sample(xs, state_gla.reshape(sb, GLA_HEADS, GLA_DK, GLA_DV), norm_g[0, 2:4], gla_w)
    xp, xs = ffn(xp, xs, 0, 1)

    xp, xs = ffn(xp, xs, 1, 0)
    prep = _s5_prep(s5_lam_re[0], s5_lam_im[0], s5_log_dt[0], s5_b_re[0], s5_b_im[0],
                    s5_c_re[0], s5_c_im[0])
    zeros = jnp.zeros((pb, S5_FLAT), F32)
    s5_args = (norm_g[1, 2:4], prep, s5_d[0], s5_w_glu[0], s5_b_glu[0])
    xp, hre_p, him_p = _s5(xp, S5_STEPS, *s5_args, zeros, zeros)
    xs, hre_s, him_s = _s5(xs, st, *s5_args,
                           state_s5_re.reshape(sb, S5_FLAT), state_s5_im.reshape(sb, S5_FLAT))
    y_prompt, y_sample = ffn(xp, xs, 1, 1)

    def s5_state(h, b):
        return h.reshape(1, b, S5_GROUPS, S5_STATE)

    return (y_prompt, y_sample, gla_p.reshape(1, pb, GLA_HEADS, GLA_DK, GLA_DV),
            s5_state(hre_p, pb), s5_state(him_p, pb),
            gla_s.reshape(1, sb, GLA_HEADS, GLA_DK, GLA_DV),
            s5_state(hre_s, sb), s5_state(him_s, sb))
```

```python
import functools
import math

import jax
import jax.numpy as jnp
import numpy as np
from jax import lax
from jax.experimental import pallas as pl
from jax.experimental.pallas import tpu as pltpu

F32 = jnp.float32
BF16 = jnp.bfloat16

D_MODEL = 1024
D_FF = 2816
GLA_HEADS = 4
GLA_DK = 128
GLA_DV = 256
GLA_KEY_DIM = GLA_HEADS * GLA_DK
GLA_VAL_DIM = GLA_HEADS * GLA_DV
GATE_RANK = 16
GATE_TAU = 16.0
S5_GROUP = 16
S5_GROUPS = 64
S5_STATE = 64
S5_FLAT = S5_GROUPS * S5_STATE
S5_SUPER = 8
S5_NSUPER = S5_GROUPS // S5_SUPER
S5_SUPER_CH = S5_SUPER * S5_GROUP
S5_SUPER_ST = S5_SUPER * S5_STATE
EPS = 1e-6
LANE = 128
SUBLANE = 8
VMEM_LIMIT = 56 * 1024 * 1024

FFN_ROWS = 1024
FFN_GROUP = 512
FFN_CHUNK = 256
FFN_SLOTS = 4
GLA_CHUNK = 64
GLA_TBLOCK = 512
GLA_PROJ_CHUNK = 256
GLA_CUM_ROWS = 256
LOG2E = math.log2(math.e)
GLA_SAMPLE_SEQS = 16
S5_LANES = 512
S5_STEPS = 32
S5_GROUPS_PER_STEP = 4
S5_REGIONS = 2


def _dot(a, b):
    return jnp.dot(a, b, preferred_element_type=F32)


def _dot_nt(a, b):
    return lax.dot_general(a, b, (((1,), (1,)), ((), ())), preferred_element_type=F32)


def _rms(x, g):
    ms = jnp.mean(x * x, axis=-1, keepdims=True)
    return x * lax.rsqrt(ms + EPS) * g


def _sigmoid(x):
    return 1.0 / (1.0 + jnp.exp(-x))


def _split_bf16(x):
    hi = x.astype(BF16)
    lo = (x - hi.astype(F32)).astype(BF16)
    return hi, lo


def _ffn_row_groups(n_rows):
    return [slice(r0, min(r0 + FFN_GROUP, n_rows)) for r0 in range(0, n_rows, FFN_GROUP)]


def _ffn_hidden_chunk(xn_ref, wgu_ref, act_ref, rows, c):
    lo = c * FFN_CHUNK
    gate = _dot(xn_ref[rows, :], wgu_ref[:, lo:lo + FFN_CHUNK])
    up = _dot(xn_ref[rows, :], wgu_ref[:, D_FF + lo:D_FF + lo + FFN_CHUNK])
    act_ref[rows, lo:lo + FFN_CHUNK] = (gate * _sigmoid(gate) * up).astype(BF16)


def _ffn_finish(x_ref, g_ref, wd_ref, o_ref, act_ref, rows):
    y = _dot(act_ref[rows, :], wd_ref[...])
    o_ref[rows, :] = x_ref[rows, :] + 0.5 * _rms(y, g_ref[1:2, :])


def _ffn_stream_kernel(x_ref, g_ref, wgu_hbm, wd_hbm, o_ref, wgu_ref, wd_ref,
                       xn_ref, act_ref, gu_stage, d_stage, sem, *, layer, which):
    n_chunks = D_FF // FFN_CHUNK
    groups = _ffn_row_groups(x_ref.shape[0])
    xn_ref[...] = _rms(x_ref[...], g_ref[0:1, :]).astype(BF16)

    def chunk_slices(c):
        lo = c * FFN_CHUNK
        return ((slice(None), pl.ds(lo, FFN_CHUNK)), (slice(None), pl.ds(D_FF + lo, FFN_CHUNK)),
                (pl.ds(lo, FFN_CHUNK), slice(None)))

    def fetches(c):
        slot = c % FFN_SLOTS
        gate, up, down = chunk_slices(c)
        return (
            pltpu.make_async_copy(wgu_hbm.at[(layer, which) + gate], gu_stage.at[slot, 0], sem.at[slot, 0]),
            pltpu.make_async_copy(wgu_hbm.at[(layer, which) + up], gu_stage.at[slot, 1], sem.at[slot, 1]),
            pltpu.make_async_copy(wd_hbm.at[(layer, which) + down], d_stage.at[slot], sem.at[slot, 2]),
        )

    ahead = FFN_SLOTS - 1
    for c in range(min(ahead, n_chunks)):
        for cp in fetches(c):
            cp.start()
    for c in range(n_chunks):
        slot, lo = c % FFN_SLOTS, c * FFN_CHUNK
        for cp in fetches(c):
            cp.wait()
        if c + ahead < n_chunks:
            for cp in fetches(c + ahead):
                cp.start()
        wgu_ref[:, lo:lo + FFN_CHUNK] = gu_stage[slot, 0].astype(BF16)
        wgu_ref[:, D_FF + lo:D_FF + lo + FFN_CHUNK] = gu_stage[slot, 1].astype(BF16)
        wd_ref[lo:lo + FFN_CHUNK, :] = d_stage[slot].astype(BF16)
        for rows in groups:
            _ffn_hidden_chunk(xn_ref, wgu_ref, act_ref, rows, c)
    for rows in groups:
        _ffn_finish(x_ref, g_ref, wd_ref, o_ref, act_ref, rows)


def _ffn_resident_kernel(x_ref, g_ref, wgu_ref, wd_ref, o_ref, xn_ref, act_ref):
    xn_ref[...] = _rms(x_ref[...], g_ref[0:1, :]).astype(BF16)
    for rows in _ffn_row_groups(x_ref.shape[0]):
        for c in range(D_FF // FFN_CHUNK):
            _ffn_hidden_chunk(xn_ref, wgu_ref, act_ref, rows, c)
        _ffn_finish(x_ref, g_ref, wd_ref, o_ref, act_ref, rows)


def _ffn_stream(x, g2, wgu, wd, layer, which):
    shape = x.shape
    x = x.reshape(-1, D_MODEL)
    n = x.shape[0]
    zero = lambda i: (0, 0)
    y, wgu_bf, wd_bf = pl.pallas_call(
        functools.partial(_ffn_stream_kernel, layer=layer, which=which),
        grid=(1,),
        in_specs=[
            pl.BlockSpec((n, D_MODEL), zero),
            pl.BlockSpec((2, D_MODEL), zero),
            pl.BlockSpec(memory_space=pl.ANY),
            pl.BlockSpec(memory_space=pl.ANY),
        ],
        out_specs=[
            pl.BlockSpec((n, D_MODEL), zero),
            pl.BlockSpec((D_MODEL, 2 * D_FF), zero, pipeline_mode=pl.Buffered(1)),
            pl.BlockSpec((D_FF, D_MODEL), zero, pipeline_mode=pl.Buffered(1)),
        ],
        out_shape=[
            jax.ShapeDtypeStruct((n, D_MODEL), F32),
            jax.ShapeDtypeStruct((D_MODEL, 2 * D_FF), BF16),
            jax.ShapeDtypeStruct((D_FF, D_MODEL), BF16),
        ],
        scratch_shapes=[
            pltpu.VMEM((n, D_MODEL), BF16),
            pltpu.VMEM((n, D_FF), BF16),
            pltpu.VMEM((FFN_SLOTS, 2, D_MODEL, FFN_CHUNK), F32),
            pltpu.VMEM((FFN_SLOTS, FFN_CHUNK, D_MODEL), F32),
            pltpu.SemaphoreType.DMA((FFN_SLOTS, 3)),
        ],
        compiler_params=pltpu.CompilerParams(
            dimension_semantics=("arbitrary",), vmem_limit_bytes=VMEM_LIMIT),
        name="ffn_stream",
    )(x, g2, wgu, wd)
    return y.reshape(shape), wgu_bf, wd_bf


def _ffn_resident(x, g2, wgu_bf, wd_bf):
    shape = x.shape
    x = x.reshape(-1, D_MODEL)
    n = x.shape[0]
    tm = min(FFN_ROWS, n)
    assert n % tm == 0
    zero = lambda i: (0, 0)
    resident = dict(pipeline_mode=pl.Buffered(1))
    return pl.pallas_call(
        _ffn_resident_kernel,
        grid=(n // tm,),
        in_specs=[
            pl.BlockSpec((tm, D_MODEL), lambda i: (i, 0)),
            pl.BlockSpec((2, D_MODEL), zero),
            pl.BlockSpec((D_MODEL, 2 * D_FF), zero, **resident),
            pl.BlockSpec((D_FF, D_MODEL), zero, **resident),
        ],
        out_specs=pl.BlockSpec((tm, D_MODEL), lambda i: (i, 0)),
        out_shape=jax.ShapeDtypeStruct((n, D_MODEL), F32),
        scratch_shapes=[pltpu.VMEM((tm, D_MODEL), BF16), pltpu.VMEM((tm, D_FF), BF16)],
        compiler_params=pltpu.CompilerParams(
            dimension_semantics=("arbitrary",), vmem_limit_bytes=VMEM_LIMIT),
        name="ffn",
    )(x, g2, wgu_bf, wd_bf).reshape(shape)


def _gla_tables(rows, seq):
    t = np.arange(rows)[:, None]
    r = np.arange(rows)[None, :]
    cum = ((t // seq) == (r // seq)) & (r <= t)
    masks = [t == r]
    m = seq // 2
    while m >= 1:
        masks.append(((t // (2 * m)) == (r // (2 * m))) & (t % (2 * m) >= m) & (r % (2 * m) < m))
        m //= 2
    return jnp.asarray(cum.astype(np.float32), BF16), jnp.asarray(np.stack(masks).astype(np.float32))


def _gla_project_cols(hn_ref, win_ref, proj_ref, c0):
    proj_ref[:, c0:c0 + GLA_PROJ_CHUNK] = _dot(hn_ref[...], win_ref[:, c0:c0 + GLA_PROJ_CHUNK])


def _gla_project(x, g_ref, win_ref, wglr_ref, wg2_ref, bg_ref, cum_ref, hn_ref, proj_ref, lf_ref, b_ref):
    hn = _rms(x, g_ref[0:1, :]).astype(BF16)
    hn_ref[...] = hn
    glr = _dot(hn, wglr_ref[...]).astype(BF16)
    gate = _dot(glr, wg2_ref[...]) + bg_ref[...]
    for c0 in range(0, 2 * GLA_KEY_DIM, GLA_PROJ_CHUNK):
        _gla_project_cols(hn_ref, win_ref, proj_ref, c0)
    lf = (jnp.minimum(gate, 0.0) - jnp.log1p(jnp.exp(-jnp.abs(gate)))) * (1.0 / GATE_TAU)
    lf_ref[...] = lf
    hi, lo = _split_bf16(lf)
    span = cum_ref.shape[0]
    for r0 in range(0, lf.shape[0], span):
        b_ref[r0:r0 + span, :] = (_dot(cum_ref[...], hi[r0:r0 + span]) + _dot(cum_ref[...], lo[r0:r0 + span]))


def _block_rows(ref, r0, col, rows, period, offset):
    def bc(row, n):
        tile = ref[pl.ds(r0 + row // SUBLANE * SUBLANE, SUBLANE), col]
        return jnp.broadcast_to(tile[row % SUBLANE:row % SUBLANE + 1, :], (n, GLA_DK))

    if period >= SUBLANE:
        parts = [bc(p0 + offset, period) for p0 in range(0, rows, period)]
    else:
        assert 2 * period == SUBLANE
        low = lax.broadcasted_iota(jnp.int32, (SUBLANE, 1), 0) < period
        parts = [jnp.where(low, bc(t0 + offset, SUBLANE), bc(t0 + period + offset, SUBLANE))
                 for t0 in range(0, rows, SUBLANE)]
    return parts[0] if len(parts) == 1 else jnp.concatenate(parts, axis=0)


def _gla_attention(qs, k, b, lf, b_ref, r0, col, rows, seq, mask_ref):
    qb, kb = qs.astype(BF16), k.astype(BF16)
    att = _dot_nt(qb, kb) * mask_ref[0]
    lvl, m = 1, seq // 2
    while m >= 1:
        if m > 1:
            d = b - _block_rows(b_ref, r0, col, rows, 2 * m, m - 1)
            w = jnp.exp2(jnp.abs(d) * (-LOG2E))
        else:
            odd = lax.broadcasted_iota(jnp.int32, (rows, 1), 0) % 2 == 1
            w = jnp.where(odd, jnp.exp(lf), 1.0)
        wb = w.astype(BF16)
        att = att + _dot_nt(qb * wb, kb * wb) * mask_ref[lvl]
        lvl, m = lvl + 1, m // 2
    return att


def _gla_finish_head(o, r, gon_ref):
    on = _rms(o, gon_ref[...])
    return (on * (r * _sigmoid(r))).astype(BF16)


def _gla_prompt_kernel(x_ref, g_ref, win_ref, wglr_ref, wg2_ref, bg_ref, gon_ref, wout_ref,
                       cum_ref, mask_ref, o_ref, snew_ref,
                       hn_ref, proj_ref, lf_ref, b_ref, og_ref, s_ref, att_ref, qb_ref, tt_ref):
    tb = pl.program_id(1)
    c = GLA_CHUNK

    @pl.when(tb == 0)
    def _():
        s_ref[...] = jnp.zeros_like(s_ref)

    x = x_ref[0]
    _gla_project(x, g_ref, win_ref, wglr_ref, wg2_ref, bg_ref, cum_ref, hn_ref, proj_ref, lf_ref, b_ref)

    late_cols = list(range(2 * GLA_KEY_DIM, proj_ref.shape[1], GLA_PROJ_CHUNK))
    n_heads_total = GLA_TBLOCK // c * GLA_HEADS
    for ci, r0 in enumerate(range(0, GLA_TBLOCK, c)):
        rows = slice(r0, r0 + c)
        for h in range(GLA_HEADS):
            done = ci * GLA_HEADS + h
            for c0 in late_cols[done * len(late_cols) // n_heads_total:
                                (done + 1) * len(late_cols) // n_heads_total]:
                _gla_project_cols(hn_ref, win_ref, proj_ref, c0)
            kcol = slice(h * GLA_DK, (h + 1) * GLA_DK)
            qs = proj_ref[rows, kcol] * (GLA_DK ** -0.5)
            k = proj_ref[rows, GLA_KEY_DIM + h * GLA_DK:GLA_KEY_DIM + (h + 1) * GLA_DK]
            b = b_ref[rows, kcol]
            att = _gla_attention(qs, k, b, lf_ref[rows, kcol], b_ref, r0, kcol, c, c, mask_ref)
            att_ref[ci * GLA_HEADS + h] = att.astype(BF16)
            qb_ref[rows, kcol] = (qs * jnp.exp(b)).astype(BF16)
            khat = k * jnp.exp(_block_rows(b_ref, r0, kcol, c, c, c - 1) - b)
            tile = jnp.concatenate(
                [khat, jnp.exp(b[c - SUBLANE:c, :]), jnp.zeros((LANE - c - SUBLANE, GLA_DK), F32)], axis=0)
            tt_ref[ci * GLA_HEADS + h] = jnp.transpose(tile)

    for ci, r0 in enumerate(range(0, GLA_TBLOCK, c)):
        rows = slice(r0, r0 + c)
        for h in range(GLA_HEADS):
            kcol = slice(h * GLA_DK, (h + 1) * GLA_DK)
            vcol = slice(h * GLA_DV, (h + 1) * GLA_DV)
            v = proj_ref[rows, 2 * GLA_KEY_DIM + h * GLA_DV:2 * GLA_KEY_DIM + (h + 1) * GLA_DV].astype(BF16)
            r = proj_ref[rows, 2 * GLA_KEY_DIM + GLA_VAL_DIM + h * GLA_DV:
                         2 * GLA_KEY_DIM + GLA_VAL_DIM + (h + 1) * GLA_DV]
            s = s_ref[h]
            tt = tt_ref[ci * GLA_HEADS + h]
            o = _dot(att_ref[ci * GLA_HEADS + h], v) + _dot(qb_ref[rows, kcol], s.astype(BF16))
            decay = tt[:, c + SUBLANE - 1:c + SUBLANE]
            s_ref[h] = s * decay + _dot(tt[:, 0:c].astype(BF16), v)
            og_ref[rows, vcol] = _gla_finish_head(o, r, gon_ref)
    out = _dot(og_ref[...], wout_ref[...])
    o_ref[0] = x + _rms(out, g_ref[1:2, :])

    @pl.when(tb == pl.num_programs(1) - 1)
    def _():
        snew_ref[0] = s_ref[...]


def _gla_sample_kernel(x_ref, g_ref, win_ref, wglr_ref, wg2_ref, bg_ref, gon_ref, wout_ref,
                       cum_ref, mask_ref, s0_ref, o_ref, snew_ref,
                       hn_ref, proj_ref, lf_ref, b_ref, og_ref, *, seq):
    rows = x_ref.shape[0]
    grp = 2 * SUBLANE
    x = x_ref[...]
    _gla_project(x, g_ref, win_ref, wglr_ref, wg2_ref, bg_ref, cum_ref, hn_ref, proj_ref, lf_ref, b_ref)
    for c0 in range(2 * GLA_KEY_DIM, proj_ref.shape[1], GLA_PROJ_CHUNK):
        _gla_project_cols(hn_ref, win_ref, proj_ref, c0)
    rid = lax.broadcasted_iota(jnp.int32, (grp, 1), 0)
    for h in range(GLA_HEADS):
        kcol = slice(h * GLA_DK, (h + 1) * GLA_DK)
        vcol = slice(h * GLA_DV, (h + 1) * GLA_DV)
        qs = proj_ref[:, kcol] * (GLA_DK ** -0.5)
        k = proj_ref[:, GLA_KEY_DIM + h * GLA_DK:GLA_KEY_DIM + (h + 1) * GLA_DK]
        v = proj_ref[:, 2 * GLA_KEY_DIM + h * GLA_DV:2 * GLA_KEY_DIM + (h + 1) * GLA_DV]
        r = proj_ref[:, 2 * GLA_KEY_DIM + GLA_VAL_DIM + h * GLA_DV:
                     2 * GLA_KEY_DIM + GLA_VAL_DIM + (h + 1) * GLA_DV]
        b = b_ref[:, kcol]
        att = _gla_attention(qs, k, b, lf_ref[:, kcol], b_ref, 0, kcol, rows, seq, mask_ref)
        eb = jnp.exp(b)
        qb = (qs * eb).astype(BF16)
        khat = k * jnp.exp(_block_rows(b_ref, 0, kcol, rows, seq, seq - 1) - b)
        o_intra = _dot(att.astype(BF16), v.astype(BF16))
        o_parts = []
        for gi in range(rows // grp):
            gr = slice(gi * grp, (gi + 1) * grp)
            tile = jnp.concatenate(
                [khat[gr], eb[gr], jnp.zeros((LANE - 2 * grp, GLA_DK), F32)], axis=0)
            tt = jnp.transpose(tile)
            kt = tt[:, 0:grp].astype(BF16)
            qg = qb[gr]
            vg = v[gr]
            o_g = jnp.zeros((grp, GLA_DV), F32)
            for si in range(grp // seq):
                sq = gi * (grp // seq) + si
                mine = (rid >= si * seq) & (rid < (si + 1) * seq)
                s = s0_ref[sq, h]
                o_g = o_g + jnp.where(mine, _dot(qg, s.astype(BF16)), 0.0)
                vm = jnp.where(mine, vg, 0.0).astype(BF16)
                last = grp + (si + 1) * seq - 1
                snew_ref[sq, h] = s * tt[:, last:last + 1] + _dot(kt, vm)
            o_parts.append(o_g)
        o = o_intra + jnp.concatenate(o_parts, axis=0)
        og_ref[:, vcol] = _gla_finish_head(o, r, gon_ref)
    out = _dot(og_ref[...], wout_ref[...])
    o_ref[...] = x + _rms(out, g_ref[1:2, :])


def _gla_weight_specs(nidx):
    zero = (lambda *_: (0, 0))
    resident = dict(pipeline_mode=pl.Buffered(1))
    del nidx
    return [
        pl.BlockSpec((2, D_MODEL), zero),
        pl.BlockSpec((D_MODEL, 2 * GLA_KEY_DIM + 2 * GLA_VAL_DIM), zero, **resident),
        pl.BlockSpec((D_MODEL, LANE), zero, **resident),
        pl.BlockSpec((LANE, GLA_KEY_DIM), zero, **resident),
        pl.BlockSpec((1, GLA_KEY_DIM), zero),
        pl.BlockSpec((1, GLA_DV), zero),
        pl.BlockSpec((GLA_VAL_DIM, D_MODEL), zero, **resident),
    ]


def _gla_weights(w_in, w_g2, b_g, g_onorm, w_out):
    n_main = 2 * GLA_KEY_DIM + 2 * GLA_VAL_DIM
    w_main = w_in[:, :n_main].astype(BF16)
    w_glr = jnp.pad(w_in[:, n_main:], ((0, 0), (0, LANE - GATE_RANK))).astype(BF16)
    w_g2p = jnp.pad(w_g2, ((0, LANE - GATE_RANK), (0, 0))).astype(BF16)
    return (w_main, w_glr, w_g2p, b_g.reshape(1, GLA_KEY_DIM), g_onorm.reshape(1, GLA_DV),
            w_out.astype(BF16))


def _gla_prompt(x, g2, weights):
    bsz, t, _ = x.shape
    cum, _ = _gla_tables(GLA_CUM_ROWS, GLA_CHUNK)
    _, masks = _gla_tables(GLA_CHUNK, GLA_CHUNK)
    n_main = 2 * GLA_KEY_DIM + 2 * GLA_VAL_DIM
    const2 = lambda b, i: (0, 0)
    return pl.pallas_call(
        _gla_prompt_kernel,
        grid=(bsz, t // GLA_TBLOCK),
        in_specs=[pl.BlockSpec((1, GLA_TBLOCK, D_MODEL), lambda b, i: (b, i, 0))]
        + _gla_weight_specs(2)
        + [pl.BlockSpec(cum.shape, const2),
           pl.BlockSpec(masks.shape, lambda b, i: (0, 0, 0))],
        out_specs=[
            pl.BlockSpec((1, GLA_TBLOCK, D_MODEL), lambda b, i: (b, i, 0)),
            pl.BlockSpec((1, GLA_HEADS, GLA_DK, GLA_DV), lambda b, i: (b, 0, 0, 0)),
        ],
        out_shape=[
            jax.ShapeDtypeStruct(x.shape, F32),
            jax.ShapeDtypeStruct((bsz, GLA_HEADS, GLA_DK, GLA_DV), F32),
        ],
        scratch_shapes=[
            pltpu.VMEM((GLA_TBLOCK, D_MODEL), BF16),
            pltpu.VMEM((GLA_TBLOCK, n_main), F32),
            pltpu.VMEM((GLA_TBLOCK, GLA_KEY_DIM), F32),
            pltpu.VMEM((GLA_TBLOCK, GLA_KEY_DIM), F32),
            pltpu.VMEM((GLA_TBLOCK, GLA_VAL_DIM), BF16),
            pltpu.VMEM((GLA_HEADS, GLA_DK, GLA_DV), F32),
            pltpu.VMEM((GLA_TBLOCK // GLA_CHUNK * GLA_HEADS, GLA_CHUNK, GLA_CHUNK), BF16),
            pltpu.VMEM((GLA_TBLOCK, GLA_KEY_DIM), BF16),
            pltpu.VMEM((GLA_TBLOCK // GLA_CHUNK * GLA_HEADS, LANE, GLA_DK), F32),
        ],
        compiler_params=pltpu.CompilerParams(
            dimension_semantics=("arbitrary", "arbitrary"), vmem_limit_bytes=VMEM_LIMIT),
        name="gla_prompt",
    )(x, g2, *weights, cum, masks)


def _gla_sample(x, s0, g2, weights):
    bsz, seq, _ = x.shape
    rows = GLA_SAMPLE_SEQS * seq
    cum, masks = _gla_tables(rows, seq)
    n_main = 2 * GLA_KEY_DIM + 2 * GLA_VAL_DIM
    state_spec = pl.BlockSpec((GLA_SAMPLE_SEQS, GLA_HEADS, GLA_DK, GLA_DV), lambda i: (i, 0, 0, 0))
    y, snew = pl.pallas_call(
        functools.partial(_gla_sample_kernel, seq=seq),
        grid=(bsz // GLA_SAMPLE_SEQS,),
        in_specs=[pl.BlockSpec((rows, D_MODEL), lambda i: (i, 0))]
        + _gla_weight_specs(1)
        + [pl.BlockSpec(cum.shape, lambda i: (0, 0)),
           pl.BlockSpec(masks.shape, lambda i: (0, 0, 0)),
           state_spec],
        out_specs=[pl.BlockSpec((rows, D_MODEL), lambda i: (i, 0)), state_spec],
        out_shape=[
            jax.ShapeDtypeStruct((bsz * seq, D_MODEL), F32),
            jax.ShapeDtypeStruct(s0.shape, F32),
        ],
        scratch_shapes=[
            pltpu.VMEM((rows, D_MODEL), BF16),
            pltpu.VMEM((rows, n_main), F32),
            pltpu.VMEM((rows, GLA_KEY_DIM), F32),
            pltpu.VMEM((rows, GLA_KEY_DIM), F32),
            pltpu.VMEM((rows, GLA_VAL_DIM), BF16),
        ],
        compiler_params=pltpu.CompilerParams(
            dimension_semantics=("arbitrary",), vmem_limit_bytes=VMEM_LIMIT),
        name="gla_sample",
    )(x.reshape(bsz * seq, D_MODEL), g2, *weights, cum, masks, s0)
    return y.reshape(x.shape), snew


def _s5_discretize(lam_re, lam_im, log_dt):
    dt = jnp.exp(log_dt)
    mag = jnp.exp(lam_re * dt)
    ang = lam_im * dt
    a_re, a_im = mag * jnp.cos(ang), mag * jnp.sin(ang)
    nr, ni = a_re - 1.0, a_im
    den = lam_re * lam_re + lam_im * lam_im
    f_re = (nr * lam_re + ni * lam_im) / den
    f_im = (ni * lam_re - nr * lam_im) / den
    return a_re, a_im, f_re, f_im


def _s5_prep_kernel(lre_ref, lim_ref, ldt_ref, bre_ref, bim_ref, cre_ref, cim_ref,
                    lref_ref, limf_ref, ldtf_ref, wb_ref, wc_ref, are_ref, aim_ref):
    _, _, f_re, f_im = _s5_discretize(lre_ref[...], lim_ref[...], ldt_ref[...])
    b_re, b_im = bre_ref[...], bim_ref[...]
    row = lax.broadcasted_iota(jnp.int32, (S5_SUPER_CH, S5_SUPER_ST), 0)
    col = lax.broadcasted_iota(jnp.int32, (S5_SUPER_CH, S5_SUPER_ST), 1)
    own = (row // S5_GROUP) == (col // S5_STATE)
    wb_ref[0, :, 0:S5_SUPER_ST] = jnp.where(own, f_re * b_re - f_im * b_im, 0.0).astype(BF16)
    wb_ref[0, :, S5_SUPER_ST:] = jnp.where(own, f_re * b_im + f_im * b_re, 0.0).astype(BF16)
    rowc = lax.broadcasted_iota(jnp.int32, (S5_SUPER_ST, S5_SUPER_CH), 0)
    colc = lax.broadcasted_iota(jnp.int32, (S5_SUPER_ST, S5_SUPER_CH), 1)
    ownc = (rowc // S5_STATE) == (colc // S5_GROUP)
    wc_ref[0, 0:S5_SUPER_ST, :] = jnp.where(ownc, cre_ref[...], 0.0).astype(BF16)
    wc_ref[0, S5_SUPER_ST:, :] = jnp.where(ownc, -cim_ref[...], 0.0).astype(BF16)
    a_re, a_im, _, _ = _s5_discretize(lref_ref[...], limf_ref[...], ldtf_ref[...])
    are_ref[...] = a_re
    aim_ref[...] = a_im


def _s5_prep(lam_re, lam_im, log_dt, b_re, b_im, c_re, c_im):
    g, p, ch, sg = S5_GROUPS, S5_STATE, S5_GROUP, S5_SUPER

    def rows_by_group(a_gp):
        return jnp.broadcast_to(a_gp[:, None, None, :], (g, ch, sg, p)).reshape(g * ch, sg * p)

    def b_layout(b):
        bt = jnp.transpose(b, (0, 2, 1))
        return jnp.broadcast_to(bt[:, :, None, :], (g, ch, sg, p)).reshape(g * ch, sg * p)

    def c_layout(cm):
        ct = jnp.transpose(cm.reshape(g // sg, sg, ch, p), (3, 0, 1, 2)).reshape(p, g * ch)
        return jnp.broadcast_to(ct[None], (sg, p, g * ch)).reshape(sg * p, g * ch)

    ldt_gp = jnp.broadcast_to(log_dt[:, None], (g, p))
    big = pl.BlockSpec((S5_SUPER_CH, S5_SUPER_ST), lambda j: (j, 0))
    cspec = pl.BlockSpec((S5_SUPER_ST, S5_SUPER_CH), lambda j: (0, j))
    flat = pl.BlockSpec((1, S5_SUPER_ST), lambda j: (0, j))
    return pl.pallas_call(
        _s5_prep_kernel,
        grid=(S5_NSUPER,),
        in_specs=[big] * 5 + [cspec] * 2 + [flat] * 3,
        out_specs=[
            pl.BlockSpec((1, S5_SUPER_CH, 2 * S5_SUPER_ST), lambda j: (j, 0, 0)),
            pl.BlockSpec((1, 2 * S5_SUPER_ST, S5_SUPER_CH), lambda j: (j, 0, 0)),
            flat, flat,
        ],
        out_shape=[
            jax.ShapeDtypeStruct((S5_NSUPER, S5_SUPER_CH, 2 * S5_SUPER_ST), BF16),
            jax.ShapeDtypeStruct((S5_NSUPER, 2 * S5_SUPER_ST, S5_SUPER_CH), BF16),
            jax.ShapeDtypeStruct((1, S5_FLAT), F32),
            jax.ShapeDtypeStruct((1, S5_FLAT), F32),
        ],
        compiler_params=pltpu.CompilerParams(dimension_semantics=("arbitrary",)),
        name="s5_prep",
    )(rows_by_group(lam_re), rows_by_group(lam_im), rows_by_group(ldt_gp),
      b_layout(b_re), b_layout(b_im), c_layout(c_re), c_layout(c_im),
      lam_re.reshape(1, S5_FLAT), lam_im.reshape(1, S5_FLAT), ldt_gp.reshape(1, S5_FLAT))


def _s5_kernel(x_ref, g_ref, wb_ref, wc_ref, are_ref, aim_ref, d_ref, wglu_ref, bglu_ref,
               h0re_ref, h0im_ref, perm_ref, unperm_ref, o_ref, hre_ref, him_ref,
               u_ref, sre_ref, sim_ref, z_ref, *, bsz, steps, sub):
    i = pl.program_id(0)
    rows = bsz * steps

    @pl.when(i == 0)
    def _():
        hre_ref[...] = h0re_ref[...]
        him_ref[...] = h0im_ref[...]

    def x_group(k):
        xk = x_ref[...] if sub == 1 else x_ref[:, k * steps:(k + 1) * steps, :]
        return xk.reshape(rows, D_MODEL)

    def region(k):
        r0 = (k % S5_REGIONS) * rows
        return slice(r0, r0 + rows)

    def input_side(k):
        grp = region(k)
        u = _dot(perm_ref[...], _rms(x_group(k), g_ref[0:1, :]).astype(BF16))
        u_ref[grp, :] = u
        ub = u.astype(BF16)
        for j in range(S5_NSUPER):
            bu = _dot(ub[:, j * S5_SUPER_CH:(j + 1) * S5_SUPER_CH], wb_ref[j])
            sre_ref[grp, j * S5_SUPER_ST:(j + 1) * S5_SUPER_ST] = bu[:, :S5_SUPER_ST]
            sim_ref[grp, j * S5_SUPER_ST:(j + 1) * S5_SUPER_ST] = bu[:, S5_SUPER_ST:]

    def scan_group(k):
        base = (k % S5_REGIONS) * rows
        for lc in range(S5_FLAT // S5_LANES):
            ls = slice(lc * S5_LANES, (lc + 1) * S5_LANES)
            a_re = jnp.broadcast_to(are_ref[:, ls], (SUBLANE, S5_LANES))
            a_im = jnp.broadcast_to(aim_ref[:, ls], (SUBLANE, S5_LANES))
            for rt in range(bsz // SUBLANE):
                rs = slice(rt * SUBLANE, (rt + 1) * SUBLANE)

                def step(t, carry, ls=ls, rt=rt, a_re=a_re, a_im=a_im):
                    h_re, h_im = carry
                    r8 = pl.ds(pl.multiple_of(base + t * bsz + rt * SUBLANE, SUBLANE), SUBLANE)
                    n_re = a_re * h_re - a_im * h_im + sre_ref[r8, ls]
                    n_im = a_re * h_im + a_im * h_re + sim_ref[r8, ls]
                    sre_ref[r8, ls] = n_re
                    sim_ref[r8, ls] = n_im
                    return n_re, n_im

                h_re, h_im = lax.fori_loop(0, steps, step, (hre_ref[rs, ls], him_ref[rs, ls]),
                                           unroll=True)
                hre_ref[rs, ls] = h_re
                him_ref[rs, ls] = h_im

    for k in range(min(S5_REGIONS, sub)):
        input_side(k)
    for k in range(sub):
        scan_group(k)
        grp = region(k)
        for j in range(S5_NSUPER):
            st = slice(j * S5_SUPER_ST, (j + 1) * S5_SUPER_ST)
            ch = slice(j * S5_SUPER_CH, (j + 1) * S5_SUPER_CH)
            y = (_dot(sre_ref[grp, st].astype(BF16), wc_ref[j, 0:S5_SUPER_ST, :])
                 + _dot(sim_ref[grp, st].astype(BF16), wc_ref[j, S5_SUPER_ST:, :]))
            z_ref[grp, ch] = (y + d_ref[:, ch] * u_ref[grp, ch]).astype(BF16)
        z = _dot(unperm_ref[...], z_ref[grp, :]).astype(BF16)
        zz = _dot(z, wglu_ref[...]) + bglu_ref[...]
        out = zz[:, :D_MODEL] * _sigmoid(zz[:, D_MODEL:])
        res = x_group(k) + _rms(out, g_ref[1:2, :])
        if sub == 1:
            o_ref[...] = res.reshape(o_ref.shape)
        else:
            o_ref[:, k * steps:(k + 1) * steps, :] = res.reshape(bsz, steps, D_MODEL)
        if k + S5_REGIONS < sub:
            input_side(k + S5_REGIONS)


def _s5(x, steps_per_block, g2, prep, d_skip, w_glu, b_glu, h0_re, h0_im):
    wb, wc, a_re, a_im = prep
    bsz, t, _ = x.shape
    rows = steps_per_block * bsz
    r = np.arange(rows)
    perm = np.zeros((rows, rows), np.float32)
    perm[(r % steps_per_block) * bsz + r // steps_per_block, r] = 1.0
    zero2 = lambda i: (0, 0)
    zero3 = lambda i: (0, 0, 0)
    resident = dict(pipeline_mode=pl.Buffered(1))
    state = pl.BlockSpec((bsz, S5_FLAT), zero2)
    if steps_per_block == t:
        sub = 1
        x = x.reshape(1, rows, D_MODEL)
        xspec = pl.BlockSpec((1, rows, D_MODEL), lambda i: (0, 0, 0))
    else:
        sub = S5_GROUPS_PER_STEP
        assert steps_per_block % SUBLANE == 0 and t % (sub * steps_per_block) == 0
        xspec = pl.BlockSpec((bsz, sub * steps_per_block, D_MODEL), lambda i: (0, i, 0))
    y, h_re, h_im = pl.pallas_call(
        functools.partial(_s5_kernel, bsz=bsz, steps=steps_per_block, sub=sub),
        grid=(t // (sub * steps_per_block),),
        in_specs=[
            xspec,
            pl.BlockSpec((2, D_MODEL), zero2),
            pl.BlockSpec(wb.shape, zero3, **resident),
            pl.BlockSpec(wc.shape, zero3, **resident),
            pl.BlockSpec((1, S5_FLAT), zero2),
            pl.BlockSpec((1, S5_FLAT), zero2),
            pl.BlockSpec((1, D_MODEL), zero2),
            pl.BlockSpec((D_MODEL, 2 * D_MODEL), zero2, **resident),
            pl.BlockSpec((1, 2 * D_MODEL), zero2),
            state, state,
            pl.BlockSpec((rows, rows), zero2),
            pl.BlockSpec((rows, rows), zero2),
        ],
        out_specs=[xspec, state, state],
        out_shape=[
            jax.ShapeDtypeStruct(x.shape, F32),
            jax.ShapeDtypeStruct((bsz, S5_FLAT), F32),
            jax.ShapeDtypeStruct((bsz, S5_FLAT), F32),
        ],
        scratch_shapes=[
            pltpu.VMEM((min(sub, S5_REGIONS) * rows, D_MODEL), F32),
            pltpu.VMEM((min(sub, S5_REGIONS) * rows, S5_FLAT), F32),
            pltpu.VMEM((min(sub, S5_REGIONS) * rows, S5_FLAT), F32),
            pltpu.VMEM((min(sub, S5_REGIONS) * rows, D_MODEL), BF16),
        ],
        compiler_params=pltpu.CompilerParams(
            dimension_semantics=("arbitrary",), vmem_limit_bytes=VMEM_LIMIT),
        name="s5",
    )(x, g2, wb, wc, a_re, a_im, d_skip.reshape(1, D_MODEL), w_glu.astype(BF16),
      b_glu.reshape(1, 2 * D_MODEL), h0_re, h0_im,
      jnp.asarray(perm, BF16), jnp.asarray(perm.T, BF16))
    return y.reshape(bsz, t, D_MODEL), h_re, h_im


def kernel(x_prompt, x_sample, state_gla, state_s5_re, state_s5_im, norm_g, w_ffn_gu, w_ffn_down,
           gla_w_in, gla_w_g2, gla_b_g, gla_g_onorm, gla_w_out,
           s5_lam_re, s5_lam_im, s5_log_dt, s5_b_re, s5_b_im, s5_c_re, s5_c_im, s5_d, s5_w_glu,
           s5_b_glu):
    pb, pt, _ = x_prompt.shape
    sb, st, _ = x_sample.shape

    def ffn(xp, xs, layer, which):
        g2 = norm_g[layer, 4 * which:4 * which + 2]
        xs, wgu_bf, wd_bf = _ffn_stream(xs, g2, w_ffn_gu, w_ffn_down, layer, which)
        return _ffn_resident(xp, g2, wgu_bf, wd_bf), xs

    gla_w = _gla_weights(gla_w_in[0], gla_w_g2[0], gla_b_g[0], gla_g_onorm[0], gla_w_out[0])
    xp, xs = ffn(x_prompt, x_sample, 0, 0)
    xp, gla_p = _gla_prompt(xp, norm_g[0, 2:4], gla_w)
    xs, gla_s = _gla_sample(xs, state_gla.reshape(sb, GLA_HEADS, GLA_DK, GLA_DV), norm_g[0, 2:4], gla_w)
    xp, xs = ffn(xp, xs, 0, 1)

    xp, xs = ffn(xp, xs, 1, 0)
    prep = _s5_prep(s5_lam_re[0], s5_lam_im[0], s5_log_dt[0], s5_b_re[0], s5_b_im[0],
                    s5_c_re[0], s5_c_im[0])
    zeros = jnp.zeros((pb, S5_FLAT), F32)
    s5_args = (norm_g[1, 2:4], prep, s5_d[0], s5_w_glu[0], s5_b_glu[0])
    xp, hre_p, him_p = _s5(xp, S5_STEPS, *s5_args, zeros, zeros)
    xs, hre_s, him_s = _s5(xs, st, *s5_args,
                           state_s5_re.reshape(sb, S5_FLAT), state_s5_im.reshape(sb, S5_FLAT))
    y_prompt, y_sample = ffn(xp, xs, 1, 1)

    def s5_state(h, b):
        return h.reshape(1, b, S5_GROUPS, S5_STATE)

    return (y_prompt, y_sample, gla_p.reshape(1, pb, GLA_HEADS, GLA_DK, GLA_DV),
            s5_state(hre_p, pb), s5_state(him_p, pb),
            gla_s.reshape(1, sb, GLA_HEADS, GLA_DK, GLA_DV),
            s5_state(hre_s, sb), s5_state(him_s, sb))
```

```python
import functools
import math

import jax
import jax.numpy as jnp
import numpy as np
from jax import lax
from jax.experimental import pallas as pl
from jax.experimental.pallas import tpu as pltpu

F32 = jnp.float32
BF16 = jnp.bfloat16

D_MODEL = 1024
D_FF = 2816
GLA_HEADS = 4
GLA_DK = 128
GLA_DV = 256
GLA_KEY_DIM = GLA_HEADS * GLA_DK
GLA_VAL_DIM = GLA_HEADS * GLA_DV
GATE_RANK = 16
GLA_MAIN_DIM = 2 * GLA_KEY_DIM + 2 * GLA_VAL_DIM
GLA_IN_DIM = GLA_MAIN_DIM + GATE_RANK
GATE_TAU = 16.0
S5_GROUP = 16
S5_GROUPS = 64
S5_STATE = 64
S5_FLAT = S5_GROUPS * S5_STATE
S5_SUPER = 8
S5_NSUPER = S5_GROUPS // S5_SUPER
S5_SUPER_CH = S5_SUPER * S5_GROUP
S5_SUPER_ST = S5_SUPER * S5_STATE
EPS = 1e-6
LANE = 128
SUBLANE = 8
VMEM_LIMIT = 56 * 1024 * 1024

FFN_ROWS = 1024
FFN_GROUP = 512
FFN_CHUNK = 256
FFN_SLOTS = 4
GLA_CHUNK = 64
GLA_TBLOCK = 512
GLA_PROJ_CHUNK = 256
GLA_CUM_ROWS = 256
LOG2E = math.log2(math.e)
GLA_SAMPLE_SEQS = 16
S5_LANES = 512
S5_STEPS = 32
S5_GROUPS_PER_STEP = 4
S5_REGIONS = 2


def _dot(a, b):
    return jnp.dot(a, b, preferred_element_type=F32)


def _dot_nt(a, b):
    return lax.dot_general(a, b, (((1,), (1,)), ((), ())), preferred_element_type=F32)


def _rms(x, g):
    ms = jnp.mean(x * x, axis=-1, keepdims=True)
    return x * lax.rsqrt(ms + EPS) * g


def _sigmoid(x):
    return 1.0 / (1.0 + jnp.exp(-x))


def _split_bf16(x):
    hi = x.astype(BF16)
    lo = (x - hi.astype(F32)).astype(BF16)
    return hi, lo


def _ffn_row_groups(n_rows):
    return [slice(r0, min(r0 + FFN_GROUP, n_rows)) for r0 in range(0, n_rows, FFN_GROUP)]


def _ffn_hidden_chunk(xn_ref, wgu_ref, act_ref, rows, c):
    lo = c * FFN_CHUNK
    gate = _dot(xn_ref[rows, :], wgu_ref[:, lo:lo + FFN_CHUNK])
    up = _dot(xn_ref[rows, :], wgu_ref[:, D_FF + lo:D_FF + lo + FFN_CHUNK])
    act_ref[rows, lo:lo + FFN_CHUNK] = (gate * _sigmoid(gate) * up).astype(BF16)


def _ffn_finish(x_ref, g_ref, wd_ref, o_ref, act_ref, rows):
    y = _dot(act_ref[rows, :], wd_ref[...])
    o_ref[rows, :] = x_ref[rows, :] + 0.5 * _rms(y, g_ref[1:2, :])


def _ffn_stream_kernel(x_ref, g_ref, wgu_hbm, wd_hbm, o_ref, wgu_ref, wd_ref,
                       xn_ref, act_ref, gu_stage, d_stage, sem, *, layer, which):
    n_chunks = D_FF // FFN_CHUNK
    groups = _ffn_row_groups(x_ref.shape[0])
    xn_ref[...] = _rms(x_ref[...], g_ref[0:1, :]).astype(BF16)

    def chunk_slices(c):
        lo = c * FFN_CHUNK
        return ((slice(None), pl.ds(lo, FFN_CHUNK)), (slice(None), pl.ds(D_FF + lo, FFN_CHUNK)),
                (pl.ds(lo, FFN_CHUNK), slice(None)))

    def fetches(c):
        slot = c % FFN_SLOTS
        gate, up, down = chunk_slices(c)
        return (
            pltpu.make_async_copy(wgu_hbm.at[(layer, which) + gate], gu_stage.at[slot, 0], sem.at[slot, 0]),
            pltpu.make_async_copy(wgu_hbm.at[(layer, which) + up], gu_stage.at[slot, 1], sem.at[slot, 1]),
            pltpu.make_async_copy(wd_hbm.at[(layer, which) + down], d_stage.at[slot], sem.at[slot, 2]),
        )

    ahead = FFN_SLOTS - 1
    for c in range(min(ahead, n_chunks)):
        for cp in fetches(c):
            cp.start()
    for c in range(n_chunks):
        slot, lo = c % FFN_SLOTS, c * FFN_CHUNK
        for cp in fetches(c):
            cp.wait()
        if c + ahead < n_chunks:
            for cp in fetches(c + ahead):
                cp.start()
        wgu_ref[:, lo:lo + FFN_CHUNK] = gu_stage[slot, 0].astype(BF16)
        wgu_ref[:, D_FF + lo:D_FF + lo + FFN_CHUNK] = gu_stage[slot, 1].astype(BF16)
        wd_ref[lo:lo + FFN_CHUNK, :] = d_stage[slot].astype(BF16)
        for rows in groups:
            _ffn_hidden_chunk(xn_ref, wgu_ref, act_ref, rows, c)
    for rows in groups:
        _ffn_finish(x_ref, g_ref, wd_ref, o_ref, act_ref, rows)


def _ffn_resident_kernel(x_ref, g_ref, wgu_ref, wd_ref, o_ref, xn_ref, act_ref):
    xn_ref[...] = _rms(x_ref[...], g_ref[0:1, :]).astype(BF16)
    for rows in _ffn_row_groups(x_ref.shape[0]):
        for c in range(D_FF // FFN_CHUNK):
            _ffn_hidden_chunk(xn_ref, wgu_ref, act_ref, rows, c)
        _ffn_finish(x_ref, g_ref, wd_ref, o_ref, act_ref, rows)


def _ffn_stream(x, g2, wgu, wd, layer, which):
    shape = x.shape
    x = x.reshape(-1, D_MODEL)
    n = x.shape[0]
    zero = lambda i: (0, 0)
    y, wgu_bf, wd_bf = pl.pallas_call(
        functools.partial(_ffn_stream_kernel, layer=layer, which=which),
        grid=(1,),
        in_specs=[
            pl.BlockSpec((n, D_MODEL), zero),
            pl.BlockSpec((2, D_MODEL), zero),
            pl.BlockSpec(memory_space=pl.ANY),
            pl.BlockSpec(memory_space=pl.ANY),
        ],
        out_specs=[
            pl.BlockSpec((n, D_MODEL), zero),
            pl.BlockSpec((D_MODEL, 2 * D_FF), zero, pipeline_mode=pl.Buffered(1)),
            pl.BlockSpec((D_FF, D_MODEL), zero, pipeline_mode=pl.Buffered(1)),
        ],
        out_shape=[
            jax.ShapeDtypeStruct((n, D_MODEL), F32),
            jax.ShapeDtypeStruct((D_MODEL, 2 * D_FF), BF16),
            jax.ShapeDtypeStruct((D_FF, D_MODEL), BF16),
        ],
        scratch_shapes=[
            pltpu.VMEM((n, D_MODEL), BF16),
            pltpu.VMEM((n, D_FF), BF16),
            pltpu.VMEM((FFN_SLOTS, 2, D_MODEL, FFN_CHUNK), F32),
            pltpu.VMEM((FFN_SLOTS, FFN_CHUNK, D_MODEL), F32),
            pltpu.SemaphoreType.DMA((FFN_SLOTS, 3)),
        ],
        compiler_params=pltpu.CompilerParams(
            dimension_semantics=("arbitrary",), vmem_limit_bytes=VMEM_LIMIT),
        name="ffn_stream",
    )(x, g2, wgu, wd)
    return y.reshape(shape), wgu_bf, wd_bf


def _ffn_resident(x, g2, wgu_bf, wd_bf):
    shape = x.shape
    x = x.reshape(-1, D_MODEL)
    n = x.shape[0]
    tm = min(FFN_ROWS, n)
    assert n % tm == 0
    zero = lambda i: (0, 0)
    resident = dict(pipeline_mode=pl.Buffered(1))
    return pl.pallas_call(
        _ffn_resident_kernel,
        grid=(n // tm,),
        in_specs=[
            pl.BlockSpec((tm, D_MODEL), lambda i: (i, 0)),
            pl.BlockSpec((2, D_MODEL), zero),
            pl.BlockSpec((D_MODEL, 2 * D_FF), zero, **resident),
            pl.BlockSpec((D_FF, D_MODEL), zero, **resident),
        ],
        out_specs=pl.BlockSpec((tm, D_MODEL), lambda i: (i, 0)),
        out_shape=jax.ShapeDtypeStruct((n, D_MODEL), F32),
        scratch_shapes=[pltpu.VMEM((tm, D_MODEL), BF16), pltpu.VMEM((tm, D_FF), BF16)],
        compiler_params=pltpu.CompilerParams(
            dimension_semantics=("arbitrary",), vmem_limit_bytes=VMEM_LIMIT),
        name="ffn",
    )(x, g2, wgu_bf, wd_bf).reshape(shape)


def _gla_tables(rows, seq):
    t = np.arange(rows)[:, None]
    r = np.arange(rows)[None, :]
    cum = ((t // seq) == (r // seq)) & (r <= t)
    masks = [t == r]
    m = seq // 2
    while m >= 1:
        masks.append(((t // (2 * m)) == (r // (2 * m))) & (t % (2 * m) >= m) & (r % (2 * m) < m))
        m //= 2
    return jnp.asarray(cum.astype(np.float32), BF16), jnp.asarray(np.stack(masks).astype(np.float32))


def _gla_project_cols(hn_ref, win_ref, proj_ref, c0):
    proj_ref[:, c0:c0 + GLA_PROJ_CHUNK] = _dot(hn_ref[...], win_ref[:, c0:c0 + GLA_PROJ_CHUNK])


def _gla_project(x, g_ref, win_ref, wglr_ref, wg2_ref, bg_ref, cum_ref, hn_ref, proj_ref, lf_ref, b_ref):
    hn = _rms(x, g_ref[0:1, :]).astype(BF16)
    hn_ref[...] = hn
    glr = _dot(hn, wglr_ref[...]).astype(BF16)
    gate = _dot(glr, wg2_ref[...]) + bg_ref[...]
    for c0 in range(0, 2 * GLA_KEY_DIM, GLA_PROJ_CHUNK):
        _gla_project_cols(hn_ref, win_ref, proj_ref, c0)
    lf = (jnp.minimum(gate, 0.0) - jnp.log1p(jnp.exp(-jnp.abs(gate)))) * (1.0 / GATE_TAU)
    lf_ref[...] = lf
    hi, lo = _split_bf16(lf)
    span = cum_ref.shape[0]
    for r0 in range(0, lf.shape[0], span):
        b_ref[r0:r0 + span, :] = (_dot(cum_ref[...], hi[r0:r0 + span]) + _dot(cum_ref[...], lo[r0:r0 + span]))


def _block_rows(ref, r0, col, rows, period, offset):
    def bc(row, n):
        tile = ref[pl.ds(r0 + row // SUBLANE * SUBLANE, SUBLANE), col]
        return jnp.broadcast_to(tile[row % SUBLANE:row % SUBLANE + 1, :], (n, GLA_DK))

    if period >= SUBLANE:
        parts = [bc(p0 + offset, period) for p0 in range(0, rows, period)]
    else:
        assert 2 * period == SUBLANE
        low = lax.broadcasted_iota(jnp.int32, (SUBLANE, 1), 0) < period
        parts = [jnp.where(low, bc(t0 + offset, SUBLANE), bc(t0 + period + offset, SUBLANE))
                 for t0 in range(0, rows, SUBLANE)]
    return parts[0] if len(parts) == 1 else jnp.concatenate(parts, axis=0)


def _gla_attention(qs, k, b, lf, b_ref, r0, col, rows, seq, mask_ref):
    qb, kb = qs.astype(BF16), k.astype(BF16)
    att = _dot_nt(qb, kb) * mask_ref[0]
    lvl, m = 1, seq // 2
    while m >= 1:
        if m > 1:
            d = b - _block_rows(b_ref, r0, col, rows, 2 * m, m - 1)
            w = jnp.exp2(jnp.abs(d) * (-LOG2E))
        else:
            odd = lax.broadcasted_iota(jnp.int32, (rows, 1), 0) % 2 == 1
            w = jnp.where(odd, jnp.exp(lf), 1.0)
        wb = w.astype(BF16)
        att = att + _dot_nt(qb * wb, kb * wb) * mask_ref[lvl]
        lvl, m = lvl + 1, m // 2
    return att


def _gla_finish_head(o, r, gon_ref):
    on = _rms(o, gon_ref[...])
    return (on * (r * _sigmoid(r))).astype(BF16)


def _gla_prompt_kernel(x_ref, g_ref, win_ref, wglr_ref, wg2_ref, bg_ref, gon_ref, wout_ref,
                       cum_ref, mask_ref, o_ref, snew_ref,
                       hn_ref, proj_ref, lf_ref, b_ref, og_ref, s_ref, att_ref, qb_ref, tt_ref):
    tb = pl.program_id(1)
    c = GLA_CHUNK

    @pl.when(tb == 0)
    def _():
        s_ref[...] = jnp.zeros_like(s_ref)

    x = x_ref[0]
    _gla_project(x, g_ref, win_ref, wglr_ref, wg2_ref, bg_ref, cum_ref, hn_ref, proj_ref, lf_ref, b_ref)

    late_cols = list(range(2 * GLA_KEY_DIM, proj_ref.shape[1], GLA_PROJ_CHUNK))
    n_heads_total = GLA_TBLOCK // c * GLA_HEADS
    for ci, r0 in enumerate(range(0, GLA_TBLOCK, c)):
        rows = slice(r0, r0 + c)
        for h in range(GLA_HEADS):
            done = ci * GLA_HEADS + h
            for c0 in late_cols[done * len(late_cols) // n_heads_total:
                                (done + 1) * len(late_cols) // n_heads_total]:
                _gla_project_cols(hn_ref, win_ref, proj_ref, c0)
            kcol = slice(h * GLA_DK, (h + 1) * GLA_DK)
            qs = proj_ref[rows, kcol] * (GLA_DK ** -0.5)
            k = proj_ref[rows, GLA_KEY_DIM + h * GLA_DK:GLA_KEY_DIM + (h + 1) * GLA_DK]
            b = b_ref[rows, kcol]
            att = _gla_attention(qs, k, b, lf_ref[rows, kcol], b_ref, r0, kcol, c, c, mask_ref)
            att_ref[ci * GLA_HEADS + h] = att.astype(BF16)
            qb_ref[rows, kcol] = (qs * jnp.exp(b)).astype(BF16)
            khat = k * jnp.exp(_block_rows(b_ref, r0, kcol, c, c, c - 1) - b)
            tile = jnp.concatenate(
                [khat, jnp.exp(b[c - SUBLANE:c, :]), jnp.zeros((LANE - c - SUBLANE, GLA_DK), F32)], axis=0)
            tt_ref[ci * GLA_HEADS + h] = jnp.transpose(tile)

    for ci, r0 in enumerate(range(0, GLA_TBLOCK, c)):
        rows = slice(r0, r0 + c)
        for h in range(GLA_HEADS):
            kcol = slice(h * GLA_DK, (h + 1) * GLA_DK)
            vcol = slice(h * GLA_DV, (h + 1) * GLA_DV)
            v = proj_ref[rows, 2 * GLA_KEY_DIM + h * GLA_DV:2 * GLA_KEY_DIM + (h + 1) * GLA_DV].astype(BF16)
            r = proj_ref[rows, 2 * GLA_KEY_DIM + GLA_VAL_DIM + h * GLA_DV:
                         2 * GLA_KEY_DIM + GLA_VAL_DIM + (h + 1) * GLA_DV]
            s = s_ref[h]
            tt = tt_ref[ci * GLA_HEADS + h]
            o = _dot(att_ref[ci * GLA_HEADS + h], v) + _dot(qb_ref[rows, kcol], s.astype(BF16))
            decay = tt[:, c + SUBLANE - 1:c + SUBLANE]
            s_ref[h] = s * decay + _dot(tt[:, 0:c].astype(BF16), v)
            og_ref[rows, vcol] = _gla_finish_head(o, r, gon_ref)
    out = _dot(og_ref[...], wout_ref[...])
    o_ref[0] = x + _rms(out, g_ref[1:2, :])

    @pl.when(tb == pl.num_programs(1) - 1)
    def _():
        snew_ref[0] = s_ref[...]


def _gla_sample_kernel(x_ref, g_ref, win_ref, wglr_ref, wg2_ref, bg_ref, gon_ref, wout_ref,
                       cum_ref, mask_ref, s0_ref, o_ref, snew_ref,
                       hn_ref, proj_ref, lf_ref, b_ref, og_ref, *, seq):
    rows = x_ref.shape[0]
    grp = 2 * SUBLANE
    x = x_ref[...]
    _gla_project(x, g_ref, win_ref, wglr_ref, wg2_ref, bg_ref, cum_ref, hn_ref, proj_ref, lf_ref, b_ref)
    for c0 in range(2 * GLA_KEY_DIM, proj_ref.shape[1], GLA_PROJ_CHUNK):
        _gla_project_cols(hn_ref, win_ref, proj_ref, c0)
    rid = lax.broadcasted_iota(jnp.int32, (grp, 1), 0)
    for h in range(GLA_HEADS):
        kcol = slice(h * GLA_DK, (h + 1) * GLA_DK)
        vcol = slice(h * GLA_DV, (h + 1) * GLA_DV)
        qs = proj_ref[:, kcol] * (GLA_DK ** -0.5)
        k = proj_ref[:, GLA_KEY_DIM + h * GLA_DK:GLA_KEY_DIM + (h + 1) * GLA_DK]
        v = proj_ref[:, 2 * GLA_KEY_DIM + h * GLA_DV:2 * GLA_KEY_DIM + (h + 1) * GLA_DV]
        r = proj_ref[:, 2 * GLA_KEY_DIM + GLA_VAL_DIM + h * GLA_DV:
                     2 * GLA_KEY_DIM + GLA_VAL_DIM + (h + 1) * GLA_DV]
        b = b_ref[:, kcol]
        att = _gla_attention(qs, k, b, lf_ref[:, kcol], b_ref, 0, kcol, rows, seq, mask_ref)
        eb = jnp.exp(b)
        qb = (qs * eb).astype(BF16)
        khat = k * jnp.exp(_block_rows(b_ref, 0, kcol, rows, seq, seq - 1) - b)
        o_intra = _dot(att.astype(BF16), v.astype(BF16))
        o_parts = []
        for gi in range(rows // grp):
            gr = slice(gi * grp, (gi + 1) * grp)
            tile = jnp.concatenate(
                [khat[gr], eb[gr], jnp.zeros((LANE - 2 * grp, GLA_DK), F32)], axis=0)
            tt = jnp.transpose(tile)
            kt = tt[:, 0:grp].astype(BF16)
            qg = qb[gr]
            vg = v[gr]
            o_g = jnp.zeros((grp, GLA_DV), F32)
            for si in range(grp // seq):
                sq = gi * (grp // seq) + si
                mine = (rid >= si * seq) & (rid < (si + 1) * seq)
                s = s0_ref[sq, h]
                o_g = o_g + jnp.where(mine, _dot(qg, s.astype(BF16)), 0.0)
                vm = jnp.where(mine, vg, 0.0).astype(BF16)
                last = grp + (si + 1) * seq - 1
                snew_ref[sq, h] = s * tt[:, last:last + 1] + _dot(kt, vm)
            o_parts.append(o_g)
        o = o_intra + jnp.concatenate(o_parts, axis=0)
        og_ref[:, vcol] = _gla_finish_head(o, r, gon_ref)
    out = _dot(og_ref[...], wout_ref[...])
    o_ref[...] = x + _rms(out, g_ref[1:2, :])


def _gla_weight_specs():
    zero = (lambda *_: (0, 0))
    resident = dict(pipeline_mode=pl.Buffered(1))
    return [
        pl.BlockSpec((2, D_MODEL), zero),
        pl.BlockSpec((D_MODEL, GLA_IN_DIM), zero, **resident),
        pl.BlockSpec((D_MODEL, LANE), zero, **resident),
        pl.BlockSpec((LANE, GLA_KEY_DIM), zero, **resident),
        pl.BlockSpec((1, GLA_KEY_DIM), zero),
        pl.BlockSpec((1, GLA_DV), zero),
        pl.BlockSpec((GLA_VAL_DIM, D_MODEL), zero, **resident),
    ]


def _gla_weights(w_in, w_g2, b_g, g_onorm, w_out):
    w_glr = jnp.pad(w_in[:, GLA_MAIN_DIM:], ((0, 0), (0, LANE - GATE_RANK))).astype(BF16)
    w_g2p = jnp.pad(w_g2, ((0, LANE - GATE_RANK), (0, 0))).astype(BF16)
    return (w_in.astype(BF16), w_glr, w_g2p, b_g.reshape(1, GLA_KEY_DIM),
            g_onorm.reshape(1, GLA_DV), w_out.astype(BF16))


def _gla_prompt(x, g2, weights):
    bsz, t, _ = x.shape
    cum, _ = _gla_tables(GLA_CUM_ROWS, GLA_CHUNK)
    _, masks = _gla_tables(GLA_CHUNK, GLA_CHUNK)
    n_main = GLA_MAIN_DIM
    const2 = lambda b, i: (0, 0)
    return pl.pallas_call(
        _gla_prompt_kernel,
        grid=(bsz, t // GLA_TBLOCK),
        in_specs=[pl.BlockSpec((1, GLA_TBLOCK, D_MODEL), lambda b, i: (b, i, 0))]
        + _gla_weight_specs()
        + [pl.BlockSpec(cum.shape, const2),
           pl.BlockSpec(masks.shape, lambda b, i: (0, 0, 0))],
        out_specs=[
            pl.BlockSpec((1, GLA_TBLOCK, D_MODEL), lambda b, i: (b, i, 0)),
            pl.BlockSpec((1, GLA_HEADS, GLA_DK, GLA_DV), lambda b, i: (b, 0, 0, 0)),
        ],
        out_shape=[
            jax.ShapeDtypeStruct(x.shape, F32),
            jax.ShapeDtypeStruct((bsz, GLA_HEADS, GLA_DK, GLA_DV), F32),
        ],
        scratch_shapes=[
            pltpu.VMEM((GLA_TBLOCK, D_MODEL), BF16),
            pltpu.VMEM((GLA_TBLOCK, n_main), F32),
            pltpu.VMEM((GLA_TBLOCK, GLA_KEY_DIM), F32),
            pltpu.VMEM((GLA_TBLOCK, GLA_KEY_DIM), F32),
            pltpu.VMEM((GLA_TBLOCK, GLA_VAL_DIM), BF16),
            pltpu.VMEM((GLA_HEADS, GLA_DK, GLA_DV), F32),
            pltpu.VMEM((GLA_TBLOCK // GLA_CHUNK * GLA_HEADS, GLA_CHUNK, GLA_CHUNK), BF16),
            pltpu.VMEM((GLA_TBLOCK, GLA_KEY_DIM), BF16),
            pltpu.VMEM((GLA_TBLOCK // GLA_CHUNK * GLA_HEADS, LANE, GLA_DK), F32),
        ],
        compiler_params=pltpu.CompilerParams(
            dimension_semantics=("arbitrary", "arbitrary"), vmem_limit_bytes=VMEM_LIMIT),
        name="gla_prompt",
    )(x, g2, *weights, cum, masks)


def _gla_sample(x, s0, g2, weights):
    bsz, seq, _ = x.shape
    rows = GLA_SAMPLE_SEQS * seq
    cum, masks = _gla_tables(rows, seq)
    n_main = GLA_MAIN_DIM
    state_spec = pl.BlockSpec((GLA_SAMPLE_SEQS, GLA_HEADS, GLA_DK, GLA_DV), lambda i: (i, 0, 0, 0))
    y, snew = pl.pallas_call(
        functools.partial(_gla_sample_kernel, seq=seq),
        grid=(bsz // GLA_SAMPLE_SEQS,),
        in_specs=[pl.BlockSpec((rows, D_MODEL), lambda i: (i, 0))]
        + _gla_weight_specs()
        + [pl.BlockSpec(cum.shape, lambda i: (0, 0)),
           pl.BlockSpec(masks.shape, lambda i: (0, 0, 0)),
           state_spec],
        out_specs=[pl.BlockSpec((rows, D_MODEL), lambda i: (i, 0)), state_spec],
        out_shape=[
            jax.ShapeDtypeStruct((bsz * seq, D_MODEL), F32),
            jax.ShapeDtypeStruct(s0.shape, F32),
        ],
        scratch_shapes=[
            pltpu.VMEM((rows, D_MODEL), BF16),
            pltpu.VMEM((rows, n_main), F32),
            pltpu.VMEM((rows, GLA_KEY_DIM), F32),
            pltpu.VMEM((rows, GLA_KEY_DIM), F32),
            pltpu.VMEM((rows, GLA_VAL_DIM), BF16),
        ],
        compiler_params=pltpu.CompilerParams(
            dimension_semantics=("arbitrary",), vmem_limit_bytes=VMEM_LIMIT),
        name="gla_sample",
    )(x.reshape(bsz * seq, D_MODEL), g2, *weights, cum, masks, s0)
    return y.reshape(x.shape), snew


def _s5_discretize(lam_re, lam_im, log_dt):
    dt = jnp.exp(log_dt)
    mag = jnp.exp(lam_re * dt)
    ang = lam_im * dt
    a_re, a_im = mag * jnp.cos(ang), mag * jnp.sin(ang)
    nr, ni = a_re - 1.0, a_im
    den = lam_re * lam_re + lam_im * lam_im
    f_re = (nr * lam_re + ni * lam_im) / den
    f_im = (ni * lam_re - nr * lam_im) / den
    return a_re, a_im, f_re, f_im


def _s5_prep_kernel(lre_ref, lim_ref, ldt_ref, bre_ref, bim_ref, cre_ref, cim_ref,
                    lref_ref, limf_ref, ldtf_ref, wb_ref, wc_ref, are_ref, aim_ref):
    _, _, f_re, f_im = _s5_discretize(lre_ref[...], lim_ref[...], ldt_ref[...])
    b_re, b_im = bre_ref[...], bim_ref[...]
    row = lax.broadcasted_iota(jnp.int32, (S5_SUPER_CH, S5_SUPER_ST), 0)
    col = lax.broadcasted_iota(jnp.int32, (S5_SUPER_CH, S5_SUPER_ST), 1)
    own = (row // S5_GROUP) == (col // S5_STATE)
    wb_ref[0, :, 0:S5_SUPER_ST] = jnp.where(own, f_re * b_re - f_im * b_im, 0.0).astype(BF16)
    wb_ref[0, :, S5_SUPER_ST:] = jnp.where(own, f_re * b_im + f_im * b_re, 0.0).astype(BF16)
    rowc = lax.broadcasted_iota(jnp.int32, (S5_SUPER_ST, S5_SUPER_CH), 0)
    colc = lax.broadcasted_iota(jnp.int32, (S5_SUPER_ST, S5_SUPER_CH), 1)
    ownc = (rowc // S5_STATE) == (colc // S5_GROUP)
    wc_ref[0, 0:S5_SUPER_ST, :] = jnp.where(ownc, cre_ref[...], 0.0).astype(BF16)
    wc_ref[0, S5_SUPER_ST:, :] = jnp.where(ownc, -cim_ref[...], 0.0).astype(BF16)
    a_re, a_im, _, _ = _s5_discretize(lref_ref[...], limf_ref[...], ldtf_ref[...])
    are_ref[...] = a_re
    aim_ref[...] = a_im


def _s5_prep(lam_re, lam_im, log_dt, b_re, b_im, c_re, c_im):
    g, p, ch, sg = S5_GROUPS, S5_STATE, S5_GROUP, S5_SUPER

    def rows_by_group(a_gp):
        return jnp.broadcast_to(a_gp[:, None, None, :], (g, ch, sg, p)).reshape(g * ch, sg * p)

    def b_layout(b):
        bt = jnp.transpose(b, (0, 2, 1))
        return jnp.broadcast_to(bt[:, :, None, :], (g, ch, sg, p)).reshape(g * ch, sg * p)

    def c_layout(cm):
        ct = jnp.transpose(cm.reshape(g // sg, sg, ch, p), (3, 0, 1, 2)).reshape(p, g * ch)
        return jnp.broadcast_to(ct[None], (sg, p, g * ch)).reshape(sg * p, g * ch)

    ldt_gp = jnp.broadcast_to(log_dt[:, None], (g, p))
    big = pl.BlockSpec((S5_SUPER_CH, S5_SUPER_ST), lambda j: (j, 0))
    cspec = pl.BlockSpec((S5_SUPER_ST, S5_SUPER_CH), lambda j: (0, j))
    flat = pl.BlockSpec((1, S5_SUPER_ST), lambda j: (0, j))
    return pl.pallas_call(
        _s5_prep_kernel,
        grid=(S5_NSUPER,),
        in_specs=[big] * 5 + [cspec] * 2 + [flat] * 3,
        out_specs=[
            pl.BlockSpec((1, S5_SUPER_CH, 2 * S5_SUPER_ST), lambda j: (j, 0, 0)),
            pl.BlockSpec((1, 2 * S5_SUPER_ST, S5_SUPER_CH), lambda j: (j, 0, 0)),
            flat, flat,
        ],
        out_shape=[
            jax.ShapeDtypeStruct((S5_NSUPER, S5_SUPER_CH, 2 * S5_SUPER_ST), BF16),
            jax.ShapeDtypeStruct((S5_NSUPER, 2 * S5_SUPER_ST, S5_SUPER_CH), BF16),
            jax.ShapeDtypeStruct((1, S5_FLAT), F32),
            jax.ShapeDtypeStruct((1, S5_FLAT), F32),
        ],
        compiler_params=pltpu.CompilerParams(dimension_semantics=("arbitrary",)),
        name="s5_prep",
    )(rows_by_group(lam_re), rows_by_group(lam_im), rows_by_group(ldt_gp),
      b_layout(b_re), b_layout(b_im), c_layout(c_re), c_layout(c_im),
      lam_re.reshape(1, S5_FLAT), lam_im.reshape(1, S5_FLAT), ldt_gp.reshape(1, S5_FLAT))


def _s5_kernel(x_ref, g_ref, wb_ref, wc_ref, are_ref, aim_ref, d_ref, wglu_ref, bglu_ref,
               h0re_ref, h0im_ref, perm_ref, unperm_ref, o_ref, hre_ref, him_ref,
               u_ref, sre_ref, sim_ref, z_ref, *, bsz, steps, sub):
    i = pl.program_id(0)
    rows = bsz * steps

    @pl.when(i == 0)
    def _():
        hre_ref[...] = h0re_ref[...]
        him_ref[...] = h0im_ref[...]

    def x_group(k):
        xk = x_ref[...] if sub == 1 else x_ref[:, k * steps:(k + 1) * steps, :]
        return xk.reshape(rows, D_MODEL)

    def region(k):
        r0 = (k % S5_REGIONS) * rows
        return slice(r0, r0 + rows)

    def input_side(k):
        grp = region(k)
        u = _dot(perm_ref[...], _rms(x_group(k), g_ref[0:1, :]).astype(BF16))
        u_ref[grp, :] = u
        ub = u.astype(BF16)
        for j in range(S5_NSUPER):
            bu = _dot(ub[:, j * S5_SUPER_CH:(j + 1) * S5_SUPER_CH], wb_ref[j])
            sre_ref[grp, j * S5_SUPER_ST:(j + 1) * S5_SUPER_ST] = bu[:, :S5_SUPER_ST]
            sim_ref[grp, j * S5_SUPER_ST:(j + 1) * S5_SUPER_ST] = bu[:, S5_SUPER_ST:]

    def scan_group(k):
        base = (k % S5_REGIONS) * rows
        for lc in range(S5_FLAT // S5_LANES):
            ls = slice(lc * S5_LANES, (lc + 1) * S5_LANES)
            a_re = jnp.broadcast_to(are_ref[:, ls], (SUBLANE, S5_LANES))
            a_im = jnp.broadcast_to(aim_ref[:, ls], (SUBLANE, S5_LANES))
            for rt in range(bsz // SUBLANE):
                rs = slice(rt * SUBLANE, (rt + 1) * SUBLANE)

                def step(t, carry, ls=ls, rt=rt, a_re=a_re, a_im=a_im):
                    h_re, h_im = carry
                    r8 = pl.ds(pl.multiple_of(base + t * bsz + rt * SUBLANE, SUBLANE), SUBLANE)
                    n_re = a_re * h_re - a_im * h_im + sre_ref[r8, ls]
                    n_im = a_re * h_im + a_im * h_re + sim_ref[r8, ls]
                    sre_ref[r8, ls] = n_re
                    sim_ref[r8, ls] = n_im
                    return n_re, n_im

                h_re, h_im = lax.fori_loop(0, steps, step, (hre_ref[rs, ls], him_ref[rs, ls]),
                                           unroll=True)
                hre_ref[rs, ls] = h_re
                him_ref[rs, ls] = h_im

    for k in range(min(S5_REGIONS, sub)):
        input_side(k)
    for k in range(sub):
        scan_group(k)
        grp = region(k)
        for j in range(S5_NSUPER):
            st = slice(j * S5_SUPER_ST, (j + 1) * S5_SUPER_ST)
            ch = slice(j * S5_SUPER_CH, (j + 1) * S5_SUPER_CH)
            y = (_dot(sre_ref[grp, st].astype(BF16), wc_ref[j, 0:S5_SUPER_ST, :])
                 + _dot(sim_ref[grp, st].astype(BF16), wc_ref[j, S5_SUPER_ST:, :]))
            z_ref[grp, ch] = (y + d_ref[:, ch] * u_ref[grp, ch]).astype(BF16)
        z = _dot(unperm_ref[...], z_ref[grp, :]).astype(BF16)
        zz = _dot(z, wglu_ref[...]) + bglu_ref[...]
        out = zz[:, :D_MODEL] * _sigmoid(zz[:, D_MODEL:])
        res = x_group(k) + _rms(out, g_ref[1:2, :])
        if sub == 1:
            o_ref[...] = res.reshape(o_ref.shape)
        else:
            o_ref[:, k * steps:(k + 1) * steps, :] = res.reshape(bsz, steps, D_MODEL)
        if k + S5_REGIONS < sub:
            input_side(k + S5_REGIONS)


def _s5(x, steps_per_block, g2, prep, d_skip, w_glu, b_glu, h0_re, h0_im):
    wb, wc, a_re, a_im = prep
    bsz, t, _ = x.shape
    rows = steps_per_block * bsz
    r = np.arange(rows)
    perm = np.zeros((rows, rows), np.float32)
    perm[(r % steps_per_block) * bsz + r // steps_per_block, r] = 1.0
    zero2 = lambda i: (0, 0)
    zero3 = lambda i: (0, 0, 0)
    resident = dict(pipeline_mode=pl.Buffered(1))
    state = pl.BlockSpec((bsz, S5_FLAT), zero2)
    if steps_per_block == t:
        sub = 1
        x = x.reshape(1, rows, D_MODEL)
        xspec = pl.BlockSpec((1, rows, D_MODEL), lambda i: (0, 0, 0))
    else:
        sub = S5_GROUPS_PER_STEP
        assert steps_per_block % SUBLANE == 0 and t % (sub * steps_per_block) == 0
        xspec = pl.BlockSpec((bsz, sub * steps_per_block, D_MODEL), lambda i: (0, i, 0))
    y, h_re, h_im = pl.pallas_call(
        functools.partial(_s5_kernel, bsz=bsz, steps=steps_per_block, sub=sub),
        grid=(t // (sub * steps_per_block),),
        in_specs=[
            xspec,
            pl.BlockSpec((2, D_MODEL), zero2),
            pl.BlockSpec(wb.shape, zero3, **resident),
            pl.BlockSpec(wc.shape, zero3, **resident),
            pl.BlockSpec((1, S5_FLAT), zero2),
            pl.BlockSpec((1, S5_FLAT), zero2),
            pl.BlockSpec((1, D_MODEL), zero2),
            pl.BlockSpec((D_MODEL, 2 * D_MODEL), zero2, **resident),
            pl.BlockSpec((1, 2 * D_MODEL), zero2),
            state, state,
            pl.BlockSpec((rows, rows), zero2),
            pl.BlockSpec((rows, rows), zero2),
        ],
        out_specs=[xspec, state, state],
        out_shape=[
            jax.ShapeDtypeStruct(x.shape, F32),
            jax.ShapeDtypeStruct((bsz, S5_FLAT), F32),
            jax.ShapeDtypeStruct((bsz, S5_FLAT), F32),
        ],
        scratch_shapes=[
            pltpu.VMEM((min(sub, S5_REGIONS) * rows, D_MODEL), F32),
            pltpu.VMEM((min(sub, S5_REGIONS) * rows, S5_FLAT), F32),
            pltpu.VMEM((min(sub, S5_REGIONS) * rows, S5_FLAT), F32),
            pltpu.VMEM((min(sub, S5_REGIONS) * rows, D_MODEL), BF16),
        ],
        compiler_params=pltpu.CompilerParams(
            dimension_semantics=("arbitrary",), vmem_limit_bytes=VMEM_LIMIT),
        name="s5",
    )(x, g2, wb, wc, a_re, a_im, d_skip.reshape(1, D_MODEL), w_glu.astype(BF16),
      b_glu.reshape(1, 2 * D_MODEL), h0_re, h0_im,
      jnp.asarray(perm, BF16), jnp.asarray(perm.T, BF16))
    return y.reshape(bsz, t, D_MODEL), h_re, h_im


def kernel(x_prompt, x_sample, state_gla, state_s5_re, state_s5_im, norm_g, w_ffn_gu, w_ffn_down,
           gla_w_in, gla_w_g2, gla_b_g, gla_g_onorm, gla_w_out,
           s5_lam_re, s5_lam_im, s5_log_dt, s5_b_re, s5_b_im, s5_c_re, s5_c_im, s5_d, s5_w_glu,
           s5_b_glu):
    pb = x_prompt.shape[0]
    sb, st, _ = x_sample.shape

    def ffn(xp, xs, layer, which):
        g2 = norm_g[layer, 4 * which:4 * which + 2]
        xs, wgu_bf, wd_bf = _ffn_stream(xs, g2, w_ffn_gu, w_ffn_down, layer, which)
        return _ffn_resident(xp, g2, wgu_bf, wd_bf), xs

    gla_w = _gla_weights(gla_w_in[0], gla_w_g2[0], gla_b_g[0], gla_g_onorm[0], gla_w_out[0])
    xp, xs = ffn(x_prompt, x_sample, 0, 0)
    xp, gla_p = _gla_prompt(xp, norm_g[0, 2:4], gla_w)
    xs, gla_s = _gla_sample(xs, state_gla.reshape(sb, GLA_HEADS, GLA_DK, GLA_DV), norm_g[0, 2:4], gla_w)
    xp, xs = ffn(xp, xs, 0, 1)

    xp, xs = ffn(xp, xs, 1, 0)
    prep = _s5_prep(s5_lam_re[0], s5_lam_im[0], s5_log_dt[0], s5_b_re[0], s5_b_im[0],
                    s5_c_re[0], s5_c_im[0])
    zeros = jnp.zeros((pb, S5_FLAT), F32)
    s5_args = (norm_g[1, 2:4], prep, s5_d[0], s5_w_glu[0], s5_b_glu[0])
    xp, hre_p, him_p = _s5(xp, S5_STEPS, *s5_args, zeros, zeros)
    xs, hre_s, him_s = _s5(xs, st, *s5_args,
                           state_s5_re.reshape(sb, S5_FLAT), state_s5_im.reshape(sb, S5_FLAT))
    y_prompt, y_sample = ffn(xp, xs, 1, 1)

    def s5_state(h, b):
        return h.reshape(1, b, S5_GROUPS, S5_STATE)

    return (y_prompt, y_sample, gla_p.reshape(1, pb, GLA_HEADS, GLA_DK, GLA_DV),
            s5_state(hre_p, pb), s5_state(him_p, pb),
            gla_s.reshape(1, sb, GLA_HEADS, GLA_DK, GLA_DV),
            s5_state(hre_s, sb), s5_state(him_s, sb))
```

```python
import functools
import math

import jax
import jax.numpy as jnp
import numpy as np
from jax import lax
from jax.experimental import pallas as pl
from jax.experimental.pallas import tpu as pltpu

F32 = jnp.float32
BF16 = jnp.bfloat16

D_MODEL = 1024
D_FF = 2816
GLA_HEADS = 4
GLA_DK = 128
GLA_DV = 256
GLA_KEY_DIM = GLA_HEADS * GLA_DK
GLA_VAL_DIM = GLA_HEADS * GLA_DV
GATE_RANK = 16
GLA_MAIN_DIM = 2 * GLA_KEY_DIM + 2 * GLA_VAL_DIM
GLA_IN_DIM = GLA_MAIN_DIM + GATE_RANK
GATE_TAU = 16.0
S5_GROUP = 16
S5_GROUPS = 64
S5_STATE = 64
S5_FLAT = S5_GROUPS * S5_STATE
S5_SUPER = 8
S5_NSUPER = S5_GROUPS // S5_SUPER
S5_SUPER_CH = S5_SUPER * S5_GROUP
S5_SUPER_ST = S5_SUPER * S5_STATE
EPS = 1e-6
LANE = 128
SUBLANE = 8
VMEM_LIMIT = 56 * 1024 * 1024

FFN_ROWS = 1024
FFN_GROUP = 512
FFN_CHUNK = 256
FFN_SLOTS = 4
GLA_CHUNK = 64
GLA_TBLOCK = 512
GLA_PROJ_CHUNK = 256
GLA_CUM_ROWS = 256
LOG2E = math.log2(math.e)
GLA_SAMPLE_SEQS = 16
S5_LANES = 512
S5_STEPS = 32
S5_GROUPS_PER_STEP = 4
S5_REGIONS = 2


def _dot(a, b):
    return jnp.dot(a, b, preferred_element_type=F32)


def _dot_nt(a, b):
    return lax.dot_general(a, b, (((1,), (1,)), ((), ())), preferred_element_type=F32)


def _rms(x, g):
    ms = jnp.mean(x * x, axis=-1, keepdims=True)
    return x * lax.rsqrt(ms + EPS) * g


def _sigmoid(x):
    return 1.0 / (1.0 + jnp.exp(-x))


def _split_bf16(x):
    hi = x.astype(BF16)
    lo = (x - hi.astype(F32)).astype(BF16)
    return hi, lo


def _ffn_row_groups(n_rows):
    return [slice(r0, min(r0 + FFN_GROUP, n_rows)) for r0 in range(0, n_rows, FFN_GROUP)]


def _ffn_hidden_chunk(xn_ref, wgu_ref, act_ref, rows, c):
    lo = c * FFN_CHUNK
    gate = _dot(xn_ref[rows, :], wgu_ref[:, lo:lo + FFN_CHUNK])
    up = _dot(xn_ref[rows, :], wgu_ref[:, D_FF + lo:D_FF + lo + FFN_CHUNK])
    act_ref[rows, lo:lo + FFN_CHUNK] = (gate * _sigmoid(gate) * up).astype(BF16)


def _ffn_finish(x_ref, g_ref, wd_ref, o_ref, act_ref, rows):
    y = _dot(act_ref[rows, :], wd_ref[...])
    o_ref[rows, :] = x_ref[rows, :] + 0.5 * _rms(y, g_ref[1:2, :])


def _ffn_stream_kernel(x_ref, g_ref, wgu_hbm, wd_hbm, o_ref, wgu_ref, wd_ref,
                       xn_ref, act_ref, gu_stage, d_stage, sem, *, layer, which):
    n_chunks = D_FF // FFN_CHUNK
    groups = _ffn_row_groups(x_ref.shape[0])
    xn_ref[...] = _rms(x_ref[...], g_ref[0:1, :]).astype(BF16)

    def chunk_slices(c):
        lo = c * FFN_CHUNK
        return ((slice(None), pl.ds(lo, FFN_CHUNK)), (slice(None), pl.ds(D_FF + lo, FFN_CHUNK)),
                (pl.ds(lo, FFN_CHUNK), slice(None)))

    def fetches(c):
        slot = c % FFN_SLOTS
        gate, up, down = chunk_slices(c)
        return (
            pltpu.make_async_copy(wgu_hbm.at[(layer, which) + gate], gu_stage.at[slot, 0], sem.at[slot, 0]),
            pltpu.make_async_copy(wgu_hbm.at[(layer, which) + up], gu_stage.at[slot, 1], sem.at[slot, 1]),
            pltpu.make_async_copy(wd_hbm.at[(layer, which) + down], d_stage.at[slot], sem.at[slot, 2]),
        )

    ahead = FFN_SLOTS - 1
    for c in range(min(ahead, n_chunks)):
        for cp in fetches(c):
            cp.start()
    for c in range(n_chunks):
        slot, lo = c % FFN_SLOTS, c * FFN_CHUNK
        for cp in fetches(c):
            cp.wait()
        if c + ahead < n_chunks:
            for cp in fetches(c + ahead):
                cp.start()
        wgu_ref[:, lo:lo + FFN_CHUNK] = gu_stage[slot, 0].astype(BF16)
        wgu_ref[:, D_FF + lo:D_FF + lo + FFN_CHUNK] = gu_stage[slot, 1].astype(BF16)
        wd_ref[lo:lo + FFN_CHUNK, :] = d_stage[slot].astype(BF16)
        for rows in groups:
            _ffn_hidden_chunk(xn_ref, wgu_ref, act_ref, rows, c)
    for rows in groups:
        _ffn_finish(x_ref, g_ref, wd_ref, o_ref, act_ref, rows)


def _ffn_resident_kernel(x_ref, g_ref, wgu_ref, wd_ref, o_ref, xn_ref, act_ref):
    xn_ref[...] = _rms(x_ref[...], g_ref[0:1, :]).astype(BF16)
    for rows in _ffn_row_groups(x_ref.shape[0]):
        for c in range(D_FF // FFN_CHUNK):
            _ffn_hidden_chunk(xn_ref, wgu_ref, act_ref, rows, c)
        _ffn_finish(x_ref, g_ref, wd_ref, o_ref, act_ref, rows)


def _ffn_stream(x, g2, wgu, wd, layer, which):
    shape = x.shape
    x = x.reshape(-1, D_MODEL)
    n = x.shape[0]
    zero = lambda i: (0, 0)
    y, wgu_bf, wd_bf = pl.pallas_call(
        functools.partial(_ffn_stream_kernel, layer=layer, which=which),
        grid=(1,),
        in_specs=[
            pl.BlockSpec((n, D_MODEL), zero),
            pl.BlockSpec((2, D_MODEL), zero),
            pl.BlockSpec(memory_space=pl.ANY),
            pl.BlockSpec(memory_space=pl.ANY),
        ],
        out_specs=[
            pl.BlockSpec((n, D_MODEL), zero),
            pl.BlockSpec((D_MODEL, 2 * D_FF), zero, pipeline_mode=pl.Buffered(1)),
            pl.BlockSpec((D_FF, D_MODEL), zero, pipeline_mode=pl.Buffered(1)),
        ],
        out_shape=[
            jax.ShapeDtypeStruct((n, D_MODEL), F32),
            jax.ShapeDtypeStruct((D_MODEL, 2 * D_FF), BF16),
            jax.ShapeDtypeStruct((D_FF, D_MODEL), BF16),
        ],
        scratch_shapes=[
            pltpu.VMEM((n, D_MODEL), BF16),
            pltpu.VMEM((n, D_FF), BF16),
            pltpu.VMEM((FFN_SLOTS, 2, D_MODEL, FFN_CHUNK), F32),
            pltpu.VMEM((FFN_SLOTS, FFN_CHUNK, D_MODEL), F32),
            pltpu.SemaphoreType.DMA((FFN_SLOTS, 3)),
        ],
        compiler_params=pltpu.CompilerParams(
            dimension_semantics=("arbitrary",), vmem_limit_bytes=VMEM_LIMIT),
        name="ffn_stream",
    )(x, g2, wgu, wd)
    return y.reshape(shape), wgu_bf, wd_bf


def _ffn_resident(x, g2, wgu_bf, wd_bf):
    shape = x.shape
    x = x.reshape(-1, D_MODEL)
    n = x.shape[0]
    tm = min(FFN_ROWS, n)
    assert n % tm == 0
    zero = lambda i: (0, 0)
    resident = dict(pipeline_mode=pl.Buffered(1))
    return pl.pallas_call(
        _ffn_resident_kernel,
        grid=(n // tm,),
        in_specs=[
            pl.BlockSpec((tm, D_MODEL), lambda i: (i, 0)),
            pl.BlockSpec((2, D_MODEL), zero),
            pl.BlockSpec((D_MODEL, 2 * D_FF), zero, **resident),
            pl.BlockSpec((D_FF, D_MODEL), zero, **resident),
        ],
        out_specs=pl.BlockSpec((tm, D_MODEL), lambda i: (i, 0)),
        out_shape=jax.ShapeDtypeStruct((n, D_MODEL), F32),
        scratch_shapes=[pltpu.VMEM((tm, D_MODEL), BF16), pltpu.VMEM((tm, D_FF), BF16)],
        compiler_params=pltpu.CompilerParams(
            dimension_semantics=("arbitrary",), vmem_limit_bytes=VMEM_LIMIT),
        name="ffn",
    )(x, g2, wgu_bf, wd_bf).reshape(shape)


def _gla_tables(rows, seq):
    t = np.arange(rows)[:, None]
    r = np.arange(rows)[None, :]
    cum = ((t // seq) == (r // seq)) & (r <= t)
    masks = [t == r]
    m = seq // 2
    while m >= 1:
        masks.append(((t // (2 * m)) == (r // (2 * m))) & (t % (2 * m) >= m) & (r % (2 * m) < m))
        m //= 2
    return jnp.asarray(cum.astype(np.float32), BF16), jnp.asarray(np.stack(masks).astype(np.float32))


def _gla_project_cols(hn_ref, win_ref, proj_ref, c0):
    proj_ref[:, c0:c0 + GLA_PROJ_CHUNK] = _dot(hn_ref[...], win_ref[:, c0:c0 + GLA_PROJ_CHUNK])


def _gla_project_stages(x_ref, g_ref, win_ref, wglr_ref, wg2_ref, bg_ref, cum_ref,
                        hn_ref, proj_ref, lf_ref, b_ref):
    def norm_and_gate():
        x = x_ref[...].reshape(hn_ref.shape)
        hn_ref[...] = _rms(x, g_ref[0:1, :]).astype(BF16)
        glr = _dot(hn_ref[...], wglr_ref[...]).astype(BF16)
        gate = _dot(glr, wg2_ref[...]) + bg_ref[...]
        lf_ref[...] = (jnp.minimum(gate, 0.0) - jnp.log1p(jnp.exp(-jnp.abs(gate)))) * (1.0 / GATE_TAU)

    def cumulate():
        hi, lo = _split_bf16(lf_ref[...])
        span = cum_ref.shape[0]
        for r0 in range(0, lf_ref.shape[0], span):
            b_ref[r0:r0 + span, :] = (_dot(cum_ref[...], hi[r0:r0 + span])
                                      + _dot(cum_ref[...], lo[r0:r0 + span]))

    qk = [functools.partial(_gla_project_cols, hn_ref, win_ref, proj_ref, c0)
          for c0 in range(0, 2 * GLA_KEY_DIM, GLA_PROJ_CHUNK)]
    return [norm_and_gate] + qk + [cumulate]


def _block_rows(ref, r0, col, rows, period, offset):
    def bc(row, n):
        tile = ref[pl.ds(r0 + row // SUBLANE * SUBLANE, SUBLANE), col]
        return jnp.broadcast_to(tile[row % SUBLANE:row % SUBLANE + 1, :], (n, GLA_DK))

    if period >= SUBLANE:
        parts = [bc(p0 + offset, period) for p0 in range(0, rows, period)]
    else:
        assert 2 * period == SUBLANE
        low = lax.broadcasted_iota(jnp.int32, (SUBLANE, 1), 0) < period
        parts = [jnp.where(low, bc(t0 + offset, SUBLANE), bc(t0 + period + offset, SUBLANE))
                 for t0 in range(0, rows, SUBLANE)]
    return parts[0] if len(parts) == 1 else jnp.concatenate(parts, axis=0)


def _gla_attention(qs, k, b, lf, b_ref, r0, col, rows, seq, mask_ref):
    qb, kb = qs.astype(BF16), k.astype(BF16)
    att = _dot_nt(qb, kb) * mask_ref[0]
    lvl, m = 1, seq // 2
    while m >= 1:
        if m > 1:
            d = b - _block_rows(b_ref, r0, col, rows, 2 * m, m - 1)
            w = jnp.exp2(jnp.abs(d) * (-LOG2E))
        else:
            odd = lax.broadcasted_iota(jnp.int32, (rows, 1), 0) % 2 == 1
            w = jnp.where(odd, jnp.exp(lf), 1.0)
        wb = w.astype(BF16)
        att = att + _dot_nt(qb * wb, kb * wb) * mask_ref[lvl]
        lvl, m = lvl + 1, m // 2
    return att


def _gla_finish_head(o, r, gon_ref):
    on = _rms(o, gon_ref[...])
    return (on * (r * _sigmoid(r))).astype(BF16)


def _gla_prompt_kernel(x_ref, xnext_ref, g_ref, win_ref, wglr_ref, wg2_ref, bg_ref, gon_ref, wout_ref,
                       cum_ref, mask_ref, o_ref, snew_ref,
                       hn_ref, proj_ref, lf_ref, b_ref, og_ref, s_ref, att_ref, qb_ref, tt_ref):
    tb = pl.program_id(1)
    c = GLA_CHUNK

    @pl.when(tb == 0)
    def _():
        s_ref[...] = jnp.zeros_like(s_ref)

    def stages(src_ref):
        return _gla_project_stages(src_ref, g_ref, win_ref, wglr_ref, wg2_ref, bg_ref, cum_ref,
                                   hn_ref, proj_ref, lf_ref, b_ref)

    @pl.when((pl.program_id(0) == 0) & (tb == 0))
    def _():
        for stage in stages(x_ref):
            stage()

    x = x_ref[0]

    late_cols = list(range(2 * GLA_KEY_DIM, proj_ref.shape[1], GLA_PROJ_CHUNK))
    n_heads_total = GLA_TBLOCK // c * GLA_HEADS
    for ci, r0 in enumerate(range(0, GLA_TBLOCK, c)):
        rows = slice(r0, r0 + c)
        for h in range(GLA_HEADS):
            done = ci * GLA_HEADS + h
            for c0 in late_cols[done * len(late_cols) // n_heads_total:
                                (done + 1) * len(late_cols) // n_heads_total]:
                _gla_project_cols(hn_ref, win_ref, proj_ref, c0)
            kcol = slice(h * GLA_DK, (h + 1) * GLA_DK)
            qs = proj_ref[rows, kcol] * (GLA_DK ** -0.5)
            k = proj_ref[rows, GLA_KEY_DIM + h * GLA_DK:GLA_KEY_DIM + (h + 1) * GLA_DK]
            b = b_ref[rows, kcol]
            att = _gla_attention(qs, k, b, lf_ref[rows, kcol], b_ref, r0, kcol, c, c, mask_ref)
            att_ref[ci * GLA_HEADS + h] = att.astype(BF16)
            qb_ref[rows, kcol] = (qs * jnp.exp(b)).astype(BF16)
            khat = k * jnp.exp(_block_rows(b_ref, r0, kcol, c, c, c - 1) - b)
            tile = jnp.concatenate(
                [khat, jnp.exp(b[c - SUBLANE:c, :]), jnp.zeros((LANE - c - SUBLANE, GLA_DK), F32)], axis=0)
            tt_ref[ci * GLA_HEADS + h] = jnp.transpose(tile)

    next_stages = stages(xnext_ref)
    for ci, r0 in enumerate(range(0, GLA_TBLOCK, c)):
        rows = slice(r0, r0 + c)
        for h in range(GLA_HEADS):
            done = ci * GLA_HEADS + h
            for stage in next_stages[done * len(next_stages) // n_heads_total:
                                     (done + 1) * len(next_stages) // n_heads_total]:
                stage()
            kcol = slice(h * GLA_DK, (h + 1) * GLA_DK)
            vcol = slice(h * GLA_DV, (h + 1) * GLA_DV)
            v = proj_ref[rows, 2 * GLA_KEY_DIM + h * GLA_DV:2 * GLA_KEY_DIM + (h + 1) * GLA_DV].astype(BF16)
            r = proj_ref[rows, 2 * GLA_KEY_DIM + GLA_VAL_DIM + h * GLA_DV:
                         2 * GLA_KEY_DIM + GLA_VAL_DIM + (h + 1) * GLA_DV]
            s = s_ref[h]
            tt = tt_ref[ci * GLA_HEADS + h]
            o = _dot(att_ref[ci * GLA_HEADS + h], v) + _dot(qb_ref[rows, kcol], s.astype(BF16))
            decay = tt[:, c + SUBLANE - 1:c + SUBLANE]
            s_ref[h] = s * decay + _dot(tt[:, 0:c].astype(BF16), v)
            og_ref[rows, vcol] = _gla_finish_head(o, r, gon_ref)
    out = _dot(og_ref[...], wout_ref[...])
    o_ref[0] = x + _rms(out, g_ref[1:2, :])

    @pl.when(tb == pl.num_programs(1) - 1)
    def _():
        snew_ref[0] = s_ref[...]


def _gla_sample_kernel(x_ref, g_ref, win_ref, wglr_ref, wg2_ref, bg_ref, gon_ref, wout_ref,
                       cum_ref, mask_ref, s0_ref, o_ref, snew_ref,
                       hn_ref, proj_ref, lf_ref, b_ref, og_ref, *, seq):
    rows = x_ref.shape[0]
    grp = 2 * SUBLANE
    x = x_ref[...]
    for stage in _gla_project_stages(x_ref, g_ref, win_ref, wglr_ref, wg2_ref, bg_ref, cum_ref,
                                     hn_ref, proj_ref, lf_ref, b_ref):
        stage()
    for c0 in range(2 * GLA_KEY_DIM, proj_ref.shape[1], GLA_PROJ_CHUNK):
        _gla_project_cols(hn_ref, win_ref, proj_ref, c0)
    rid = lax.broadcasted_iota(jnp.int32, (grp, 1), 0)
    for h in range(GLA_HEADS):
        kcol = slice(h * GLA_DK, (h + 1) * GLA_DK)
        vcol = slice(h * GLA_DV, (h + 1) * GLA_DV)
        qs = proj_ref[:, kcol] * (GLA_DK ** -0.5)
        k = proj_ref[:, GLA_KEY_DIM + h * GLA_DK:GLA_KEY_DIM + (h + 1) * GLA_DK]
        v = proj_ref[:, 2 * GLA_KEY_DIM + h * GLA_DV:2 * GLA_KEY_DIM + (h + 1) * GLA_DV]
        r = proj_ref[:, 2 * GLA_KEY_DIM + GLA_VAL_DIM + h * GLA_DV:
                     2 * GLA_KEY_DIM + GLA_VAL_DIM + (h + 1) * GLA_DV]
        b = b_ref[:, kcol]
        att = _gla_attention(qs, k, b, lf_ref[:, kcol], b_ref, 0, kcol, rows, seq, mask_ref)
        eb = jnp.exp(b)
        qb = (qs * eb).astype(BF16)
        khat = k * jnp.exp(_block_rows(b_ref, 0, kcol, rows, seq, seq - 1) - b)
        o_intra = _dot(att.astype(BF16), v.astype(BF16))
        o_parts = []
        for gi in range(rows // grp):
            gr = slice(gi * grp, (gi + 1) * grp)
            tile = jnp.concatenate(
                [khat[gr], eb[gr], jnp.zeros((LANE - 2 * grp, GLA_DK), F32)], axis=0)
            tt = jnp.transpose(tile)
            kt = tt[:, 0:grp].astype(BF16)
            qg = qb[gr]
            vg = v[gr]
            o_g = jnp.zeros((grp, GLA_DV), F32)
            for si in range(grp // seq):
                sq = gi * (grp // seq) + si
                mine = (rid >= si * seq) & (rid < (si + 1) * seq)
                s = s0_ref[sq, h]
                o_g = o_g + jnp.where(mine, _dot(qg, s.astype(BF16)), 0.0)
                vm = jnp.where(mine, vg, 0.0).astype(BF16)
                last = grp + (si + 1) * seq - 1
                snew_ref[sq, h] = s * tt[:, last:last + 1] + _dot(kt, vm)
            o_parts.append(o_g)
        o = o_intra + jnp.concatenate(o_parts, axis=0)
        og_ref[:, vcol] = _gla_finish_head(o, r, gon_ref)
    out = _dot(og_ref[...], wout_ref[...])
    o_ref[...] = x + _rms(out, g_ref[1:2, :])


def _gla_weight_specs():
    zero = (lambda *_: (0, 0))
    resident = dict(pipeline_mode=pl.Buffered(1))
    return [
        pl.BlockSpec((2, D_MODEL), zero),
        pl.BlockSpec((D_MODEL, GLA_IN_DIM), zero, **resident),
        pl.BlockSpec((D_MODEL, LANE), zero, **resident),
        pl.BlockSpec((LANE, GLA_KEY_DIM), zero, **resident),
        pl.BlockSpec((1, GLA_KEY_DIM), zero),
        pl.BlockSpec((1, GLA_DV), zero),
        pl.BlockSpec((GLA_VAL_DIM, D_MODEL), zero, **resident),
    ]


def _gla_weights(w_in, w_g2, b_g, g_onorm, w_out):
    w_glr = jnp.pad(w_in[:, GLA_MAIN_DIM:], ((0, 0), (0, LANE - GATE_RANK))).astype(BF16)
    w_g2p = jnp.pad(w_g2, ((0, LANE - GATE_RANK), (0, 0))).astype(BF16)
    return (w_in.astype(BF16), w_glr, w_g2p, b_g.reshape(1, GLA_KEY_DIM),
            g_onorm.reshape(1, GLA_DV), w_out.astype(BF16))


def _gla_prompt(x, g2, weights):
    bsz, t, _ = x.shape
    cum, _ = _gla_tables(GLA_CUM_ROWS, GLA_CHUNK)
    _, masks = _gla_tables(GLA_CHUNK, GLA_CHUNK)
    n_main = GLA_MAIN_DIM
    const2 = lambda b, i: (0, 0)
    n_t = t // GLA_TBLOCK

    def next_block(b, i):
        f = jnp.minimum(b * n_t + i + 1, bsz * n_t - 1)
        return (f // n_t, f % n_t, 0)

    return pl.pallas_call(
        _gla_prompt_kernel,
        grid=(bsz, n_t),
        in_specs=[pl.BlockSpec((1, GLA_TBLOCK, D_MODEL), lambda b, i: (b, i, 0)),
                  pl.BlockSpec((1, GLA_TBLOCK, D_MODEL), next_block)]
        + _gla_weight_specs()
        + [pl.BlockSpec(cum.shape, const2),
           pl.BlockSpec(masks.shape, lambda b, i: (0, 0, 0))],
        out_specs=[
            pl.BlockSpec((1, GLA_TBLOCK, D_MODEL), lambda b, i: (b, i, 0)),
            pl.BlockSpec((1, GLA_HEADS, GLA_DK, GLA_DV), lambda b, i: (b, 0, 0, 0)),
        ],
        out_shape=[
            jax.ShapeDtypeStruct(x.shape, F32),
            jax.ShapeDtypeStruct((bsz, GLA_HEADS, GLA_DK, GLA_DV), F32),
        ],
        scratch_shapes=[
            pltpu.VMEM((GLA_TBLOCK, D_MODEL), BF16),
            pltpu.VMEM((GLA_TBLOCK, n_main), F32),
            pltpu.VMEM((GLA_TBLOCK, GLA_KEY_DIM), F32),
            pltpu.VMEM((GLA_TBLOCK, GLA_KEY_DIM), F32),
            pltpu.VMEM((GLA_TBLOCK, GLA_VAL_DIM), BF16),
            pltpu.VMEM((GLA_HEADS, GLA_DK, GLA_DV), F32),
            pltpu.VMEM((GLA_TBLOCK // GLA_CHUNK * GLA_HEADS, GLA_CHUNK, GLA_CHUNK), BF16),
            pltpu.VMEM((GLA_TBLOCK, GLA_KEY_DIM), BF16),
            pltpu.VMEM((GLA_TBLOCK // GLA_CHUNK * GLA_HEADS, LANE, GLA_DK), F32),
        ],
        compiler_params=pltpu.CompilerParams(
            dimension_semantics=("arbitrary", "arbitrary"), vmem_limit_bytes=VMEM_LIMIT),
        name="gla_prompt",
    )(x, x, g2, *weights, cum, masks)


def _gla_sample(x, s0, g2, weights):
    bsz, seq, _ = x.shape
    rows = GLA_SAMPLE_SEQS * seq
    cum, masks = _gla_tables(rows, seq)
    n_main = GLA_MAIN_DIM
    state_spec = pl.BlockSpec((GLA_SAMPLE_SEQS, GLA_HEADS, GLA_DK, GLA_DV), lambda i: (i, 0, 0, 0))
    y, snew = pl.pallas_call(
        functools.partial(_gla_sample_kernel, seq=seq),
        grid=(bsz // GLA_SAMPLE_SEQS,),
        in_specs=[pl.BlockSpec((rows, D_MODEL), lambda i: (i, 0))]
        + _gla_weight_specs()
        + [pl.BlockSpec(cum.shape, lambda i: (0, 0)),
           pl.BlockSpec(masks.shape, lambda i: (0, 0, 0)),
           state_spec],
        out_specs=[pl.BlockSpec((rows, D_MODEL), lambda i: (i, 0)), state_spec],
        out_shape=[
            jax.ShapeDtypeStruct((bsz * seq, D_MODEL), F32),
            jax.ShapeDtypeStruct(s0.shape, F32),
        ],
        scratch_shapes=[
            pltpu.VMEM((rows, D_MODEL), BF16),
            pltpu.VMEM((rows, n_main), F32),
            pltpu.VMEM((rows, GLA_KEY_DIM), F32),
            pltpu.VMEM((rows, GLA_KEY_DIM), F32),
            pltpu.VMEM((rows, GLA_VAL_DIM), BF16),
        ],
        compiler_params=pltpu.CompilerParams(
            dimension_semantics=("arbitrary",), vmem_limit_bytes=VMEM_LIMIT),
        name="gla_sample",
    )(x.reshape(bsz * seq, D_MODEL), g2, *weights, cum, masks, s0)
    return y.reshape(x.shape), snew


def _s5_discretize(lam_re, lam_im, log_dt):
    dt = jnp.exp(log_dt)
    mag = jnp.exp(lam_re * dt)
    ang = lam_im * dt
    a_re, a_im = mag * jnp.cos(ang), mag * jnp.sin(ang)
    nr, ni = a_re - 1.0, a_im
    den = lam_re * lam_re + lam_im * lam_im
    f_re = (nr * lam_re + ni * lam_im) / den
    f_im = (ni * lam_re - nr * lam_im) / den
    return a_re, a_im, f_re, f_im


def _s5_prep_kernel(lre_ref, lim_ref, ldt_ref, bre_ref, bim_ref, cre_ref, cim_ref,
                    lref_ref, limf_ref, ldtf_ref, wb_ref, wc_ref, are_ref, aim_ref):
    _, _, f_re, f_im = _s5_discretize(lre_ref[...], lim_ref[...], ldt_ref[...])
    b_re, b_im = bre_ref[...], bim_ref[...]
    row = lax.broadcasted_iota(jnp.int32, (S5_SUPER_CH, S5_SUPER_ST), 0)
    col = lax.broadcasted_iota(jnp.int32, (S5_SUPER_CH, S5_SUPER_ST), 1)
    own = (row // S5_GROUP) == (col // S5_STATE)
    wb_ref[0, :, 0:S5_SUPER_ST] = jnp.where(own, f_re * b_re - f_im * b_im, 0.0).astype(BF16)
    wb_ref[0, :, S5_SUPER_ST:] = jnp.where(own, f_re * b_im + f_im * b_re, 0.0).astype(BF16)
    rowc = lax.broadcasted_iota(jnp.int32, (S5_SUPER_ST, S5_SUPER_CH), 0)
    colc = lax.broadcasted_iota(jnp.int32, (S5_SUPER_ST, S5_SUPER_CH), 1)
    ownc = (rowc // S5_STATE) == (colc // S5_GROUP)
    wc_ref[0, 0:S5_SUPER_ST, :] = jnp.where(ownc, cre_ref[...], 0.0).astype(BF16)
    wc_ref[0, S5_SUPER_ST:, :] = jnp.where(ownc, -cim_ref[...], 0.0).astype(BF16)
    a_re, a_im, _, _ = _s5_discretize(lref_ref[...], limf_ref[...], ldtf_ref[...])
    are_ref[...] = a_re
    aim_ref[...] = a_im


def _s5_prep(lam_re, lam_im, log_dt, b_re, b_im, c_re, c_im):
    g, p, ch, sg = S5_GROUPS, S5_STATE, S5_GROUP, S5_SUPER

    def rows_by_group(a_gp):
        return jnp.broadcast_to(a_gp[:, None, None, :], (g, ch, sg, p)).reshape(g * ch, sg * p)

    def b_layout(b):
        bt = jnp.transpose(b, (0, 2, 1))
        return jnp.broadcast_to(bt[:, :, None, :], (g, ch, sg, p)).reshape(g * ch, sg * p)

    def c_layout(cm):
        ct = jnp.transpose(cm.reshape(g // sg, sg, ch, p), (3, 0, 1, 2)).reshape(p, g * ch)
        return jnp.broadcast_to(ct[None], (sg, p, g * ch)).reshape(sg * p, g * ch)

    ldt_gp = jnp.broadcast_to(log_dt[:, None], (g, p))
    big = pl.BlockSpec((S5_SUPER_CH, S5_SUPER_ST), lambda j: (j, 0))
    cspec = pl.BlockSpec((S5_SUPER_ST, S5_SUPER_CH), lambda j: (0, j))
    flat = pl.BlockSpec((1, S5_SUPER_ST), lambda j: (0, j))
    return pl.pallas_call(
        _s5_prep_kernel,
        grid=(S5_NSUPER,),
        in_specs=[big] * 5 + [cspec] * 2 + [flat] * 3,
        out_specs=[
            pl.BlockSpec((1, S5_SUPER_CH, 2 * S5_SUPER_ST), lambda j: (j, 0, 0)),
            pl.BlockSpec((1, 2 * S5_SUPER_ST, S5_SUPER_CH), lambda j: (j, 0, 0)),
            flat, flat,
        ],
        out_shape=[
            jax.ShapeDtypeStruct((S5_NSUPER, S5_SUPER_CH, 2 * S5_SUPER_ST), BF16),
            jax.ShapeDtypeStruct((S5_NSUPER, 2 * S5_SUPER_ST, S5_SUPER_CH), BF16),
            jax.ShapeDtypeStruct((1, S5_FLAT), F32),
            jax.ShapeDtypeStruct((1, S5_FLAT), F32),
        ],
        compiler_params=pltpu.CompilerParams(dimension_semantics=("arbitrary",)),
        name="s5_prep",
    )(rows_by_group(lam_re), rows_by_group(lam_im), rows_by_group(ldt_gp),
      b_layout(b_re), b_layout(b_im), c_layout(c_re), c_layout(c_im),
      lam_re.reshape(1, S5_FLAT), lam_im.reshape(1, S5_FLAT), ldt_gp.reshape(1, S5_FLAT))


def _s5_kernel(x_ref, xnext_ref, g_ref, wb_ref, wc_ref, are_ref, aim_ref, d_ref, wglu_ref, bglu_ref,
               h0re_ref, h0im_ref, perm_ref, unperm_ref, o_ref, hre_ref, him_ref,
               u_ref, sre_ref, sim_ref, z_ref, *, bsz, steps, sub):
    i = pl.program_id(0)
    rows = bsz * steps

    @pl.when(i == 0)
    def _():
        hre_ref[...] = h0re_ref[...]
        him_ref[...] = h0im_ref[...]

    def x_group(k, src_ref=x_ref):
        xk = src_ref[...] if sub == 1 else src_ref[:, k * steps:(k + 1) * steps, :]
        return xk.reshape(rows, D_MODEL)

    def region(k):
        r0 = (k % S5_REGIONS) * rows
        return slice(r0, r0 + rows)

    def input_side(k, src_ref=x_ref):
        grp = region(k)
        u = _dot(perm_ref[...], _rms(x_group(k, src_ref), g_ref[0:1, :]).astype(BF16))
        u_ref[grp, :] = u
        ub = u.astype(BF16)
        for j in range(S5_NSUPER):
            bu = _dot(ub[:, j * S5_SUPER_CH:(j + 1) * S5_SUPER_CH], wb_ref[j])
            sre_ref[grp, j * S5_SUPER_ST:(j + 1) * S5_SUPER_ST] = bu[:, :S5_SUPER_ST]
            sim_ref[grp, j * S5_SUPER_ST:(j + 1) * S5_SUPER_ST] = bu[:, S5_SUPER_ST:]

    def scan_group(k):
        base = (k % S5_REGIONS) * rows
        for lc in range(S5_FLAT // S5_LANES):
            ls = slice(lc * S5_LANES, (lc + 1) * S5_LANES)
            a_re = jnp.broadcast_to(are_ref[:, ls], (SUBLANE, S5_LANES))
            a_im = jnp.broadcast_to(aim_ref[:, ls], (SUBLANE, S5_LANES))
            for rt in range(bsz // SUBLANE):
                rs = slice(rt * SUBLANE, (rt + 1) * SUBLANE)

                def step(t, carry, ls=ls, rt=rt, a_re=a_re, a_im=a_im):
                    h_re, h_im = carry
                    r8 = pl.ds(pl.multiple_of(base + t * bsz + rt * SUBLANE, SUBLANE), SUBLANE)
                    n_re = a_re * h_re - a_im * h_im + sre_ref[r8, ls]
                    n_im = a_re * h_im + a_im * h_re + sim_ref[r8, ls]
                    sre_ref[r8, ls] = n_re
                    sim_ref[r8, ls] = n_im
                    return n_re, n_im

                h_re, h_im = lax.fori_loop(0, steps, step, (hre_ref[rs, ls], him_ref[rs, ls]),
                                           unroll=True)
                hre_ref[rs, ls] = h_re
                him_ref[rs, ls] = h_im

    rotate = sub >= S5_REGIONS and sub % S5_REGIONS == 0
    if rotate:
        @pl.when(i == 0)
        def _():
            for k in range(S5_REGIONS):
                input_side(k)
    else:
        for k in range(min(S5_REGIONS, sub)):
            input_side(k)
    for k in range(sub):
        scan_group(k)
        grp = region(k)
        for j in range(S5_NSUPER):
            st = slice(j * S5_SUPER_ST, (j + 1) * S5_SUPER_ST)
            ch = slice(j * S5_SUPER_CH, (j + 1) * S5_SUPER_CH)
            y = (_dot(sre_ref[grp, st].astype(BF16), wc_ref[j, 0:S5_SUPER_ST, :])
                 + _dot(sim_ref[grp, st].astype(BF16), wc_ref[j, S5_SUPER_ST:, :]))
            z_ref[grp, ch] = (y + d_ref[:, ch] * u_ref[grp, ch]).astype(BF16)
        z = _dot(unperm_ref[...], z_ref[grp, :]).astype(BF16)
        zz = _dot(z, wglu_ref[...]) + bglu_ref[...]
        out = zz[:, :D_MODEL] * _sigmoid(zz[:, D_MODEL:])
        res = x_group(k) + _rms(out, g_ref[1:2, :])
        if sub == 1:
            o_ref[...] = res.reshape(o_ref.shape)
        else:
            o_ref[:, k * steps:(k + 1) * steps, :] = res.reshape(bsz, steps, D_MODEL)
        if k + S5_REGIONS < sub:
            input_side(k + S5_REGIONS)
        elif rotate:
            input_side(k + S5_REGIONS - sub, xnext_ref)


def _s5(x, steps_per_block, g2, prep, d_skip, w_glu, b_glu, h0_re, h0_im):
    wb, wc, a_re, a_im = prep
    bsz, t, _ = x.shape
    rows = steps_per_block * bsz
    r = np.arange(rows)
    perm = np.zeros((rows, rows), np.float32)
    perm[(r % steps_per_block) * bsz + r // steps_per_block, r] = 1.0
    zero2 = lambda i: (0, 0)
    zero3 = lambda i: (0, 0, 0)
    resident = dict(pipeline_mode=pl.Buffered(1))
    state = pl.BlockSpec((bsz, S5_FLAT), zero2)
    if steps_per_block == t:
        sub = 1
        x = x.reshape(1, rows, D_MODEL)
        xspec = xnext_spec = pl.BlockSpec((1, rows, D_MODEL), lambda i: (0, 0, 0))
    else:
        sub = S5_GROUPS_PER_STEP
        assert steps_per_block % SUBLANE == 0 and t % (sub * steps_per_block) == 0
        last = t // (sub * steps_per_block) - 1
        xspec = pl.BlockSpec((bsz, sub * steps_per_block, D_MODEL), lambda i: (0, i, 0))
        xnext_spec = pl.BlockSpec((bsz, sub * steps_per_block, D_MODEL),
                                  lambda i: (0, jnp.minimum(i + 1, last), 0))
    y, h_re, h_im = pl.pallas_call(
        functools.partial(_s5_kernel, bsz=bsz, steps=steps_per_block, sub=sub),
        grid=(t // (sub * steps_per_block),),
        in_specs=[
            xspec,
            xnext_spec,
            pl.BlockSpec((2, D_MODEL), zero2),
            pl.BlockSpec(wb.shape, zero3, **resident),
            pl.BlockSpec(wc.shape, zero3, **resident),
            pl.BlockSpec((1, S5_FLAT), zero2),
            pl.BlockSpec((1, S5_FLAT), zero2),
            pl.BlockSpec((1, D_MODEL), zero2),
            pl.BlockSpec((D_MODEL, 2 * D_MODEL), zero2, **resident),
            pl.BlockSpec((1, 2 * D_MODEL), zero2),
            state, state,
            pl.BlockSpec((rows, rows), zero2),
            pl.BlockSpec((rows, rows), zero2),
        ],
        out_specs=[xspec, state, state],
        out_shape=[
            jax.ShapeDtypeStruct(x.shape, F32),
            jax.ShapeDtypeStruct((bsz, S5_FLAT), F32),
            jax.ShapeDtypeStruct((bsz, S5_FLAT), F32),
        ],
        scratch_shapes=[
            pltpu.VMEM((min(sub, S5_REGIONS) * rows, D_MODEL), F32),
            pltpu.VMEM((min(sub, S5_REGIONS) * rows, S5_FLAT), F32),
            pltpu.VMEM((min(sub, S5_REGIONS) * rows, S5_FLAT), F32),
            pltpu.VMEM((min(sub, S5_REGIONS) * rows, D_MODEL), BF16),
        ],
        compiler_params=pltpu.CompilerParams(
            dimension_semantics=("arbitrary",), vmem_limit_bytes=VMEM_LIMIT),
        name="s5",
    )(x, x, g2, wb, wc, a_re, a_im, d_skip.reshape(1, D_MODEL), w_glu.astype(BF16),
      b_glu.reshape(1, 2 * D_MODEL), h0_re, h0_im,
      jnp.asarray(perm, BF16), jnp.asarray(perm.T, BF16))
    return y.reshape(bsz, t, D_MODEL), h_re, h_im


def kernel(x_prompt, x_sample, state_gla, state_s5_re, state_s5_im, norm_g, w_ffn_gu, w_ffn_down,
           gla_w_in, gla_w_g2, gla_b_g, gla_g_onorm, gla_w_out,
           s5_lam_re, s5_lam_im, s5_log_dt, s5_b_re, s5_b_im, s5_c_re, s5_c_im, s5_d, s5_w_glu,
           s5_b_glu):
    pb = x_prompt.shape[0]
    sb, st, _ = x_sample.shape

    def ffn(xp, xs, layer, which):
        g2 = norm_g[layer, 4 * which:4 * which + 2]
        xs, wgu_bf, wd_bf = _ffn_stream(xs, g2, w_ffn_gu, w_ffn_down, layer, which)
        return _ffn_resident(xp, g2, wgu_bf, wd_bf), xs

    gla_w = _gla_weights(gla_w_in[0], gla_w_g2[0], gla_b_g[0], gla_g_onorm[0], gla_w_out[0])
    xp, xs = ffn(x_prompt, x_sample, 0, 0)
    xp, gla_p = _gla_prompt(xp, norm_g[0, 2:4], gla_w)
    xs, gla_s = _gla_sample(xs, state_gla.reshape(sb, GLA_HEADS, GLA_DK, GLA_DV), norm_g[0, 2:4], gla_w)
    xp, xs = ffn(xp, xs, 0, 1)

    xp, xs = ffn(xp, xs, 1, 0)
    prep = _s5_prep(s5_lam_re[0], s5_lam_im[0], s5_log_dt[0], s5_b_re[0], s5_b_im[0],
                    s5_c_re[0], s5_c_im[0])
    zeros = jnp.zeros((pb, S5_FLAT), F32)
    s5_args = (norm_g[1, 2:4], prep, s5_d[0], s5_w_glu[0], s5_b_glu[0])
    xp, hre_p, him_p = _s5(xp, S5_STEPS, *s5_args, zeros, zeros)
    xs, hre_s, him_s = _s5(xs, st, *s5_args,
                           state_s5_re.reshape(sb, S5_FLAT), state_s5_im.reshape(sb, S5_FLAT))
    y_prompt, y_sample = ffn(xp, xs, 1, 1)

    def s5_state(h, b):
        return h.reshape(1, b, S5_GROUPS, S5_STATE)

    return (y_prompt, y_sample, gla_p.reshape(1, pb, GLA_HEADS, GLA_DK, GLA_DV),
            s5_state(hre_p, pb), s5_state(him_p, pb),
            gla_s.reshape(1, sb, GLA_HEADS, GLA_DK, GLA_DV),
            s5_state(hre_s, sb), s5_state(him_s, sb))
```

```python
import functools
import math

import jax
import jax.numpy as jnp
import numpy as np
from jax import lax
from jax.experimental import pallas as pl
from jax.experimental.pallas import tpu as pltpu

F32 = jnp.float32
BF16 = jnp.bfloat16

D_MODEL = 1024
D_FF = 2816
GLA_HEADS = 4
GLA_DK = 128
GLA_DV = 256
GLA_KEY_DIM = GLA_HEADS * GLA_DK
GLA_VAL_DIM = GLA_HEADS * GLA_DV
GATE_RANK = 16
GLA_MAIN_DIM = 2 * GLA_KEY_DIM + 2 * GLA_VAL_DIM
GLA_IN_DIM = GLA_MAIN_DIM + GATE_RANK
GATE_TAU = 16.0
S5_GROUP = 16
S5_GROUPS = 64
S5_STATE = 64
S5_FLAT = S5_GROUPS * S5_STATE
S5_SUPER = 8
S5_NSUPER = S5_GROUPS // S5_SUPER
S5_SUPER_CH = S5_SUPER * S5_GROUP
S5_SUPER_ST = S5_SUPER * S5_STATE
EPS = 1e-6
LANE = 128
SUBLANE = 8
VMEM_LIMIT = 56 * 1024 * 1024

FFN_ROWS = 1024
FFN_GROUP = 512
FFN_CHUNK = 256
FFN_SLOTS = 4
GLA_CHUNK = 64
GLA_TBLOCK = 512
GLA_PROJ_CHUNK = 256
GLA_CUM_ROWS = 256
LOG2E = math.log2(math.e)
GLA_SAMPLE_SEQS = 16
S5_LANES = 512
S5_STEPS = 32
S5_GROUPS_PER_STEP = 4
S5_REGIONS = 2


def _dot(a, b):
    return jnp.dot(a, b, preferred_element_type=F32)


def _dot_nt(a, b):
    return lax.dot_general(a, b, (((1,), (1,)), ((), ())), preferred_element_type=F32)


def _rms(x, g):
    ms = jnp.mean(x * x, axis=-1, keepdims=True)
    return x * lax.rsqrt(ms + EPS) * g


def _sigmoid(x):
    return 1.0 / (1.0 + jnp.exp(-x))


def _split_bf16(x):
    hi = x.astype(BF16)
    lo = (x - hi.astype(F32)).astype(BF16)
    return hi, lo


def _ffn_row_groups(n_rows):
    return [slice(r0, min(r0 + FFN_GROUP, n_rows)) for r0 in range(0, n_rows, FFN_GROUP)]


def _ffn_hidden_chunk(xn_ref, wgu_ref, act_ref, rows, c):
    lo = c * FFN_CHUNK
    gate = _dot(xn_ref[rows, :], wgu_ref[:, lo:lo + FFN_CHUNK])
    up = _dot(xn_ref[rows, :], wgu_ref[:, D_FF + lo:D_FF + lo + FFN_CHUNK])
    act_ref[rows, lo:lo + FFN_CHUNK] = (gate * _sigmoid(gate) * up).astype(BF16)


def _ffn_finish(x_ref, g_ref, wd_ref, o_ref, act_ref, rows):
    y = _dot(act_ref[rows, :], wd_ref[...])
    o_ref[rows, :] = x_ref[rows, :] + 0.5 * _rms(y, g_ref[1:2, :])


def _ffn_stream_kernel(x_ref, g_ref, wgu_hbm, wd_hbm, o_ref, wgu_ref, wd_ref,
                       xn_ref, act_ref, gu_stage, d_stage, sem, *, layer, which):
    n_chunks = D_FF // FFN_CHUNK
    groups = _ffn_row_groups(x_ref.shape[0])
    xn_ref[...] = _rms(x_ref[...], g_ref[0:1, :]).astype(BF16)

    def chunk_slices(c):
        lo = c * FFN_CHUNK
        return ((slice(None), pl.ds(lo, FFN_CHUNK)), (slice(None), pl.ds(D_FF + lo, FFN_CHUNK)),
                (pl.ds(lo, FFN_CHUNK), slice(None)))

    def fetches(c):
        slot = c % FFN_SLOTS
        gate, up, down = chunk_slices(c)
        return (
            pltpu.make_async_copy(wgu_hbm.at[(layer, which) + gate], gu_stage.at[slot, 0], sem.at[slot, 0]),
            pltpu.make_async_copy(wgu_hbm.at[(layer, which) + up], gu_stage.at[slot, 1], sem.at[slot, 1]),
            pltpu.make_async_copy(wd_hbm.at[(layer, which) + down], d_stage.at[slot], sem.at[slot, 2]),
        )

    ahead = FFN_SLOTS - 1
    for c in range(min(ahead, n_chunks)):
        for cp in fetches(c):
            cp.start()
    for c in range(n_chunks):
        slot, lo = c % FFN_SLOTS, c * FFN_CHUNK
        for cp in fetches(c):
            cp.wait()
        if c + ahead < n_chunks:
            for cp in fetches(c + ahead):
                cp.start()
        wgu_ref[:, lo:lo + FFN_CHUNK] = gu_stage[slot, 0].astype(BF16)
        wgu_ref[:, D_FF + lo:D_FF + lo + FFN_CHUNK] = gu_stage[slot, 1].astype(BF16)
        wd_ref[lo:lo + FFN_CHUNK, :] = d_stage[slot].astype(BF16)
        for rows in groups:
            _ffn_hidden_chunk(xn_ref, wgu_ref, act_ref, rows, c)
    for rows in groups:
        _ffn_finish(x_ref, g_ref, wd_ref, o_ref, act_ref, rows)


def _ffn_resident_kernel(x_ref, g_ref, wgu_ref, wd_ref, o_ref, xn_ref, act_ref):
    xn_ref[...] = _rms(x_ref[...], g_ref[0:1, :]).astype(BF16)
    for rows in _ffn_row_groups(x_ref.shape[0]):
        for c in range(D_FF // FFN_CHUNK):
            _ffn_hidden_chunk(xn_ref, wgu_ref, act_ref, rows, c)
        _ffn_finish(x_ref, g_ref, wd_ref, o_ref, act_ref, rows)


def _ffn_stream(x, g2, wgu, wd, layer, which):
    shape = x.shape
    x = x.reshape(-1, D_MODEL)
    n = x.shape[0]
    zero = lambda i: (0, 0)
    y, wgu_bf, wd_bf = pl.pallas_call(
        functools.partial(_ffn_stream_kernel, layer=layer, which=which),
        grid=(1,),
        in_specs=[
            pl.BlockSpec((n, D_MODEL), zero),
            pl.BlockSpec((2, D_MODEL), zero),
            pl.BlockSpec(memory_space=pl.ANY),
            pl.BlockSpec(memory_space=pl.ANY),
        ],
        out_specs=[
            pl.BlockSpec((n, D_MODEL), zero),
            pl.BlockSpec((D_MODEL, 2 * D_FF), zero, pipeline_mode=pl.Buffered(1)),
            pl.BlockSpec((D_FF, D_MODEL), zero, pipeline_mode=pl.Buffered(1)),
        ],
        out_shape=[
            jax.ShapeDtypeStruct((n, D_MODEL), F32),
            jax.ShapeDtypeStruct((D_MODEL, 2 * D_FF), BF16),
            jax.ShapeDtypeStruct((D_FF, D_MODEL), BF16),
        ],
        scratch_shapes=[
            pltpu.VMEM((n, D_MODEL), BF16),
            pltpu.VMEM((n, D_FF), BF16),
            pltpu.VMEM((FFN_SLOTS, 2, D_MODEL, FFN_CHUNK), F32),
            pltpu.VMEM((FFN_SLOTS, FFN_CHUNK, D_MODEL), F32),
            pltpu.SemaphoreType.DMA((FFN_SLOTS, 3)),
        ],
        compiler_params=pltpu.CompilerParams(
            dimension_semantics=("arbitrary",), vmem_limit_bytes=VMEM_LIMIT),
        name="ffn_stream",
    )(x, g2, wgu, wd)
    return y.reshape(shape), wgu_bf, wd_bf


def _ffn_resident(x, g2, wgu_bf, wd_bf):
    shape = x.shape
    x = x.reshape(-1, D_MODEL)
    n = x.shape[0]
    tm = min(FFN_ROWS, n)
    assert n % tm == 0
    zero = lambda i: (0, 0)
    resident = dict(pipeline_mode=pl.Buffered(1))
    return pl.pallas_call(
        _ffn_resident_kernel,
        grid=(n // tm,),
        in_specs=[
            pl.BlockSpec((tm, D_MODEL), lambda i: (i, 0)),
            pl.BlockSpec((2, D_MODEL), zero),
            pl.BlockSpec((D_MODEL, 2 * D_FF), zero, **resident),
            pl.BlockSpec((D_FF, D_MODEL), zero, **resident),
        ],
        out_specs=pl.BlockSpec((tm, D_MODEL), lambda i: (i, 0)),
        out_shape=jax.ShapeDtypeStruct((n, D_MODEL), F32),
        scratch_shapes=[pltpu.VMEM((tm, D_MODEL), BF16), pltpu.VMEM((tm, D_FF), BF16)],
        compiler_params=pltpu.CompilerParams(
            dimension_semantics=("arbitrary",), vmem_limit_bytes=VMEM_LIMIT),
        name="ffn",
    )(x, g2, wgu_bf, wd_bf).reshape(shape)


def _gla_tables(rows, seq):
    t = np.arange(rows)[:, None]
    r = np.arange(rows)[None, :]
    cum = ((t // seq) == (r // seq)) & (r <= t)
    masks = [t == r]
    m = seq // 2
    while m >= 1:
        masks.append(((t // (2 * m)) == (r // (2 * m))) & (t % (2 * m) >= m) & (r % (2 * m) < m))
        m //= 2
    return jnp.asarray(cum.astype(np.float32), BF16), jnp.asarray(np.stack(masks).astype(np.float32))


def _gla_project_cols(hn_ref, win_ref, proj_ref, c0):
    proj_ref[:, c0:c0 + GLA_PROJ_CHUNK] = _dot(hn_ref[...], win_ref[:, c0:c0 + GLA_PROJ_CHUNK])


def _gla_project(x, g_ref, win_ref, wglr_ref, wg2_ref, bg_ref, cum_ref, hn_ref, proj_ref, lf_ref, b_ref):
    hn = _rms(x, g_ref[0:1, :]).astype(BF16)
    hn_ref[...] = hn
    glr = _dot(hn, wglr_ref[...]).astype(BF16)
    gate = _dot(glr, wg2_ref[...]) + bg_ref[...]
    for c0 in range(0, 2 * GLA_KEY_DIM, GLA_PROJ_CHUNK):
        _gla_project_cols(hn_ref, win_ref, proj_ref, c0)
    lf = (jnp.minimum(gate, 0.0) - jnp.log1p(jnp.exp(-jnp.abs(gate)))) * (1.0 / GATE_TAU)
    lf_ref[...] = lf
    hi, lo = _split_bf16(lf)
    span = cum_ref.shape[0]
    for r0 in range(0, lf.shape[0], span):
        b_ref[r0:r0 + span, :] = (_dot(cum_ref[...], hi[r0:r0 + span]) + _dot(cum_ref[...], lo[r0:r0 + span]))


def _block_rows(ref, r0, col, rows, period, offset):
    def bc(row, n):
        tile = ref[pl.ds(r0 + row // SUBLANE * SUBLANE, SUBLANE), col]
        return jnp.broadcast_to(tile[row % SUBLANE:row % SUBLANE + 1, :], (n, GLA_DK))

    if period >= SUBLANE:
        parts = [bc(p0 + offset, period) for p0 in range(0, rows, period)]
    else:
        assert 2 * period == SUBLANE
        low = lax.broadcasted_iota(jnp.int32, (SUBLANE, 1), 0) < period
        parts = [jnp.where(low, bc(t0 + offset, SUBLANE), bc(t0 + period + offset, SUBLANE))
                 for t0 in range(0, rows, SUBLANE)]
    return parts[0] if len(parts) == 1 else jnp.concatenate(parts, axis=0)


def _gla_attention(qs, k, b, lf, b_ref, r0, col, rows, seq, mask_ref):
    qb, kb = qs.astype(BF16), k.astype(BF16)
    att = _dot_nt(qb, kb) * mask_ref[0]
    lvl, m = 1, seq // 2
    while m >= 1:
        if m > 1:
            d = b - _block_rows(b_ref, r0, col, rows, 2 * m, m - 1)
            w = jnp.exp2(jnp.abs(d) * (-LOG2E))
        else:
            odd = lax.broadcasted_iota(jnp.int32, (rows, 1), 0) % 2 == 1
            w = jnp.where(odd, jnp.exp(lf), 1.0)
        wb = w.astype(BF16)
        att = att + _dot_nt(qb * wb, kb * wb) * mask_ref[lvl]
        lvl, m = lvl + 1, m // 2
    return att


def _gla_finish_head(o, r, gon_ref):
    on = _rms(o, gon_ref[...])
    return (on * (r * _sigmoid(r))).astype(BF16)


def _gla_prompt_kernel(x_ref, g_ref, win_ref, wglr_ref, wg2_ref, bg_ref, gon_ref, wout_ref,
                       cum_ref, mask_ref, o_ref, snew_ref,
                       hn_ref, proj_ref, lf_ref, b_ref, og_ref, s_ref, att_ref, qb_ref, tt_ref):
    tb = pl.program_id(1)
    c = GLA_CHUNK

    @pl.when(tb == 0)
    def _():
        s_ref[...] = jnp.zeros_like(s_ref)

    x = x_ref[0]
    _gla_project(x, g_ref, win_ref, wglr_ref, wg2_ref, bg_ref, cum_ref, hn_ref, proj_ref, lf_ref, b_ref)

    late_cols = list(range(2 * GLA_KEY_DIM, proj_ref.shape[1], GLA_PROJ_CHUNK))
    n_heads_total = GLA_TBLOCK // c * GLA_HEADS
    for ci, r0 in enumerate(range(0, GLA_TBLOCK, c)):
        rows = slice(r0, r0 + c)
        for h in range(GLA_HEADS):
            done = ci * GLA_HEADS + h
            for c0 in late_cols[done * len(late_cols) // n_heads_total:
                                (done + 1) * len(late_cols) // n_heads_total]:
                _gla_project_cols(hn_ref, win_ref, proj_ref, c0)
            kcol = slice(h * GLA_DK, (h + 1) * GLA_DK)
            qs = proj_ref[rows, kcol] * (GLA_DK ** -0.5)
            k = proj_ref[rows, GLA_KEY_DIM + h * GLA_DK:GLA_KEY_DIM + (h + 1) * GLA_DK]
            b = b_ref[rows, kcol]
            att = _gla_attention(qs, k, b, lf_ref[rows, kcol], b_ref, r0, kcol, c, c, mask_ref)
            att_ref[ci * GLA_HEADS + h] = att.astype(BF16)
            qb_ref[rows, kcol] = (qs * jnp.exp(b)).astype(BF16)
            khat = k * jnp.exp(_block_rows(b_ref, r0, kcol, c, c, c - 1) - b)
            tile = jnp.concatenate(
                [khat, jnp.exp(b[c - SUBLANE:c, :]), jnp.zeros((LANE - c - SUBLANE, GLA_DK), F32)], axis=0)
            tt_ref[ci * GLA_HEADS + h] = jnp.transpose(tile)

    for ci, r0 in enumerate(range(0, GLA_TBLOCK, c)):
        rows = slice(r0, r0 + c)
        for h in range(GLA_HEADS):
            kcol = slice(h * GLA_DK, (h + 1) * GLA_DK)
            vcol = slice(h * GLA_DV, (h + 1) * GLA_DV)
            v = proj_ref[rows, 2 * GLA_KEY_DIM + h * GLA_DV:2 * GLA_KEY_DIM + (h + 1) * GLA_DV].astype(BF16)
            r = proj_ref[rows, 2 * GLA_KEY_DIM + GLA_VAL_DIM + h * GLA_DV:
                         2 * GLA_KEY_DIM + GLA_VAL_DIM + (h + 1) * GLA_DV]
            s = s_ref[h]
            tt = tt_ref[ci * GLA_HEADS + h]
            o = _dot(att_ref[ci * GLA_HEADS + h], v) + _dot(qb_ref[rows, kcol], s.astype(BF16))
            decay = tt[:, c + SUBLANE - 1:c + SUBLANE]
            s_ref[h] = s * decay + _dot(tt[:, 0:c].astype(BF16), v)
            og_ref[rows, vcol] = _gla_finish_head(o, r, gon_ref)
    out = _dot(og_ref[...], wout_ref[...])
    o_ref[0] = x + _rms(out, g_ref[1:2, :])

    @pl.when(tb == pl.num_programs(1) - 1)
    def _():
        snew_ref[0] = s_ref[...]


def _gla_sample_kernel(x_ref, g_ref, win_ref, wglr_ref, wg2_ref, bg_ref, gon_ref, wout_ref,
                       cum_ref, mask_ref, s0_ref, o_ref, snew_ref,
                       hn_ref, proj_ref, lf_ref, b_ref, og_ref, *, seq):
    rows = x_ref.shape[0]
    grp = 2 * SUBLANE
    x = x_ref[...]
    _gla_project(x, g_ref, win_ref, wglr_ref, wg2_ref, bg_ref, cum_ref, hn_ref, proj_ref, lf_ref, b_ref)
    for c0 in range(2 * GLA_KEY_DIM, proj_ref.shape[1], GLA_PROJ_CHUNK):
        _gla_project_cols(hn_ref, win_ref, proj_ref, c0)
    rid = lax.broadcasted_iota(jnp.int32, (grp, 1), 0)
    for h in range(GLA_HEADS):
        kcol = slice(h * GLA_DK, (h + 1) * GLA_DK)
        vcol = slice(h * GLA_DV, (h + 1) * GLA_DV)
        qs = proj_ref[:, kcol] * (GLA_DK ** -0.5)
        k = proj_ref[:, GLA_KEY_DIM + h * GLA_DK:GLA_KEY_DIM + (h + 1) * GLA_DK]
        v = proj_ref[:, 2 * GLA_KEY_DIM + h * GLA_DV:2 * GLA_KEY_DIM + (h + 1) * GLA_DV]
        r = proj_ref[:, 2 * GLA_KEY_DIM + GLA_VAL_DIM + h * GLA_DV:
                     2 * GLA_KEY_DIM + GLA_VAL_DIM + (h + 1) * GLA_DV]
        b = b_ref[:, kcol]
        att = _gla_attention(qs, k, b, lf_ref[:, kcol], b_ref, 0, kcol, rows, seq, mask_ref)
        eb = jnp.exp(b)
        qb = (qs * eb).astype(BF16)
        khat = k * jnp.exp(_block_rows(b_ref, 0, kcol, rows, seq, seq - 1) - b)
        o_intra = _dot(att.astype(BF16), v.astype(BF16))
        o_parts = []
        for gi in range(rows // grp):
            gr = slice(gi * grp, (gi + 1) * grp)
            tile = jnp.concatenate(
                [khat[gr], eb[gr], jnp.zeros((LANE - 2 * grp, GLA_DK), F32)], axis=0)
            tt = jnp.transpose(tile)
            kt = tt[:, 0:grp].astype(BF16)
            qg = qb[gr]
            vg = v[gr]
            o_g = jnp.zeros((grp, GLA_DV), F32)
            for si in range(grp // seq):
                sq = gi * (grp // seq) + si
                mine = (rid >= si * seq) & (rid < (si + 1) * seq)
                s = s0_ref[sq, h]
                o_g = o_g + jnp.where(mine, _dot(qg, s.astype(BF16)), 0.0)
                vm = jnp.where(mine, vg, 0.0).astype(BF16)
                last = grp + (si + 1) * seq - 1
                snew_ref[sq, h] = s * tt[:, last:last + 1] + _dot(kt, vm)
            o_parts.append(o_g)
        o = o_intra + jnp.concatenate(o_parts, axis=0)
        og_ref[:, vcol] = _gla_finish_head(o, r, gon_ref)
    out = _dot(og_ref[...], wout_ref[...])
    o_ref[...] = x + _rms(out, g_ref[1:2, :])


def _gla_weight_specs():
    zero = (lambda *_: (0, 0))
    resident = dict(pipeline_mode=pl.Buffered(1))
    return [
        pl.BlockSpec((2, D_MODEL), zero),
        pl.BlockSpec((D_MODEL, GLA_IN_DIM), zero, **resident),
        pl.BlockSpec((D_MODEL, LANE), zero, **resident),
        pl.BlockSpec((LANE, GLA_KEY_DIM), zero, **resident),
        pl.BlockSpec((1, GLA_KEY_DIM), zero),
        pl.BlockSpec((1, GLA_DV), zero),
        pl.BlockSpec((GLA_VAL_DIM, D_MODEL), zero, **resident),
    ]


def _gla_weights(w_in, w_g2, b_g, g_onorm, w_out):
    w_glr = jnp.pad(w_in[:, GLA_MAIN_DIM:], ((0, 0), (0, LANE - GATE_RANK))).astype(BF16)
    w_g2p = jnp.pad(w_g2, ((0, LANE - GATE_RANK), (0, 0))).astype(BF16)
    return (w_in.astype(BF16), w_glr, w_g2p, b_g.reshape(1, GLA_KEY_DIM),
            g_onorm.reshape(1, GLA_DV), w_out.astype(BF16))


def _gla_prompt(x, g2, weights):
    bsz, t, _ = x.shape
    cum, _ = _gla_tables(GLA_CUM_ROWS, GLA_CHUNK)
    _, masks = _gla_tables(GLA_CHUNK, GLA_CHUNK)
    n_main = GLA_MAIN_DIM
    const2 = lambda b, i: (0, 0)
    return pl.pallas_call(
        _gla_prompt_kernel,
        grid=(bsz, t // GLA_TBLOCK),
        in_specs=[pl.BlockSpec((1, GLA_TBLOCK, D_MODEL), lambda b, i: (b, i, 0))]
        + _gla_weight_specs()
        + [pl.BlockSpec(cum.shape, const2),
           pl.BlockSpec(masks.shape, lambda b, i: (0, 0, 0))],
        out_specs=[
            pl.BlockSpec((1, GLA_TBLOCK, D_MODEL), lambda b, i: (b, i, 0)),
            pl.BlockSpec((1, GLA_HEADS, GLA_DK, GLA_DV), lambda b, i: (b, 0, 0, 0)),
        ],
        out_shape=[
            jax.ShapeDtypeStruct(x.shape, F32),
            jax.ShapeDtypeStruct((bsz, GLA_HEADS, GLA_DK, GLA_DV), F32),
        ],
        scratch_shapes=[
            pltpu.VMEM((GLA_TBLOCK, D_MODEL), BF16),
            pltpu.VMEM((GLA_TBLOCK, n_main), F32),
            pltpu.VMEM((GLA_TBLOCK, GLA_KEY_DIM), F32),
            pltpu.VMEM((GLA_TBLOCK, GLA_KEY_DIM), F32),
            pltpu.VMEM((GLA_TBLOCK, GLA_VAL_DIM), BF16),
            pltpu.VMEM((GLA_HEADS, GLA_DK, GLA_DV), F32),
            pltpu.VMEM((GLA_TBLOCK // GLA_CHUNK * GLA_HEADS, GLA_CHUNK, GLA_CHUNK), BF16),
            pltpu.VMEM((GLA_TBLOCK, GLA_KEY_DIM), BF16),
            pltpu.VMEM((GLA_TBLOCK // GLA_CHUNK * GLA_HEADS, LANE, GLA_DK), F32),
        ],
        compiler_params=pltpu.CompilerParams(
            dimension_semantics=("arbitrary", "arbitrary"), vmem_limit_bytes=VMEM_LIMIT),
        name="gla_prompt",
    )(x, g2, *weights, cum, masks)


def _gla_sample(x, s0, g2, weights):
    bsz, seq, _ = x.shape
    rows = GLA_SAMPLE_SEQS * seq
    cum, masks = _gla_tables(rows, seq)
    n_main = GLA_MAIN_DIM
    state_spec = pl.BlockSpec((GLA_SAMPLE_SEQS, GLA_HEADS, GLA_DK, GLA_DV), lambda i: (i, 0, 0, 0))
    y, snew = pl.pallas_call(
        functools.partial(_gla_sample_kernel, seq=seq),
        grid=(bsz // GLA_SAMPLE_SEQS,),
        in_specs=[pl.BlockSpec((rows, D_MODEL), lambda i: (i, 0))]
        + _gla_weight_specs()
        + [pl.BlockSpec(cum.shape, lambda i: (0, 0)),
           pl.BlockSpec(masks.shape, lambda i: (0, 0, 0)),
           state_spec],
        out_specs=[pl.BlockSpec((rows, D_MODEL), lambda i: (i, 0)), state_spec],
        out_shape=[
            jax.ShapeDtypeStruct((bsz * seq, D_MODEL), F32),
            jax.ShapeDtypeStruct(s0.shape, F32),
        ],
        scratch_shapes=[
            pltpu.VMEM((rows, D_MODEL), BF16),
            pltpu.VMEM((rows, n_main), F32),
            pltpu.VMEM((rows, GLA_KEY_DIM), F32),
            pltpu.VMEM((rows, GLA_KEY_DIM), F32),
            pltpu.VMEM((rows, GLA_VAL_DIM), BF16),
        ],
        compiler_params=pltpu.CompilerParams(
            dimension_semantics=("arbitrary",), vmem_limit_bytes=VMEM_LIMIT),
        name="gla_sample",
    )(x.reshape(bsz * seq, D_MODEL), g2, *weights, cum, masks, s0)
    return y.reshape(x.shape), snew


def _s5_discretize(lam_re, lam_im, log_dt):
    dt = jnp.exp(log_dt)
    mag = jnp.exp(lam_re * dt)
    ang = lam_im * dt
    a_re, a_im = mag * jnp.cos(ang), mag * jnp.sin(ang)
    nr, ni = a_re - 1.0, a_im
    den = lam_re * lam_re + lam_im * lam_im
    f_re = (nr * lam_re + ni * lam_im) / den
    f_im = (ni * lam_re - nr * lam_im) / den
    return a_re, a_im, f_re, f_im


def _s5_prep_kernel(lre_ref, lim_ref, ldt_ref, bre_ref, bim_ref, cre_ref, cim_ref,
                    lref_ref, limf_ref, ldtf_ref, wb_ref, wc_ref, are_ref, aim_ref):
    _, _, f_re, f_im = _s5_discretize(lre_ref[...], lim_ref[...], ldt_ref[...])

    def per_channel(f):
        return jnp.broadcast_to(f[:, None, :], (S5_SUPER, S5_GROUP, S5_SUPER_ST)).reshape(
            S5_SUPER_CH, S5_SUPER_ST)

    f_re, f_im = per_channel(f_re), per_channel(f_im)
    b_re, b_im = bre_ref[...], bim_ref[...]
    row = lax.broadcasted_iota(jnp.int32, (S5_SUPER_CH, S5_SUPER_ST), 0)
    col = lax.broadcasted_iota(jnp.int32, (S5_SUPER_CH, S5_SUPER_ST), 1)
    own = (row // S5_GROUP) == (col // S5_STATE)
    wb_ref[0, :, 0:S5_SUPER_ST] = jnp.where(own, f_re * b_re - f_im * b_im, 0.0).astype(BF16)
    wb_ref[0, :, S5_SUPER_ST:] = jnp.where(own, f_re * b_im + f_im * b_re, 0.0).astype(BF16)
    rowc = lax.broadcasted_iota(jnp.int32, (S5_SUPER_ST, S5_SUPER_CH), 0)
    colc = lax.broadcasted_iota(jnp.int32, (S5_SUPER_ST, S5_SUPER_CH), 1)
    ownc = (rowc // S5_STATE) == (colc // S5_GROUP)
    wc_ref[0, 0:S5_SUPER_ST, :] = jnp.where(ownc, cre_ref[...], 0.0).astype(BF16)
    wc_ref[0, S5_SUPER_ST:, :] = jnp.where(ownc, -cim_ref[...], 0.0).astype(BF16)
    a_re, a_im, _, _ = _s5_discretize(lref_ref[...], limf_ref[...], ldtf_ref[...])
    are_ref[...] = a_re
    aim_ref[...] = a_im


def _s5_prep(lam_re, lam_im, log_dt, b_re, b_im, c_re, c_im):
    g, p, ch, sg = S5_GROUPS, S5_STATE, S5_GROUP, S5_SUPER

    def rows_by_group(a_gp):
        return jnp.broadcast_to(a_gp[:, None, :], (g, sg, p)).reshape(g, sg * p)

    def b_layout(b):
        bt = jnp.transpose(b, (0, 2, 1))
        return jnp.broadcast_to(bt[:, :, None, :], (g, ch, sg, p)).reshape(g * ch, sg * p)

    def c_layout(cm):
        ct = jnp.transpose(cm.reshape(g // sg, sg, ch, p), (3, 0, 1, 2)).reshape(p, g * ch)
        return jnp.broadcast_to(ct[None], (sg, p, g * ch)).reshape(sg * p, g * ch)

    ldt_gp = jnp.broadcast_to(log_dt[:, None], (g, p))
    big = pl.BlockSpec((S5_SUPER_CH, S5_SUPER_ST), lambda j: (j, 0))
    per_group = pl.BlockSpec((S5_SUPER, S5_SUPER_ST), lambda j: (j, 0))
    cspec = pl.BlockSpec((S5_SUPER_ST, S5_SUPER_CH), lambda j: (0, j))
    flat = pl.BlockSpec((1, S5_SUPER_ST), lambda j: (0, j))
    return pl.pallas_call(
        _s5_prep_kernel,
        grid=(S5_NSUPER,),
        in_specs=[per_group] * 3 + [big] * 2 + [cspec] * 2 + [flat] * 3,
        out_specs=[
            pl.BlockSpec((1, S5_SUPER_CH, 2 * S5_SUPER_ST), lambda j: (j, 0, 0)),
            pl.BlockSpec((1, 2 * S5_SUPER_ST, S5_SUPER_CH), lambda j: (j, 0, 0)),
            flat, flat,
        ],
        out_shape=[
            jax.ShapeDtypeStruct((S5_NSUPER, S5_SUPER_CH, 2 * S5_SUPER_ST), BF16),
            jax.ShapeDtypeStruct((S5_NSUPER, 2 * S5_SUPER_ST, S5_SUPER_CH), BF16),
            jax.ShapeDtypeStruct((1, S5_FLAT), F32),
            jax.ShapeDtypeStruct((1, S5_FLAT), F32),
        ],
        compiler_params=pltpu.CompilerParams(dimension_semantics=("arbitrary",)),
        name="s5_prep",
    )(rows_by_group(lam_re), rows_by_group(lam_im), rows_by_group(ldt_gp),
      b_layout(b_re), b_layout(b_im), c_layout(c_re), c_layout(c_im),
      lam_re.reshape(1, S5_FLAT), lam_im.reshape(1, S5_FLAT), ldt_gp.reshape(1, S5_FLAT))


def _s5_kernel(x_ref, g_ref, wb_ref, wc_ref, are_ref, aim_ref, d_ref, wglu_ref, bglu_ref,
               h0re_ref, h0im_ref, perm_ref, unperm_ref, o_ref, hre_ref, him_ref,
               u_ref, sre_ref, sim_ref, z_ref, *, bsz, steps, sub):
    i = pl.program_id(0)
    rows = bsz * steps

    @pl.when(i == 0)
    def _():
        hre_ref[...] = h0re_ref[...]
        him_ref[...] = h0im_ref[...]

    def x_group(k):
        xk = x_ref[...] if sub == 1 else x_ref[:, k * steps:(k + 1) * steps, :]
        return xk.reshape(rows, D_MODEL)

    def region(k):
        r0 = (k % S5_REGIONS) * rows
        return slice(r0, r0 + rows)

    def input_side(k):
        grp = region(k)
        u = _dot(perm_ref[...], _rms(x_group(k), g_ref[0:1, :]).astype(BF16))
        u_ref[grp, :] = u
        ub = u.astype(BF16)
        for j in range(S5_NSUPER):
            bu = _dot(ub[:, j * S5_SUPER_CH:(j + 1) * S5_SUPER_CH], wb_ref[j])
            sre_ref[grp, j * S5_SUPER_ST:(j + 1) * S5_SUPER_ST] = bu[:, :S5_SUPER_ST]
            sim_ref[grp, j * S5_SUPER_ST:(j + 1) * S5_SUPER_ST] = bu[:, S5_SUPER_ST:]

    def scan_group(k):
        base = (k % S5_REGIONS) * rows
        for lc in range(S5_FLAT // S5_LANES):
            ls = slice(lc * S5_LANES, (lc + 1) * S5_LANES)
            a_re = jnp.broadcast_to(are_ref[:, ls], (SUBLANE, S5_LANES))
            a_im = jnp.broadcast_to(aim_ref[:, ls], (SUBLANE, S5_LANES))
            for rt in range(bsz // SUBLANE):
                rs = slice(rt * SUBLANE, (rt + 1) * SUBLANE)

                def step(t, carry, ls=ls, rt=rt, a_re=a_re, a_im=a_im):
                    h_re, h_im = carry
                    r8 = pl.ds(pl.multiple_of(base + t * bsz + rt * SUBLANE, SUBLANE), SUBLANE)
                    n_re = a_re * h_re - a_im * h_im + sre_ref[r8, ls]
                    n_im = a_re * h_im + a_im * h_re + sim_ref[r8, ls]
                    sre_ref[r8, ls] = n_re
                    sim_ref[r8, ls] = n_im
                    return n_re, n_im

                h_re, h_im = lax.fori_loop(0, steps, step, (hre_ref[rs, ls], him_ref[rs, ls]),
                                           unroll=True)
                hre_ref[rs, ls] = h_re
                him_ref[rs, ls] = h_im

    for k in range(min(S5_REGIONS, sub)):
        input_side(k)
    for k in range(sub):
        scan_group(k)
        grp = region(k)
        for j in range(S5_NSUPER):
            st = slice(j * S5_SUPER_ST, (j + 1) * S5_SUPER_ST)
            ch = slice(j * S5_SUPER_CH, (j + 1) * S5_SUPER_CH)
            y = (_dot(sre_ref[grp, st].astype(BF16), wc_ref[j, 0:S5_SUPER_ST, :])
                 + _dot(sim_ref[grp, st].astype(BF16), wc_ref[j, S5_SUPER_ST:, :]))
            z_ref[grp, ch] = (y + d_ref[:, ch] * u_ref[grp, ch]).astype(BF16)
        z = _dot(unperm_ref[...], z_ref[grp, :]).astype(BF16)
        zz = _dot(z, wglu_ref[...]) + bglu_ref[...]
        out = zz[:, :D_MODEL] * _sigmoid(zz[:, D_MODEL:])
        res = x_group(k) + _rms(out, g_ref[1:2, :])
        if sub == 1:
            o_ref[...] = res.reshape(o_ref.shape)
        else:
            o_ref[:, k * steps:(k + 1) * steps, :] = res.reshape(bsz, steps, D_MODEL)
        if k + S5_REGIONS < sub:
            input_side(k + S5_REGIONS)


def _s5(x, steps_per_block, g2, prep, d_skip, w_glu, b_glu, h0_re, h0_im):
    wb, wc, a_re, a_im = prep
    bsz, t, _ = x.shape
    rows = steps_per_block * bsz
    r = np.arange(rows)
    perm = np.zeros((rows, rows), np.float32)
    perm[(r % steps_per_block) * bsz + r // steps_per_block, r] = 1.0
    zero2 = lambda i: (0, 0)
    zero3 = lambda i: (0, 0, 0)
    resident = dict(pipeline_mode=pl.Buffered(1))
    state = pl.BlockSpec((bsz, S5_FLAT), zero2)
    if steps_per_block == t:
        sub = 1
        x = x.reshape(1, rows, D_MODEL)
        xspec = pl.BlockSpec((1, rows, D_MODEL), lambda i: (0, 0, 0))
    else:
        sub = S5_GROUPS_PER_STEP
        assert steps_per_block % SUBLANE == 0 and t % (sub * steps_per_block) == 0
        xspec = pl.BlockSpec((bsz, sub * steps_per_block, D_MODEL), lambda i: (0, i, 0))
    y, h_re, h_im = pl.pallas_call(
        functools.partial(_s5_kernel, bsz=bsz, steps=steps_per_block, sub=sub),
        grid=(t // (sub * steps_per_block),),
        in_specs=[
            xspec,
            pl.BlockSpec((2, D_MODEL), zero2),
            pl.BlockSpec(wb.shape, zero3, **resident),
            pl.BlockSpec(wc.shape, zero3, **resident),
            pl.BlockSpec((1, S5_FLAT), zero2),
            pl.BlockSpec((1, S5_FLAT), zero2),
            pl.BlockSpec((1, D_MODEL), zero2),
            pl.BlockSpec((D_MODEL, 2 * D_MODEL), zero2, **resident),
            pl.BlockSpec((1, 2 * D_MODEL), zero2),
            state, state,
            pl.BlockSpec((rows, rows), zero2),
            pl.BlockSpec((rows, rows), zero2),
        ],
        out_specs=[xspec, state, state],
        out_shape=[
            jax.ShapeDtypeStruct(x.shape, F32),
            jax.ShapeDtypeStruct((bsz, S5_FLAT), F32),
            jax.ShapeDtypeStruct((bsz, S5_FLAT), F32),
        ],
        scratch_shapes=[
            pltpu.VMEM((min(sub, S5_REGIONS) * rows, D_MODEL), F32),
            pltpu.VMEM((min(sub, S5_REGIONS) * rows, S5_FLAT), F32),
            pltpu.VMEM((min(sub, S5_REGIONS) * rows, S5_FLAT), F32),
            pltpu.VMEM((min(sub, S5_REGIONS) * rows, D_MODEL), BF16),
        ],
        compiler_params=pltpu.CompilerParams(
            dimension_semantics=("arbitrary",), vmem_limit_bytes=VMEM_LIMIT),
        name="s5",
    )(x, g2, wb, wc, a_re, a_im, d_skip.reshape(1, D_MODEL), w_glu.astype(BF16),
      b_glu.reshape(1, 2 * D_MODEL), h0_re, h0_im,
      jnp.asarray(perm, BF16), jnp.asarray(perm.T, BF16))
    return y.reshape(bsz, t, D_MODEL), h_re, h_im


def kernel(x_prompt, x_sample, state_gla, state_s5_re, state_s5_im, norm_g, w_ffn_gu, w_ffn_down,
           gla_w_in, gla_w_g2, gla_b_g, gla_g_onorm, gla_w_out,
           s5_lam_re, s5_lam_im, s5_log_dt, s5_b_re, s5_b_im, s5_c_re, s5_c_im, s5_d, s5_w_glu,
           s5_b_glu):
    pb = x_prompt.shape[0]
    sb, st, _ = x_sample.shape

    def ffn(xp, xs, layer, which):
        g2 = norm_g[layer, 4 * which:4 * which + 2]
        xs, wgu_bf, wd_bf = _ffn_stream(xs, g2, w_ffn_gu, w_ffn_down, layer, which)
        return _ffn_resident(xp, g2, wgu_bf, wd_bf), xs

    gla_w = _gla_weights(gla_w_in[0], gla_w_g2[0], gla_b_g[0], gla_g_onorm[0], gla_w_out[0])
    xp, xs = ffn(x_prompt, x_sample, 0, 0)
    xp, gla_p = _gla_prompt(xp, norm_g[0, 2:4], gla_w)
    xs, gla_s = _gla_sample(xs, state_gla.reshape(sb, GLA_HEADS, GLA_DK, GLA_DV), norm_g[0, 2:4], gla_w)
    xp, xs = ffn(xp, xs, 0, 1)

    xp, xs = ffn(xp, xs, 1, 0)
    prep = _s5_prep(s5_lam_re[0], s5_lam_im[0], s5_log_dt[0], s5_b_re[0], s5_b_im[0],
                    s5_c_re[0], s5_c_im[0])
    zeros = jnp.zeros((pb, S5_FLAT), F32)
    s5_args = (norm_g[1, 2:4], prep, s5_d[0], s5_w_glu[0], s5_b_glu[0])
    xp, hre_p, him_p = _s5(xp, S5_STEPS, *s5_args, zeros, zeros)
    xs, hre_s, him_s = _s5(xs, st, *s5_args,
                           state_s5_re.reshape(sb, S5_FLAT), state_s5_im.reshape(sb, S5_FLAT))
    y_prompt, y_sample = ffn(xp, xs, 1, 1)

    def s5_state(h, b):
        return h.reshape(1, b, S5_GROUPS, S5_STATE)

    return (y_prompt, y_sample, gla_p.reshape(1, pb, GLA_HEADS, GLA_DK, GLA_DV),
            s5_state(hre_p, pb), s5_state(him_p, pb),
            gla_s.reshape(1, sb, GLA_HEADS, GLA_DK, GLA_DV),
            s5_state(hre_s, sb), s5_state(him_s, sb))
```

```python
import functools
import math

import jax
import jax.numpy as jnp
import numpy as np
from jax import lax
from jax.experimental import pallas as pl
from jax.experimental.pallas import tpu as pltpu

F32 = jnp.float32
BF16 = jnp.bfloat16

D_MODEL = 1024
D_FF = 2816
GLA_HEADS = 4
GLA_DK = 128
GLA_DV = 256
GLA_KEY_DIM = GLA_HEADS * GLA_DK
GLA_VAL_DIM = GLA_HEADS * GLA_DV
GATE_RANK = 16
GLA_MAIN_DIM = 2 * GLA_KEY_DIM + 2 * GLA_VAL_DIM
GLA_IN_DIM = GLA_MAIN_DIM + GATE_RANK
GATE_TAU = 16.0
S5_GROUP = 16
S5_GROUPS = 64
S5_STATE = 64
S5_FLAT = S5_GROUPS * S5_STATE
S5_SUPER = 8
S5_NSUPER = S5_GROUPS // S5_SUPER
S5_SUPER_CH = S5_SUPER * S5_GROUP
S5_SUPER_ST = S5_SUPER * S5_STATE
EPS = 1e-6
LANE = 128
SUBLANE = 8
VMEM_LIMIT = 56 * 1024 * 1024

FFN_ROWS = 1024
FFN_GROUP = 512
FFN_CHUNK = 256
FFN_SLOTS = 4
GLA_CHUNK = 64
GLA_TBLOCK = 512
GLA_PROJ_CHUNK = 256
GLA_CUM_ROWS = 256
LOG2E = math.log2(math.e)
GLA_SAMPLE_SEQS = 16
S5_LANES = 512
S5_STEPS = 32
S5_GROUPS_PER_STEP = 4
S5_REGIONS = 2


def _dot(a, b):
    return jnp.dot(a, b, preferred_element_type=F32)


def _dot_nt(a, b):
    return lax.dot_general(a, b, (((1,), (1,)), ((), ())), preferred_element_type=F32)


def _rms(x, g):
    ms = jnp.mean(x * x, axis=-1, keepdims=True)
    return x * lax.rsqrt(ms + EPS) * g


def _sigmoid(x):
    return 1.0 / (1.0 + jnp.exp(-x))


def _split_bf16(x):
    hi = x.astype(BF16)
    lo = (x - hi.astype(F32)).astype(BF16)
    return hi, lo


def _ffn_row_groups(n_rows):
    return [slice(r0, min(r0 + FFN_GROUP, n_rows)) for r0 in range(0, n_rows, FFN_GROUP)]


def _ffn_hidden_chunk(xn_ref, wgu_ref, act_ref, rows, c):
    lo = c * FFN_CHUNK
    gate = _dot(xn_ref[rows, :], wgu_ref[:, lo:lo + FFN_CHUNK])
    up = _dot(xn_ref[rows, :], wgu_ref[:, D_FF + lo:D_FF + lo + FFN_CHUNK])
    act_ref[rows, lo:lo + FFN_CHUNK] = (gate * _sigmoid(gate) * up).astype(BF16)


def _ffn_finish(x_ref, g_ref, wd_ref, o_ref, act_ref, rows):
    y = _dot(act_ref[rows, :], wd_ref[...])
    o_ref[rows, :] = x_ref[rows, :] + 0.5 * _rms(y, g_ref[1:2, :])


def _ffn_stream_kernel(x_ref, g_ref, wgu_hbm, wd_hbm, o_ref, wgu_ref, wd_ref,
                       xn_ref, act_ref, gu_stage, d_stage, sem, *, layer, which):
    n_chunks = D_FF // FFN_CHUNK
    groups = _ffn_row_groups(x_ref.shape[0])
    xn_ref[...] = _rms(x_ref[...], g_ref[0:1, :]).astype(BF16)

    def chunk_slices(c):
        lo = c * FFN_CHUNK
        return ((slice(None), pl.ds(lo, FFN_CHUNK)), (slice(None), pl.ds(D_FF + lo, FFN_CHUNK)),
                (pl.ds(lo, FFN_CHUNK), slice(None)))

    def fetches(c):
        slot = c % FFN_SLOTS
        gate, up, down = chunk_slices(c)
        return (
            pltpu.make_async_copy(wgu_hbm.at[(layer, which) + gate], gu_stage.at[slot, 0], sem.at[slot, 0]),
            pltpu.make_async_copy(wgu_hbm.at[(layer, which) + up], gu_stage.at[slot, 1], sem.at[slot, 1]),
            pltpu.make_async_copy(wd_hbm.at[(layer, which) + down], d_stage.at[slot], sem.at[slot, 2]),
        )

    def start_fetches(c):
        for n, cp in enumerate(fetches(c)):
            cp.start(priority=(c + n) % 2)

    ahead = FFN_SLOTS - 1
    for c in range(min(ahead, n_chunks)):
        start_fetches(c)
    for c in range(n_chunks):
        slot, lo = c % FFN_SLOTS, c * FFN_CHUNK
        for cp in fetches(c):
            cp.wait()
        if c + ahead < n_chunks:
            start_fetches(c + ahead)
        wgu_ref[:, lo:lo + FFN_CHUNK] = gu_stage[slot, 0].astype(BF16)
        wgu_ref[:, D_FF + lo:D_FF + lo + FFN_CHUNK] = gu_stage[slot, 1].astype(BF16)
        wd_ref[lo:lo + FFN_CHUNK, :] = d_stage[slot].astype(BF16)
        for rows in groups:
            _ffn_hidden_chunk(xn_ref, wgu_ref, act_ref, rows, c)
    for rows in groups:
        _ffn_finish(x_ref, g_ref, wd_ref, o_ref, act_ref, rows)


def _ffn_resident_kernel(x_ref, g_ref, wgu_ref, wd_ref, o_ref, xn_ref, act_ref):
    xn_ref[...] = _rms(x_ref[...], g_ref[0:1, :]).astype(BF16)
    for rows in _ffn_row_groups(x_ref.shape[0]):
        for c in range(D_FF // FFN_CHUNK):
            _ffn_hidden_chunk(xn_ref, wgu_ref, act_ref, rows, c)
        _ffn_finish(x_ref, g_ref, wd_ref, o_ref, act_ref, rows)


def _ffn_stream(x, g2, wgu, wd, layer, which):
    shape = x.shape
    x = x.reshape(-1, D_MODEL)
    n = x.shape[0]
    zero = lambda i: (0, 0)
    y, wgu_bf, wd_bf = pl.pallas_call(
        functools.partial(_ffn_stream_kernel, layer=layer, which=which),
        grid=(1,),
        in_specs=[
            pl.BlockSpec((n, D_MODEL), zero),
            pl.BlockSpec((2, D_MODEL), zero),
            pl.BlockSpec(memory_space=pl.ANY),
            pl.BlockSpec(memory_space=pl.ANY),
        ],
        out_specs=[
            pl.BlockSpec((n, D_MODEL), zero),
            pl.BlockSpec((D_MODEL, 2 * D_FF), zero, pipeline_mode=pl.Buffered(1)),
            pl.BlockSpec((D_FF, D_MODEL), zero, pipeline_mode=pl.Buffered(1)),
        ],
        out_shape=[
            jax.ShapeDtypeStruct((n, D_MODEL), F32),
            jax.ShapeDtypeStruct((D_MODEL, 2 * D_FF), BF16),
            jax.ShapeDtypeStruct((D_FF, D_MODEL), BF16),
        ],
        scratch_shapes=[
            pltpu.VMEM((n, D_MODEL), BF16),
            pltpu.VMEM((n, D_FF), BF16),
            pltpu.VMEM((FFN_SLOTS, 2, D_MODEL, FFN_CHUNK), F32),
            pltpu.VMEM((FFN_SLOTS, FFN_CHUNK, D_MODEL), F32),
            pltpu.SemaphoreType.DMA((FFN_SLOTS, 3)),
        ],
        compiler_params=pltpu.CompilerParams(
            dimension_semantics=("arbitrary",), vmem_limit_bytes=VMEM_LIMIT),
        name="ffn_stream",
    )(x, g2, wgu, wd)
    return y.reshape(shape), wgu_bf, wd_bf


def _ffn_resident(x, g2, wgu_bf, wd_bf):
    shape = x.shape
    x = x.reshape(-1, D_MODEL)
    n = x.shape[0]
    tm = min(FFN_ROWS, n)
    assert n % tm == 0
    zero = lambda i: (0, 0)
    resident = dict(pipeline_mode=pl.Buffered(1))
    return pl.pallas_call(
        _ffn_resident_kernel,
        grid=(n // tm,),
        in_specs=[
            pl.BlockSpec((tm, D_MODEL), lambda i: (i, 0)),
            pl.BlockSpec((2, D_MODEL), zero),
            pl.BlockSpec((D_MODEL, 2 * D_FF), zero, **resident),
            pl.BlockSpec((D_FF, D_MODEL), zero, **resident),
        ],
        out_specs=pl.BlockSpec((tm, D_MODEL), lambda i: (i, 0)),
        out_shape=jax.ShapeDtypeStruct((n, D_MODEL), F32),
        scratch_shapes=[pltpu.VMEM((tm, D_MODEL), BF16), pltpu.VMEM((tm, D_FF), BF16)],
        compiler_params=pltpu.CompilerParams(
            dimension_semantics=("arbitrary",), vmem_limit_bytes=VMEM_LIMIT),
        name="ffn",
    )(x, g2, wgu_bf, wd_bf).reshape(shape)


def _gla_tables(rows, seq):
    t = np.arange(rows)[:, None]
    r = np.arange(rows)[None, :]
    cum = ((t // seq) == (r // seq)) & (r <= t)
    masks = [t == r]
    m = seq // 2
    while m >= 1:
        masks.append(((t // (2 * m)) == (r // (2 * m))) & (t % (2 * m) >= m) & (r % (2 * m) < m))
        m //= 2
    return jnp.asarray(cum.astype(np.float32), BF16), jnp.asarray(np.stack(masks).astype(np.float32))


def _gla_project_cols(hn_ref, win_ref, proj_ref, c0):
    proj_ref[:, c0:c0 + GLA_PROJ_CHUNK] = _dot(hn_ref[...], win_ref[:, c0:c0 + GLA_PROJ_CHUNK])


def _gla_project(x, g_ref, win_ref, wglr_ref, wg2_ref, bg_ref, cum_ref, hn_ref, proj_ref, lf_ref, b_ref):
    hn = _rms(x, g_ref[0:1, :]).astype(BF16)
    hn_ref[...] = hn
    glr = _dot(hn, wglr_ref[...]).astype(BF16)
    gate = _dot(glr, wg2_ref[...]) + bg_ref[...]
    for c0 in range(0, 2 * GLA_KEY_DIM, GLA_PROJ_CHUNK):
        _gla_project_cols(hn_ref, win_ref, proj_ref, c0)
    lf = (jnp.minimum(gate, 0.0) - jnp.log1p(jnp.exp(-jnp.abs(gate)))) * (1.0 / GATE_TAU)
    lf_ref[...] = lf
    hi, lo = _split_bf16(lf)
    span = cum_ref.shape[0]
    for r0 in range(0, lf.shape[0], span):
        b_ref[r0:r0 + span, :] = (_dot(cum_ref[...], hi[r0:r0 + span]) + _dot(cum_ref[...], lo[r0:r0 + span]))


def _block_rows(ref, r0, col, rows, period, offset):
    def bc(row, n):
        tile = ref[pl.ds(r0 + row // SUBLANE * SUBLANE, SUBLANE), col]
        return jnp.broadcast_to(tile[row % SUBLANE:row % SUBLANE + 1, :], (n, GLA_DK))

    if period >= SUBLANE:
        parts = [bc(p0 + offset, period) for p0 in range(0, rows, period)]
    else:
        assert 2 * period == SUBLANE
        low = lax.broadcasted_iota(jnp.int32, (SUBLANE, 1), 0) < period
        parts = [jnp.where(low, bc(t0 + offset, SUBLANE), bc(t0 + period + offset, SUBLANE))
                 for t0 in range(0, rows, SUBLANE)]
    return parts[0] if len(parts) == 1 else jnp.concatenate(parts, axis=0)


def _gla_attention(qs, k, b, lf, b_ref, r0, col, rows, seq, mask_ref):
    qb, kb = qs.astype(BF16), k.astype(BF16)
    att = _dot_nt(qb, kb) * mask_ref[0]
    lvl, m = 1, seq // 2
    while m >= 1:
        if m > 1:
            d = b - _block_rows(b_ref, r0, col, rows, 2 * m, m - 1)
            w = jnp.exp2(jnp.abs(d) * (-LOG2E))
        else:
            odd = lax.broadcasted_iota(jnp.int32, (rows, 1), 0) % 2 == 1
            w = jnp.where(odd, jnp.exp(lf), 1.0)
        wb = w.astype(BF16)
        att = att + _dot_nt(qb * wb, kb * wb) * mask_ref[lvl]
        lvl, m = lvl + 1, m // 2
    return att


def _gla_finish_head(o, r, gon_ref):
    on = _rms(o, gon_ref[...])
    return (on * (r * _sigmoid(r))).astype(BF16)


def _gla_prompt_kernel(x_ref, g_ref, win_ref, wglr_ref, wg2_ref, bg_ref, gon_ref, wout_ref,
                       cum_ref, mask_ref, o_ref, snew_ref,
                       hn_ref, proj_ref, lf_ref, b_ref, og_ref, s_ref, att_ref, qb_ref, tt_ref):
    tb = pl.program_id(1)
    c = GLA_CHUNK

    @pl.when(tb == 0)
    def _():
        s_ref[...] = jnp.zeros_like(s_ref)

    x = x_ref[0]
    _gla_project(x, g_ref, win_ref, wglr_ref, wg2_ref, bg_ref, cum_ref, hn_ref, proj_ref, lf_ref, b_ref)

    late_cols = list(range(2 * GLA_KEY_DIM, proj_ref.shape[1], GLA_PROJ_CHUNK))
    n_heads_total = GLA_TBLOCK // c * GLA_HEADS
    for ci, r0 in enumerate(range(0, GLA_TBLOCK, c)):
        rows = slice(r0, r0 + c)
        for h in range(GLA_HEADS):
            done = ci * GLA_HEADS + h
            for c0 in late_cols[done * len(late_cols) // n_heads_total:
                                (done + 1) * len(late_cols) // n_heads_total]:
                _gla_project_cols(hn_ref, win_ref, proj_ref, c0)
            kcol = slice(h * GLA_DK, (h + 1) * GLA_DK)
            qs = proj_ref[rows, kcol] * (GLA_DK ** -0.5)
            k = proj_ref[rows, GLA_KEY_DIM + h * GLA_DK:GLA_KEY_DIM + (h + 1) * GLA_DK]
            b = b_ref[rows, kcol]
            att = _gla_attention(qs, k, b, lf_ref[rows, kcol], b_ref, r0, kcol, c, c, mask_ref)
            att_ref[ci * GLA_HEADS + h] = att.astype(BF16)
            qb_ref[rows, kcol] = (qs * jnp.exp(b)).astype(BF16)
            khat = k * jnp.exp(_block_rows(b_ref, r0, kcol, c, c, c - 1) - b)
            tile = jnp.concatenate(
                [khat, jnp.exp(b[c - SUBLANE:c, :]), jnp.zeros((LANE - c - SUBLANE, GLA_DK), F32)], axis=0)
            tt_ref[ci * GLA_HEADS + h] = jnp.transpose(tile)

    for ci, r0 in enumerate(range(0, GLA_TBLOCK, c)):
        rows = slice(r0, r0 + c)
        for h in range(GLA_HEADS):
            kcol = slice(h * GLA_DK, (h + 1) * GLA_DK)
            vcol = slice(h * GLA_DV, (h + 1) * GLA_DV)
            v = proj_ref[rows, 2 * GLA_KEY_DIM + h * GLA_DV:2 * GLA_KEY_DIM + (h + 1) * GLA_DV].astype(BF16)
            r = proj_ref[rows, 2 * GLA_KEY_DIM + GLA_VAL_DIM + h * GLA_DV:
                         2 * GLA_KEY_DIM + GLA_VAL_DIM + (h + 1) * GLA_DV]
            s = s_ref[h]
            tt = tt_ref[ci * GLA_HEADS + h]
            o = _dot(att_ref[ci * GLA_HEADS + h], v) + _dot(qb_ref[rows, kcol], s.astype(BF16))
            decay = tt[:, c + SUBLANE - 1:c + SUBLANE]
            s_ref[h] = s * decay + _dot(tt[:, 0:c].astype(BF16), v)
            og_ref[rows, vcol] = _gla_finish_head(o, r, gon_ref)
    out = _dot(og_ref[...], wout_ref[...])
    o_ref[0] = x + _rms(out, g_ref[1:2, :])

    @pl.when(tb == pl.num_programs(1) - 1)
    def _():
        snew_ref[0] = s_ref[...]


def _gla_sample_kernel(x_ref, g_ref, win_ref, wglr_ref, wg2_ref, bg_ref, gon_ref, wout_ref,
                       cum_ref, mask_ref, s0_ref, o_ref, snew_ref,
                       hn_ref, proj_ref, lf_ref, b_ref, og_ref, *, seq):
    rows = x_ref.shape[0]
    grp = 2 * SUBLANE
    x = x_ref[...]
    _gla_project(x, g_ref, win_ref, wglr_ref, wg2_ref, bg_ref, cum_ref, hn_ref, proj_ref, lf_ref, b_ref)
    for c0 in range(2 * GLA_KEY_DIM, proj_ref.shape[1], GLA_PROJ_CHUNK):
        _gla_project_cols(hn_ref, win_ref, proj_ref, c0)
    rid = lax.broadcasted_iota(jnp.int32, (grp, 1), 0)
    for h in range(GLA_HEADS):
        kcol = slice(h * GLA_DK, (h + 1) * GLA_DK)
        vcol = slice(h * GLA_DV, (h + 1) * GLA_DV)
        qs = proj_ref[:, kcol] * (GLA_DK ** -0.5)
        k = proj_ref[:, GLA_KEY_DIM + h * GLA_DK:GLA_KEY_DIM + (h + 1) * GLA_DK]
        v = proj_ref[:, 2 * GLA_KEY_DIM + h * GLA_DV:2 * GLA_KEY_DIM + (h + 1) * GLA_DV]
        r = proj_ref[:, 2 * GLA_KEY_DIM + GLA_VAL_DIM + h * GLA_DV:
                     2 * GLA_KEY_DIM + GLA_VAL_DIM + (h + 1) * GLA_DV]
        b = b_ref[:, kcol]
        att = _gla_attention(qs, k, b, lf_ref[:, kcol], b_ref, 0, kcol, rows, seq, mask_ref)
        eb = jnp.exp(b)
        qb = (qs * eb).astype(BF16)
        khat = k * jnp.exp(_block_rows(b_ref, 0, kcol, rows, seq, seq - 1) - b)
        o_intra = _dot(att.astype(BF16), v.astype(BF16))
        o_parts = []
        for gi in range(rows // grp):
            gr = slice(gi * grp, (gi + 1) * grp)
            tile = jnp.concatenate(
                [khat[gr], eb[gr], jnp.zeros((LANE - 2 * grp, GLA_DK), F32)], axis=0)
            tt = jnp.transpose(tile)
            kt = tt[:, 0:grp].astype(BF16)
            qg = qb[gr]
            vg = v[gr]
            o_g = jnp.zeros((grp, GLA_DV), F32)
            for si in range(grp // seq):
                sq = gi * (grp // seq) + si
                mine = (rid >= si * seq) & (rid < (si + 1) * seq)
                s = s0_ref[sq, h]
                o_g = o_g + jnp.where(mine, _dot(qg, s.astype(BF16)), 0.0)
                vm = jnp.where(mine, vg, 0.0).astype(BF16)
                last = grp + (si + 1) * seq - 1
                snew_ref[sq, h] = s * tt[:, last:last + 1] + _dot(kt, vm)
            o_parts.append(o_g)
        o = o_intra + jnp.concatenate(o_parts, axis=0)
        og_ref[:, vcol] = _gla_finish_head(o, r, gon_ref)
    out = _dot(og_ref[...], wout_ref[...])
    o_ref[...] = x + _rms(out, g_ref[1:2, :])


def _gla_weight_specs():
    zero = (lambda *_: (0, 0))
    resident = dict(pipeline_mode=pl.Buffered(1))
    return [
        pl.BlockSpec((2, D_MODEL), zero),
        pl.BlockSpec((D_MODEL, GLA_IN_DIM), zero, **resident),
        pl.BlockSpec((D_MODEL, LANE), zero, **resident),
        pl.BlockSpec((LANE, GLA_KEY_DIM), zero, **resident),
        pl.BlockSpec((1, GLA_KEY_DIM), zero),
        pl.BlockSpec((1, GLA_DV), zero),
        pl.BlockSpec((GLA_VAL_DIM, D_MODEL), zero, **resident),
    ]


def _gla_weights(w_in, w_g2, b_g, g_onorm, w_out):
    w_glr = jnp.pad(w_in[:, GLA_MAIN_DIM:], ((0, 0), (0, LANE - GATE_RANK))).astype(BF16)
    w_g2p = jnp.pad(w_g2, ((0, LANE - GATE_RANK), (0, 0))).astype(BF16)
    return (w_in.astype(BF16), w_glr, w_g2p, b_g.reshape(1, GLA_KEY_DIM),
            g_onorm.reshape(1, GLA_DV), w_out.astype(BF16))


def _gla_prompt(x, g2, weights):
    bsz, t, _ = x.shape
    cum, _ = _gla_tables(GLA_CUM_ROWS, GLA_CHUNK)
    _, masks = _gla_tables(GLA_CHUNK, GLA_CHUNK)
    n_main = GLA_MAIN_DIM
    const2 = lambda b, i: (0, 0)
    return pl.pallas_call(
        _gla_prompt_kernel,
        grid=(bsz, t // GLA_TBLOCK),
        in_specs=[pl.BlockSpec((1, GLA_TBLOCK, D_MODEL), lambda b, i: (b, i, 0))]
        + _gla_weight_specs()
        + [pl.BlockSpec(cum.shape, const2),
           pl.BlockSpec(masks.shape, lambda b, i: (0, 0, 0))],
        out_specs=[
            pl.BlockSpec((1, GLA_TBLOCK, D_MODEL), lambda b, i: (b, i, 0)),
            pl.BlockSpec((1, GLA_HEADS, GLA_DK, GLA_DV), lambda b, i: (b, 0, 0, 0)),
        ],
        out_shape=[
            jax.ShapeDtypeStruct(x.shape, F32),
            jax.ShapeDtypeStruct((bsz, GLA_HEADS, GLA_DK, GLA_DV), F32),
        ],
        scratch_shapes=[
            pltpu.VMEM((GLA_TBLOCK, D_MODEL), BF16),
            pltpu.VMEM((GLA_TBLOCK, n_main), F32),
            pltpu.VMEM((GLA_TBLOCK, GLA_KEY_DIM), F32),
            pltpu.VMEM((GLA_TBLOCK, GLA_KEY_DIM), F32),
            pltpu.VMEM((GLA_TBLOCK, GLA_VAL_DIM), BF16),
            pltpu.VMEM((GLA_HEADS, GLA_DK, GLA_DV), F32),
            pltpu.VMEM((GLA_TBLOCK // GLA_CHUNK * GLA_HEADS, GLA_CHUNK, GLA_CHUNK), BF16),
            pltpu.VMEM((GLA_TBLOCK, GLA_KEY_DIM), BF16),
            pltpu.VMEM((GLA_TBLOCK // GLA_CHUNK * GLA_HEADS, LANE, GLA_DK), F32),
        ],
        compiler_params=pltpu.CompilerParams(
            dimension_semantics=("arbitrary", "arbitrary"), vmem_limit_bytes=VMEM_LIMIT),
        name="gla_prompt",
    )(x, g2, *weights, cum, masks)


def _gla_sample(x, s0, g2, weights):
    bsz, seq, _ = x.shape
    rows = GLA_SAMPLE_SEQS * seq
    cum, masks = _gla_tables(rows, seq)
    n_main = GLA_MAIN_DIM
    state_spec = pl.BlockSpec((GLA_SAMPLE_SEQS, GLA_HEADS, GLA_DK, GLA_DV), lambda i: (i, 0, 0, 0))
    y, snew = pl.pallas_call(
        functools.partial(_gla_sample_kernel, seq=seq),
        grid=(bsz // GLA_SAMPLE_SEQS,),
        in_specs=[pl.BlockSpec((rows, D_MODEL), lambda i: (i, 0))]
        + _gla_weight_specs()
        + [pl.BlockSpec(cum.shape, lambda i: (0, 0)),
           pl.BlockSpec(masks.shape, lambda i: (0, 0, 0)),
           state_spec],
        out_specs=[pl.BlockSpec((rows, D_MODEL), lambda i: (i, 0)), state_spec],
        out_shape=[
            jax.ShapeDtypeStruct((bsz * seq, D_MODEL), F32),
            jax.ShapeDtypeStruct(s0.shape, F32),
        ],
        scratch_shapes=[
            pltpu.VMEM((rows, D_MODEL), BF16),
            pltpu.VMEM((rows, n_main), F32),
            pltpu.VMEM((rows, GLA_KEY_DIM), F32),
            pltpu.VMEM((rows, GLA_KEY_DIM), F32),
            pltpu.VMEM((rows, GLA_VAL_DIM), BF16),
        ],
        compiler_params=pltpu.CompilerParams(
            dimension_semantics=("arbitrary",), vmem_limit_bytes=VMEM_LIMIT),
        name="gla_sample",
    )(x.reshape(bsz * seq, D_MODEL), g2, *weights, cum, masks, s0)
    return y.reshape(x.shape), snew


def _s5_discretize(lam_re, lam_im, log_dt):
    dt = jnp.exp(log_dt)
    mag = jnp.exp(lam_re * dt)
    ang = lam_im * dt
    a_re, a_im = mag * jnp.cos(ang), mag * jnp.sin(ang)
    nr, ni = a_re - 1.0, a_im
    den = lam_re * lam_re + lam_im * lam_im
    f_re = (nr * lam_re + ni * lam_im) / den
    f_im = (ni * lam_re - nr * lam_im) / den
    return a_re, a_im, f_re, f_im


def _s5_prep_kernel(lre_ref, lim_ref, ldt_ref, bre_ref, bim_ref, cre_ref, cim_ref,
                    lref_ref, limf_ref, ldtf_ref, wb_ref, wc_ref, are_ref, aim_ref):
    _, _, f_re, f_im = _s5_discretize(lre_ref[...], lim_ref[...], ldt_ref[...])

    def per_channel(f):
        return jnp.broadcast_to(f[:, None, :], (S5_SUPER, S5_GROUP, S5_SUPER_ST)).reshape(
            S5_SUPER_CH, S5_SUPER_ST)

    f_re, f_im = per_channel(f_re), per_channel(f_im)
    b_re, b_im = bre_ref[...], bim_ref[...]
    row = lax.broadcasted_iota(jnp.int32, (S5_SUPER_CH, S5_SUPER_ST), 0)
    col = lax.broadcasted_iota(jnp.int32, (S5_SUPER_CH, S5_SUPER_ST), 1)
    own = (row // S5_GROUP) == (col // S5_STATE)
    wb_ref[0, :, 0:S5_SUPER_ST] = jnp.where(own, f_re * b_re - f_im * b_im, 0.0).astype(BF16)
    wb_ref[0, :, S5_SUPER_ST:] = jnp.where(own, f_re * b_im + f_im * b_re, 0.0).astype(BF16)
    rowc = lax.broadcasted_iota(jnp.int32, (S5_SUPER_ST, S5_SUPER_CH), 0)
    colc = lax.broadcasted_iota(jnp.int32, (S5_SUPER_ST, S5_SUPER_CH), 1)
    ownc = (rowc // S5_STATE) == (colc // S5_GROUP)
    wc_ref[0, 0:S5_SUPER_ST, :] = jnp.where(ownc, cre_ref[...], 0.0).astype(BF16)
    wc_ref[0, S5_SUPER_ST:, :] = jnp.where(ownc, -cim_ref[...], 0.0).astype(BF16)
    a_re, a_im, _, _ = _s5_discretize(lref_ref[...], limf_ref[...], ldtf_ref[...])
    are_ref[...] = a_re
    aim_ref[...] = a_im


def _s5_prep(lam_re, lam_im, log_dt, b_re, b_im, c_re, c_im):
    g, p, ch, sg = S5_GROUPS, S5_STATE, S5_GROUP, S5_SUPER

    def rows_by_group(a_gp):
        return jnp.broadcast_to(a_gp[:, None, :], (g, sg, p)).reshape(g, sg * p)

    def b_layout(b):
        bt = jnp.transpose(b, (0, 2, 1))
        return jnp.broadcast_to(bt[:, :, None, :], (g, ch, sg, p)).reshape(g * ch, sg * p)

    def c_layout(cm):
        ct = jnp.transpose(cm.reshape(g // sg, sg, ch, p), (3, 0, 1, 2)).reshape(p, g * ch)
        return jnp.broadcast_to(ct[None], (sg, p, g * ch)).reshape(sg * p, g * ch)

    ldt_gp = jnp.broadcast_to(log_dt[:, None], (g, p))
    big = pl.BlockSpec((S5_SUPER_CH, S5_SUPER_ST), lambda j: (j, 0))
    per_group = pl.BlockSpec((S5_SUPER, S5_SUPER_ST), lambda j: (j, 0))
    cspec = pl.BlockSpec((S5_SUPER_ST, S5_SUPER_CH), lambda j: (0, j))
    flat = pl.BlockSpec((1, S5_SUPER_ST), lambda j: (0, j))
    return pl.pallas_call(
        _s5_prep_kernel,
        grid=(S5_NSUPER,),
        in_specs=[per_group] * 3 + [big] * 2 + [cspec] * 2 + [flat] * 3,
        out_specs=[
            pl.BlockSpec((1, S5_SUPER_CH, 2 * S5_SUPER_ST), lambda j: (j, 0, 0)),
            pl.BlockSpec((1, 2 * S5_SUPER_ST, S5_SUPER_CH), lambda j: (j, 0, 0)),
            flat, flat,
        ],
        out_shape=[
            jax.ShapeDtypeStruct((S5_NSUPER, S5_SUPER_CH, 2 * S5_SUPER_ST), BF16),
            jax.ShapeDtypeStruct((S5_NSUPER, 2 * S5_SUPER_ST, S5_SUPER_CH), BF16),
            jax.ShapeDtypeStruct((1, S5_FLAT), F32),
            jax.ShapeDtypeStruct((1, S5_FLAT), F32),
        ],
        compiler_params=pltpu.CompilerParams(dimension_semantics=("arbitrary",)),
        name="s5_prep",
    )(rows_by_group(lam_re), rows_by_group(lam_im), rows_by_group(ldt_gp),
      b_layout(b_re), b_layout(b_im), c_layout(c_re), c_layout(c_im),
      lam_re.reshape(1, S5_FLAT), lam_im.reshape(1, S5_FLAT), ldt_gp.reshape(1, S5_FLAT))


def _s5_kernel(x_ref, g_ref, wb_ref, wc_ref, are_ref, aim_ref, d_ref, wglu_ref, bglu_ref,
               h0re_ref, h0im_ref, perm_ref, unperm_ref, o_ref, hre_ref, him_ref,
               u_ref, sre_ref, sim_ref, z_ref, *, bsz, steps, sub):
    i = pl.program_id(0)
    rows = bsz * steps

    @pl.when(i == 0)
    def _():
        hre_ref[...] = h0re_ref[...]
        him_ref[...] = h0im_ref[...]

    def x_group(k):
        xk = x_ref[...] if sub == 1 else x_ref[:, k * steps:(k + 1) * steps, :]
        return xk.reshape(rows, D_MODEL)

    def region(k):
        r0 = (k % S5_REGIONS) * rows
        return slice(r0, r0 + rows)

    def input_side(k):
        grp = region(k)
        u = _dot(perm_ref[...], _rms(x_group(k), g_ref[0:1, :]).astype(BF16))
        u_ref[grp, :] = u
        ub = u.astype(BF16)
        for j in range(S5_NSUPER):
            bu = _dot(ub[:, j * S5_SUPER_CH:(j + 1) * S5_SUPER_CH], wb_ref[j])
            sre_ref[grp, j * S5_SUPER_ST:(j + 1) * S5_SUPER_ST] = bu[:, :S5_SUPER_ST]
            sim_ref[grp, j * S5_SUPER_ST:(j + 1) * S5_SUPER_ST] = bu[:, S5_SUPER_ST:]

    def scan_group(k):
        base = (k % S5_REGIONS) * rows
        for lc in range(S5_FLAT // S5_LANES):
            ls = slice(lc * S5_LANES, (lc + 1) * S5_LANES)
            a_re = jnp.broadcast_to(are_ref[:, ls], (SUBLANE, S5_LANES))
            a_im = jnp.broadcast_to(aim_ref[:, ls], (SUBLANE, S5_LANES))
            for rt in range(bsz // SUBLANE):
                rs = slice(rt * SUBLANE, (rt + 1) * SUBLANE)

                def step(t, carry, ls=ls, rt=rt, a_re=a_re, a_im=a_im):
                    h_re, h_im = carry
                    r8 = pl.ds(pl.multiple_of(base + t * bsz + rt * SUBLANE, SUBLANE), SUBLANE)
                    n_re = a_re * h_re - a_im * h_im + sre_ref[r8, ls]
                    n_im = a_re * h_im + a_im * h_re + sim_ref[r8, ls]
                    sre_ref[r8, ls] = n_re
                    sim_ref[r8, ls] = n_im
                    return n_re, n_im

                h_re, h_im = lax.fori_loop(0, steps, step, (hre_ref[rs, ls], him_ref[rs, ls]),
                                           unroll=True)
                hre_ref[rs, ls] = h_re
                him_ref[rs, ls] = h_im

    for k in range(min(S5_REGIONS, sub)):
        input_side(k)
    for k in range(sub):
        scan_group(k)
        grp = region(k)
        for j in range(S5_NSUPER):
            st = slice(j * S5_SUPER_ST, (j + 1) * S5_SUPER_ST)
            ch = slice(j * S5_SUPER_CH, (j + 1) * S5_SUPER_CH)
            y = (_dot(sre_ref[grp, st].astype(BF16), wc_ref[j, 0:S5_SUPER_ST, :])
                 + _dot(sim_ref[grp, st].astype(BF16), wc_ref[j, S5_SUPER_ST:, :]))
            z_ref[grp, ch] = (y + d_ref[:, ch] * u_ref[grp, ch]).astype(BF16)
        z = _dot(unperm_ref[...], z_ref[grp, :]).astype(BF16)
        zz = _dot(z, wglu_ref[...]) + bglu_ref[...]
        out = zz[:, :D_MODEL] * _sigmoid(zz[:, D_MODEL:])
        res = x_group(k) + _rms(out, g_ref[1:2, :])
        if sub == 1:
            o_ref[...] = res.reshape(o_ref.shape)
        else:
            o_ref[:, k * steps:(k + 1) * steps, :] = res.reshape(bsz, steps, D_MODEL)
        if k + S5_REGIONS < sub:
            input_side(k + S5_REGIONS)


def _s5(x, steps_per_block, g2, prep, d_skip, w_glu, b_glu, h0_re, h0_im):
    wb, wc, a_re, a_im = prep
    bsz, t, _ = x.shape
    rows = steps_per_block * bsz
    r = np.arange(rows)
    perm = np.zeros((rows, rows), np.float32)
    perm[(r % steps_per_block) * bsz + r // steps_per_block, r] = 1.0
    zero2 = lambda i: (0, 0)
    zero3 = lambda i: (0, 0, 0)
    resident = dict(pipeline_mode=pl.Buffered(1))
    state = pl.BlockSpec((bsz, S5_FLAT), zero2)
    if steps_per_block == t:
        sub = 1
        x = x.reshape(1, rows, D_MODEL)
        xspec = pl.BlockSpec((1, rows, D_MODEL), lambda i: (0, 0, 0))
    else:
        sub = S5_GROUPS_PER_STEP
        assert steps_per_block % SUBLANE == 0 and t % (sub * steps_per_block) == 0
        xspec = pl.BlockSpec((bsz, sub * steps_per_block, D_MODEL), lambda i: (0, i, 0))
    y, h_re, h_im = pl.pallas_call(
        functools.partial(_s5_kernel, bsz=bsz, steps=steps_per_block, sub=sub),
        grid=(t // (sub * steps_per_block),),
        in_specs=[
            xspec,
            pl.BlockSpec((2, D_MODEL), zero2),
            pl.BlockSpec(wb.shape, zero3, **resident),
            pl.BlockSpec(wc.shape, zero3, **resident),
            pl.BlockSpec((1, S5_FLAT), zero2),
            pl.BlockSpec((1, S5_FLAT), zero2),
            pl.BlockSpec((1, D_MODEL), zero2),
            pl.BlockSpec((D_MODEL, 2 * D_MODEL), zero2, **resident),
            pl.BlockSpec((1, 2 * D_MODEL), zero2),
            state, state,
            pl.BlockSpec((rows, rows), zero2),
            pl.BlockSpec((rows, rows), zero2),
        ],
        out_specs=[xspec, state, state],
        out_shape=[
            jax.ShapeDtypeStruct(x.shape, F32),
            jax.ShapeDtypeStruct((bsz, S5_FLAT), F32),
            jax.ShapeDtypeStruct((bsz, S5_FLAT), F32),
        ],
        scratch_shapes=[
            pltpu.VMEM((min(sub, S5_REGIONS) * rows, D_MODEL), F32),
            pltpu.VMEM((min(sub, S5_REGIONS) * rows, S5_FLAT), F32),
            pltpu.VMEM((min(sub, S5_REGIONS) * rows, S5_FLAT), F32),
            pltpu.VMEM((min(sub, S5_REGIONS) * rows, D_MODEL), BF16),
        ],
        compiler_params=pltpu.CompilerParams(
            dimension_semantics=("arbitrary",), vmem_limit_bytes=VMEM_LIMIT),
        name="s5",
    )(x, g2, wb, wc, a_re, a_im, d_skip.reshape(1, D_MODEL), w_glu.astype(BF16),
      b_glu.reshape(1, 2 * D_MODEL), h0_re, h0_im,
      jnp.asarray(perm, BF16), jnp.asarray(perm.T, BF16))
    return y.reshape(bsz, t, D_MODEL), h_re, h_im


def kernel(x_prompt, x_sample, state_gla, state_s5_re, state_s5_im, norm_g, w_ffn_gu, w_ffn_down,
           gla_w_in, gla_w_g2, gla_b_g, gla_g_onorm, gla_w_out,
           s5_lam_re, s5_lam_im, s5_log_dt, s5_b_re, s5_b_im, s5_c_re, s5_c_im, s5_d, s5_w_glu,
           s5_b_glu):
    pb = x_prompt.shape[0]
    sb, st, _ = x_sample.shape

    def ffn(xp, xs, layer, which):
        g2 = norm_g[layer, 4 * which:4 * which + 2]
        xs, wgu_bf, wd_bf = _ffn_stream(xs, g2, w_ffn_gu, w_ffn_down, layer, which)
        return _ffn_resident(xp, g2, wgu_bf, wd_bf), xs

    gla_w = _gla_weights(gla_w_in[0], gla_w_g2[0], gla_b_g[0], gla_g_onorm[0], gla_w_out[0])
    xp, xs = ffn(x_prompt, x_sample, 0, 0)
    xp, gla_p = _gla_prompt(xp, norm_g[0, 2:4], gla_w)
    xs, gla_s = _gla_sample(xs, state_gla.reshape(sb, GLA_HEADS, GLA_DK, GLA_DV), norm_g[0, 2:4], gla_w)
    xp, xs = ffn(xp, xs, 0, 1)

    xp, xs = ffn(xp, xs, 1, 0)
    prep = _s5_prep(s5_lam_re[0], s5_lam_im[0], s5_log_dt[0], s5_b_re[0], s5_b_im[0],
                    s5_c_re[0], s5_c_im[0])
    zeros = jnp.zeros((pb, S5_FLAT), F32)
    s5_args = (norm_g[1, 2:4], prep, s5_d[0], s5_w_glu[0], s5_b_glu[0])
    xp, hre_p, him_p = _s5(xp, S5_STEPS, *s5_args, zeros, zeros)
    xs, hre_s, him_s = _s5(xs, st, *s5_args,
                           state_s5_re.reshape(sb, S5_FLAT), state_s5_im.reshape(sb, S5_FLAT))
    y_prompt, y_sample = ffn(xp, xs, 1, 1)

    def s5_state(h, b):
        return h.reshape(1, b, S5_GROUPS, S5_STATE)

    return (y_prompt, y_sample, gla_p.reshape(1, pb, GLA_HEADS, GLA_DK, GLA_DV),
            s5_state(hre_p, pb), s5_state(him_p, pb),
            gla_s.reshape(1, sb, GLA_HEADS, GLA_DK, GLA_DV),
            s5_state(hre_s, sb), s5_state(him_s, sb))
```

```python
import functools
import math

import jax
import jax.numpy as jnp
import numpy as np
from jax import lax
from jax.experimental import pallas as pl
from jax.experimental.pallas import tpu as pltpu

F32 = jnp.float32
BF16 = jnp.bfloat16

D_MODEL = 1024
D_FF = 2816
GLA_HEADS = 4
GLA_DK = 128
GLA_DV = 256
GLA_KEY_DIM = GLA_HEADS * GLA_DK
GLA_VAL_DIM = GLA_HEADS * GLA_DV
GATE_RANK = 16
GLA_MAIN_DIM = 2 * GLA_KEY_DIM + 2 * GLA_VAL_DIM
GLA_IN_DIM = GLA_MAIN_DIM + GATE_RANK
GATE_TAU = 16.0
S5_GROUP = 16
S5_GROUPS = 64
S5_STATE = 64
S5_FLAT = S5_GROUPS * S5_STATE
S5_SUPER = 8
S5_NSUPER = S5_GROUPS // S5_SUPER
S5_SUPER_CH = S5_SUPER * S5_GROUP
S5_SUPER_ST = S5_SUPER * S5_STATE
EPS = 1e-6
LANE = 128
SUBLANE = 8
VMEM_LIMIT = 56 * 1024 * 1024

FFN_ROWS = 1024
FFN_GROUP = 512
FFN_CHUNK = 256
FFN_SLOTS = 4
GLA_CHUNK = 64
GLA_TBLOCK = 512
GLA_PROJ_CHUNK = 256
GLA_CUM_ROWS = 256
LOG2E = math.log2(math.e)
GLA_SAMPLE_SEQS = 16
S5_LANES = 512
S5_STEPS = 32
S5_GROUPS_PER_STEP = 4
S5_REGIONS = 2


def _dot(a, b):
    return jnp.dot(a, b, preferred_element_type=F32)


def _dot_nt(a, b):
    return lax.dot_general(a, b, (((1,), (1,)), ((), ())), preferred_element_type=F32)


def _rms(x, g):
    ms = jnp.mean(x * x, axis=-1, keepdims=True)
    return x * lax.rsqrt(ms + EPS) * g


def _sigmoid(x):
    return 1.0 / (1.0 + jnp.exp(-x))


def _split_bf16(x):
    hi = x.astype(BF16)
    lo = (x - hi.astype(F32)).astype(BF16)
    return hi, lo


def _ffn_row_groups(n_rows):
    return [slice(r0, min(r0 + FFN_GROUP, n_rows)) for r0 in range(0, n_rows, FFN_GROUP)]


def _ffn_hidden_chunk(xn_ref, wgu_ref, act_ref, rows, c):
    lo = c * FFN_CHUNK
    gate = _dot(xn_ref[rows, :], wgu_ref[:, lo:lo + FFN_CHUNK])
    up = _dot(xn_ref[rows, :], wgu_ref[:, D_FF + lo:D_FF + lo + FFN_CHUNK])
    act_ref[rows, lo:lo + FFN_CHUNK] = (gate * _sigmoid(gate) * up).astype(BF16)


def _ffn_finish(x_ref, g_ref, wd_ref, o_ref, act_ref, rows):
    y = _dot(act_ref[rows, :], wd_ref[...])
    o_ref[rows, :] = x_ref[rows, :] + 0.5 * _rms(y, g_ref[1:2, :])


def _ffn_stream_kernel(x_ref, g_ref, wgu_hbm, wd_hbm, o_ref, wgu_ref, wd_ref,
                       xn_ref, act_ref, gu_stage, d_stage, sem, *, layer, which):
    n_chunks = D_FF // FFN_CHUNK
    groups = _ffn_row_groups(x_ref.shape[0])
    xn_ref[...] = _rms(x_ref[...], g_ref[0:1, :]).astype(BF16)

    def chunk_slices(c):
        lo = c * FFN_CHUNK
        return ((slice(None), pl.ds(lo, FFN_CHUNK)), (slice(None), pl.ds(D_FF + lo, FFN_CHUNK)),
                (pl.ds(lo, FFN_CHUNK), slice(None)))

    def fetches(c):
        slot = c % FFN_SLOTS
        gate, up, down = chunk_slices(c)
        return (
            pltpu.make_async_copy(wgu_hbm.at[(layer, which) + gate], gu_stage.at[slot, 0], sem.at[slot, 0]),
            pltpu.make_async_copy(wgu_hbm.at[(layer, which) + up], gu_stage.at[slot, 1], sem.at[slot, 1]),
            pltpu.make_async_copy(wd_hbm.at[(layer, which) + down], d_stage.at[slot], sem.at[slot, 2]),
        )

    ahead = FFN_SLOTS - 1
    for c in range(min(ahead, n_chunks)):
        for cp in fetches(c):
            cp.start()
    for c in range(n_chunks):
        slot, lo = c % FFN_SLOTS, c * FFN_CHUNK
        for cp in fetches(c):
            cp.wait()
        if c + ahead < n_chunks:
            for cp in fetches(c + ahead):
                cp.start()
        wgu_ref[:, lo:lo + FFN_CHUNK] = gu_stage[slot, 0].astype(BF16)
        wgu_ref[:, D_FF + lo:D_FF + lo + FFN_CHUNK] = gu_stage[slot, 1].astype(BF16)
        wd_ref[lo:lo + FFN_CHUNK, :] = d_stage[slot].astype(BF16)
        for rows in groups:
            _ffn_hidden_chunk(xn_ref, wgu_ref, act_ref, rows, c)
    for rows in groups:
        _ffn_finish(x_ref, g_ref, wd_ref, o_ref, act_ref, rows)


def _ffn_resident_kernel(x_ref, g_ref, wgu_ref, wd_ref, o_ref, xn_ref, act_ref):
    groups = _ffn_row_groups(x_ref.shape[0])

    def normalise(rows):
        xn_ref[rows, :] = _rms(x_ref[rows, :], g_ref[0:1, :]).astype(BF16)

    normalise(groups[0])
    for gi, rows in enumerate(groups):
        for c in range(D_FF // FFN_CHUNK):
            _ffn_hidden_chunk(xn_ref, wgu_ref, act_ref, rows, c)
            if c == 0 and gi + 1 < len(groups):
                normalise(groups[gi + 1])
        _ffn_finish(x_ref, g_ref, wd_ref, o_ref, act_ref, rows)


def _ffn_stream(x, g2, wgu, wd, layer, which):
    shape = x.shape
    x = x.reshape(-1, D_MODEL)
    n = x.shape[0]
    zero = lambda i: (0, 0)
    y, wgu_bf, wd_bf = pl.pallas_call(
        functools.partial(_ffn_stream_kernel, layer=layer, which=which),
        grid=(1,),
        in_specs=[
            pl.BlockSpec((n, D_MODEL), zero),
            pl.BlockSpec((2, D_MODEL), zero),
            pl.BlockSpec(memory_space=pl.ANY),
            pl.BlockSpec(memory_space=pl.ANY),
        ],
        out_specs=[
            pl.BlockSpec((n, D_MODEL), zero),
            pl.BlockSpec((D_MODEL, 2 * D_FF), zero, pipeline_mode=pl.Buffered(1)),
            pl.BlockSpec((D_FF, D_MODEL), zero, pipeline_mode=pl.Buffered(1)),
        ],
        out_shape=[
            jax.ShapeDtypeStruct((n, D_MODEL), F32),
            jax.ShapeDtypeStruct((D_MODEL, 2 * D_FF), BF16),
            jax.ShapeDtypeStruct((D_FF, D_MODEL), BF16),
        ],
        scratch_shapes=[
            pltpu.VMEM((n, D_MODEL), BF16),
            pltpu.VMEM((n, D_FF), BF16),
            pltpu.VMEM((FFN_SLOTS, 2, D_MODEL, FFN_CHUNK), F32),
            pltpu.VMEM((FFN_SLOTS, FFN_CHUNK, D_MODEL), F32),
            pltpu.SemaphoreType.DMA((FFN_SLOTS, 3)),
        ],
        compiler_params=pltpu.CompilerParams(
            dimension_semantics=("arbitrary",), vmem_limit_bytes=VMEM_LIMIT),
        name="ffn_stream",
    )(x, g2, wgu, wd)
    return y.reshape(shape), wgu_bf, wd_bf


def _ffn_resident(x, g2, wgu_bf, wd_bf):
    shape = x.shape
    x = x.reshape(-1, D_MODEL)
    n = x.shape[0]
    tm = min(FFN_ROWS, n)
    assert n % tm == 0
    zero = lambda i: (0, 0)
    resident = dict(pipeline_mode=pl.Buffered(1))
    return pl.pallas_call(
        _ffn_resident_kernel,
        grid=(n // tm,),
        in_specs=[
            pl.BlockSpec((tm, D_MODEL), lambda i: (i, 0)),
            pl.BlockSpec((2, D_MODEL), zero),
            pl.BlockSpec((D_MODEL, 2 * D_FF), zero, **resident),
            pl.BlockSpec((D_FF, D_MODEL), zero, **resident),
        ],
        out_specs=pl.BlockSpec((tm, D_MODEL), lambda i: (i, 0)),
        out_shape=jax.ShapeDtypeStruct((n, D_MODEL), F32),
        scratch_shapes=[pltpu.VMEM((tm, D_MODEL), BF16), pltpu.VMEM((tm, D_FF), BF16)],
        compiler_params=pltpu.CompilerParams(
            dimension_semantics=("arbitrary",), vmem_limit_bytes=VMEM_LIMIT),
        name="ffn",
    )(x, g2, wgu_bf, wd_bf).reshape(shape)


def _gla_tables(rows, seq):
    t = np.arange(rows)[:, None]
    r = np.arange(rows)[None, :]
    cum = ((t // seq) == (r // seq)) & (r <= t)
    masks = [t == r]
    m = seq // 2
    while m >= 1:
        masks.append(((t // (2 * m)) == (r // (2 * m))) & (t % (2 * m) >= m) & (r % (2 * m) < m))
        m //= 2
    return jnp.asarray(cum.astype(np.float32), BF16), jnp.asarray(np.stack(masks).astype(np.float32))


def _gla_project_cols(hn_ref, win_ref, proj_ref, c0):
    proj_ref[:, c0:c0 + GLA_PROJ_CHUNK] = _dot(hn_ref[...], win_ref[:, c0:c0 + GLA_PROJ_CHUNK])


def _gla_project(x, g_ref, win_ref, wglr_ref, wg2_ref, bg_ref, cum_ref, hn_ref, proj_ref, lf_ref, b_ref):
    hn = _rms(x, g_ref[0:1, :]).astype(BF16)
    hn_ref[...] = hn
    glr = _dot(hn, wglr_ref[...]).astype(BF16)
    gate = _dot(glr, wg2_ref[...]) + bg_ref[...]
    for c0 in range(0, 2 * GLA_KEY_DIM, GLA_PROJ_CHUNK):
        _gla_project_cols(hn_ref, win_ref, proj_ref, c0)
    lf = (jnp.minimum(gate, 0.0) - jnp.log1p(jnp.exp(-jnp.abs(gate)))) * (1.0 / GATE_TAU)
    lf_ref[...] = lf
    hi, lo = _split_bf16(lf)
    span = cum_ref.shape[0]
    for r0 in range(0, lf.shape[0], span):
        b_ref[r0:r0 + span, :] = (_dot(cum_ref[...], hi[r0:r0 + span]) + _dot(cum_ref[...], lo[r0:r0 + span]))


def _block_rows(ref, r0, col, rows, period, offset):
    def bc(row, n):
        tile = ref[pl.ds(r0 + row // SUBLANE * SUBLANE, SUBLANE), col]
        return jnp.broadcast_to(tile[row % SUBLANE:row % SUBLANE + 1, :], (n, GLA_DK))

    if period >= SUBLANE:
        parts = [bc(p0 + offset, period) for p0 in range(0, rows, period)]
    else:
        assert 2 * period == SUBLANE
        low = lax.broadcasted_iota(jnp.int32, (SUBLANE, 1), 0) < period
        parts = [jnp.where(low, bc(t0 + offset, SUBLANE), bc(t0 + period + offset, SUBLANE))
                 for t0 in range(0, rows, SUBLANE)]
    return parts[0] if len(parts) == 1 else jnp.concatenate(parts, axis=0)


def _gla_attention(qs, k, b, lf, b_ref, r0, col, rows, seq, mask_ref):
    qb, kb = qs.astype(BF16), k.astype(BF16)
    att = _dot_nt(qb, kb) * mask_ref[0]
    lvl, m = 1, seq // 2
    while m >= 1:
        if m > 1:
            d = b - _block_rows(b_ref, r0, col, rows, 2 * m, m - 1)
            w = jnp.exp2(jnp.abs(d) * (-LOG2E))
        else:
            odd = lax.broadcasted_iota(jnp.int32, (rows, 1), 0) % 2 == 1
            w = jnp.where(odd, jnp.exp(lf), 1.0)
        wb = w.astype(BF16)
        att = att + _dot_nt(qb * wb, kb * wb) * mask_ref[lvl]
        lvl, m = lvl + 1, m // 2
    return att


def _gla_finish_head(o, r, gon_ref):
    on = _rms(o, gon_ref[...])
    return (on * (r * _sigmoid(r))).astype(BF16)


def _gla_prompt_kernel(x_ref, g_ref, win_ref, wglr_ref, wg2_ref, bg_ref, gon_ref, wout_ref,
                       cum_ref, mask_ref, o_ref, snew_ref,
                       hn_ref, proj_ref, lf_ref, b_ref, og_ref, s_ref, att_ref, qb_ref, tt_ref):
    tb = pl.program_id(1)
    c = GLA_CHUNK

    @pl.when(tb == 0)
    def _():
        s_ref[...] = jnp.zeros_like(s_ref)

    x = x_ref[0]
    _gla_project(x, g_ref, win_ref, wglr_ref, wg2_ref, bg_ref, cum_ref, hn_ref, proj_ref, lf_ref, b_ref)

    late_cols = list(range(2 * GLA_KEY_DIM, proj_ref.shape[1], GLA_PROJ_CHUNK))
    n_heads_total = GLA_TBLOCK // c * GLA_HEADS
    for ci, r0 in enumerate(range(0, GLA_TBLOCK, c)):
        rows = slice(r0, r0 + c)
        for h in range(GLA_HEADS):
            done = ci * GLA_HEADS + h
            for c0 in late_cols[done * len(late_cols) // n_heads_total:
                                (done + 1) * len(late_cols) // n_heads_total]:
                _gla_project_cols(hn_ref, win_ref, proj_ref, c0)
            kcol = slice(h * GLA_DK, (h + 1) * GLA_DK)
            qs = proj_ref[rows, kcol] * (GLA_DK ** -0.5)
            k = proj_ref[rows, GLA_KEY_DIM + h * GLA_DK:GLA_KEY_DIM + (h + 1) * GLA_DK]
            b = b_ref[rows, kcol]
            att = _gla_attention(qs, k, b, lf_ref[rows, kcol], b_ref, r0, kcol, c, c, mask_ref)
            att_ref[ci * GLA_HEADS + h] = att.astype(BF16)
            qb_ref[rows, kcol] = (qs * jnp.exp(b)).astype(BF16)
            khat = k * jnp.exp(_block_rows(b_ref, r0, kcol, c, c, c - 1) - b)
            tile = jnp.concatenate(
                [khat, jnp.exp(b[c - SUBLANE:c, :]), jnp.zeros((LANE - c - SUBLANE, GLA_DK), F32)], axis=0)
            tt_ref[ci * GLA_HEADS + h] = jnp.transpose(tile)

    for ci, r0 in enumerate(range(0, GLA_TBLOCK, c)):
        rows = slice(r0, r0 + c)
        for h in range(GLA_HEADS):
            kcol = slice(h * GLA_DK, (h + 1) * GLA_DK)
            vcol = slice(h * GLA_DV, (h + 1) * GLA_DV)
            v = proj_ref[rows, 2 * GLA_KEY_DIM + h * GLA_DV:2 * GLA_KEY_DIM + (h + 1) * GLA_DV].astype(BF16)
            r = proj_ref[rows, 2 * GLA_KEY_DIM + GLA_VAL_DIM + h * GLA_DV:
                         2 * GLA_KEY_DIM + GLA_VAL_DIM + (h + 1) * GLA_DV]
            s = s_ref[h]
            tt = tt_ref[ci * GLA_HEADS + h]
            o = _dot(att_ref[ci * GLA_HEADS + h], v) + _dot(qb_ref[rows, kcol], s.astype(BF16))
            decay = tt[:, c + SUBLANE - 1:c + SUBLANE]
            s_ref[h] = s * decay + _dot(tt[:, 0:c].astype(BF16), v)
            og_ref[rows, vcol] = _gla_finish_head(o, r, gon_ref)
    out = _dot(og_ref[...], wout_ref[...])
    o_ref[0] = x + _rms(out, g_ref[1:2, :])

    @pl.when(tb == pl.num_programs(1) - 1)
    def _():
        snew_ref[0] = s_ref[...]


def _gla_sample_kernel(x_ref, g_ref, win_ref, wglr_ref, wg2_ref, bg_ref, gon_ref, wout_ref,
                       cum_ref, mask_ref, s0_ref, o_ref, snew_ref,
                       hn_ref, proj_ref, lf_ref, b_ref, og_ref, *, seq):
    rows = x_ref.shape[0]
    grp = 2 * SUBLANE
    x = x_ref[...]
    _gla_project(x, g_ref, win_ref, wglr_ref, wg2_ref, bg_ref, cum_ref, hn_ref, proj_ref, lf_ref, b_ref)
    for c0 in range(2 * GLA_KEY_DIM, proj_ref.shape[1], GLA_PROJ_CHUNK):
        _gla_project_cols(hn_ref, win_ref, proj_ref, c0)
    rid = lax.broadcasted_iota(jnp.int32, (grp, 1), 0)
    for h in range(GLA_HEADS):
        kcol = slice(h * GLA_DK, (h + 1) * GLA_DK)
        vcol = slice(h * GLA_DV, (h + 1) * GLA_DV)
        qs = proj_ref[:, kcol] * (GLA_DK ** -0.5)
        k = proj_ref[:, GLA_KEY_DIM + h * GLA_DK:GLA_KEY_DIM + (h + 1) * GLA_DK]
        v = proj_ref[:, 2 * GLA_KEY_DIM + h * GLA_DV:2 * GLA_KEY_DIM + (h + 1) * GLA_DV]
        r = proj_ref[:, 2 * GLA_KEY_DIM + GLA_VAL_DIM + h * GLA_DV:
                     2 * GLA_KEY_DIM + GLA_VAL_DIM + (h + 1) * GLA_DV]
        b = b_ref[:, kcol]
        att = _gla_attention(qs, k, b, lf_ref[:, kcol], b_ref, 0, kcol, rows, seq, mask_ref)
        eb = jnp.exp(b)
        qb = (qs * eb).astype(BF16)
        khat = k * jnp.exp(_block_rows(b_ref, 0, kcol, rows, seq, seq - 1) - b)
        o_intra = _dot(att.astype(BF16), v.astype(BF16))
        o_parts = []
        for gi in range(rows // grp):
            gr = slice(gi * grp, (gi + 1) * grp)
            tile = jnp.concatenate(
                [khat[gr], eb[gr], jnp.zeros((LANE - 2 * grp, GLA_DK), F32)], axis=0)
            tt = jnp.transpose(tile)
            kt = tt[:, 0:grp].astype(BF16)
            qg = qb[gr]
            vg = v[gr]
            o_g = jnp.zeros((grp, GLA_DV), F32)
            for si in range(grp // seq):
                sq = gi * (grp // seq) + si
                mine = (rid >= si * seq) & (rid < (si + 1) * seq)
                s = s0_ref[sq, h]
                o_g = o_g + jnp.where(mine, _dot(qg, s.astype(BF16)), 0.0)
                vm = jnp.where(mine, vg, 0.0).astype(BF16)
                last = grp + (si + 1) * seq - 1
                snew_ref[sq, h] = s * tt[:, last:last + 1] + _dot(kt, vm)
            o_parts.append(o_g)
        o = o_intra + jnp.concatenate(o_parts, axis=0)
        og_ref[:, vcol] = _gla_finish_head(o, r, gon_ref)
    out = _dot(og_ref[...], wout_ref[...])
    o_ref[...] = x + _rms(out, g_ref[1:2, :])


def _gla_weight_specs():
    zero = (lambda *_: (0, 0))
    resident = dict(pipeline_mode=pl.Buffered(1))
    return [
        pl.BlockSpec((2, D_MODEL), zero),
        pl.BlockSpec((D_MODEL, GLA_IN_DIM), zero, **resident),
        pl.BlockSpec((D_MODEL, LANE), zero, **resident),
        pl.BlockSpec((LANE, GLA_KEY_DIM), zero, **resident),
        pl.BlockSpec((1, GLA_KEY_DIM), zero),
        pl.BlockSpec((1, GLA_DV), zero),
        pl.BlockSpec((GLA_VAL_DIM, D_MODEL), zero, **resident),
    ]


def _gla_weights(w_in, w_g2, b_g, g_onorm, w_out):
    w_glr = jnp.pad(w_in[:, GLA_MAIN_DIM:], ((0, 0), (0, LANE - GATE_RANK))).astype(BF16)
    w_g2p = jnp.pad(w_g2, ((0, LANE - GATE_RANK), (0, 0))).astype(BF16)
    return (w_in.astype(BF16), w_glr, w_g2p, b_g.reshape(1, GLA_KEY_DIM),
            g_onorm.reshape(1, GLA_DV), w_out.astype(BF16))


def _gla_prompt(x, g2, weights):
    bsz, t, _ = x.shape
    cum, _ = _gla_tables(GLA_CUM_ROWS, GLA_CHUNK)
    _, masks = _gla_tables(GLA_CHUNK, GLA_CHUNK)
    n_main = GLA_MAIN_DIM
    const2 = lambda b, i: (0, 0)
    return pl.pallas_call(
        _gla_prompt_kernel,
        grid=(bsz, t // GLA_TBLOCK),
        in_specs=[pl.BlockSpec((1, GLA_TBLOCK, D_MODEL), lambda b, i: (b, i, 0))]
        + _gla_weight_specs()
        + [pl.BlockSpec(cum.shape, const2),
           pl.BlockSpec(masks.shape, lambda b, i: (0, 0, 0))],
        out_specs=[
            pl.BlockSpec((1, GLA_TBLOCK, D_MODEL), lambda b, i: (b, i, 0)),
            pl.BlockSpec((1, GLA_HEADS, GLA_DK, GLA_DV), lambda b, i: (b, 0, 0, 0)),
        ],
        out_shape=[
            jax.ShapeDtypeStruct(x.shape, F32),
            jax.ShapeDtypeStruct((bsz, GLA_HEADS, GLA_DK, GLA_DV), F32),
        ],
        scratch_shapes=[
            pltpu.VMEM((GLA_TBLOCK, D_MODEL), BF16),
            pltpu.VMEM((GLA_TBLOCK, n_main), F32),
            pltpu.VMEM((GLA_TBLOCK, GLA_KEY_DIM), F32),
            pltpu.VMEM((GLA_TBLOCK, GLA_KEY_DIM), F32),
            pltpu.VMEM((GLA_TBLOCK, GLA_VAL_DIM), BF16),
            pltpu.VMEM((GLA_HEADS, GLA_DK, GLA_DV), F32),
            pltpu.VMEM((GLA_TBLOCK // GLA_CHUNK * GLA_HEADS, GLA_CHUNK, GLA_CHUNK), BF16),
            pltpu.VMEM((GLA_TBLOCK, GLA_KEY_DIM), BF16),
            pltpu.VMEM((GLA_TBLOCK // GLA_CHUNK * GLA_HEADS, LANE, GLA_DK), F32),
        ],
        compiler_params=pltpu.CompilerParams(
            dimension_semantics=("arbitrary", "arbitrary"), vmem_limit_bytes=VMEM_LIMIT),
        name="gla_prompt",
    )(x, g2, *weights, cum, masks)


def _gla_sample(x, s0, g2, weights):
    bsz, seq, _ = x.shape
    rows = GLA_SAMPLE_SEQS * seq
    cum, masks = _gla_tables(rows, seq)
    n_main = GLA_MAIN_DIM
    state_spec = pl.BlockSpec((GLA_SAMPLE_SEQS, GLA_HEADS, GLA_DK, GLA_DV), lambda i: (i, 0, 0, 0))
    y, snew = pl.pallas_call(
        functools.partial(_gla_sample_kernel, seq=seq),
        grid=(bsz // GLA_SAMPLE_SEQS,),
        in_specs=[pl.BlockSpec((rows, D_MODEL), lambda i: (i, 0))]
        + _gla_weight_specs()
        + [pl.BlockSpec(cum.shape, lambda i: (0, 0)),
           pl.BlockSpec(masks.shape, lambda i: (0, 0, 0)),
           state_spec],
        out_specs=[pl.BlockSpec((rows, D_MODEL), lambda i: (i, 0)), state_spec],
        out_shape=[
            jax.ShapeDtypeStruct((bsz * seq, D_MODEL), F32),
            jax.ShapeDtypeStruct(s0.shape, F32),
        ],
        scratch_shapes=[
            pltpu.VMEM((rows, D_MODEL), BF16),
            pltpu.VMEM((rows, n_main), F32),
            pltpu.VMEM((rows, GLA_KEY_DIM), F32),
            pltpu.VMEM((rows, GLA_KEY_DIM), F32),
            pltpu.VMEM((rows, GLA_VAL_DIM), BF16),
        ],
        compiler_params=pltpu.CompilerParams(
            dimension_semantics=("arbitrary",), vmem_limit_bytes=VMEM_LIMIT),
        name="gla_sample",
    )(x.reshape(bsz * seq, D_MODEL), g2, *weights, cum, masks, s0)
    return y.reshape(x.shape), snew


def _s5_discretize(lam_re, lam_im, log_dt):
    dt = jnp.exp(log_dt)
    mag = jnp.exp(lam_re * dt)
    ang = lam_im * dt
    a_re, a_im = mag * jnp.cos(ang), mag * jnp.sin(ang)
    nr, ni = a_re - 1.0, a_im
    den = lam_re * lam_re + lam_im * lam_im
    f_re = (nr * lam_re + ni * lam_im) / den
    f_im = (ni * lam_re - nr * lam_im) / den
    return a_re, a_im, f_re, f_im


def _s5_prep_kernel(lre_ref, lim_ref, ldt_ref, bre_ref, bim_ref, cre_ref, cim_ref,
                    lref_ref, limf_ref, ldtf_ref, wb_ref, wc_ref, are_ref, aim_ref):
    _, _, f_re, f_im = _s5_discretize(lre_ref[...], lim_ref[...], ldt_ref[...])

    def per_channel(f):
        return jnp.broadcast_to(f[:, None, :], (S5_SUPER, S5_GROUP, S5_SUPER_ST)).reshape(
            S5_SUPER_CH, S5_SUPER_ST)

    f_re, f_im = per_channel(f_re), per_channel(f_im)
    b_re, b_im = bre_ref[...], bim_ref[...]
    row = lax.broadcasted_iota(jnp.int32, (S5_SUPER_CH, S5_SUPER_ST), 0)
    col = lax.broadcasted_iota(jnp.int32, (S5_SUPER_CH, S5_SUPER_ST), 1)
    own = (row // S5_GROUP) == (col // S5_STATE)
    wb_ref[0, :, 0:S5_SUPER_ST] = jnp.where(own, f_re * b_re - f_im * b_im, 0.0).astype(BF16)
    wb_ref[0, :, S5_SUPER_ST:] = jnp.where(own, f_re * b_im + f_im * b_re, 0.0).astype(BF16)
    rowc = lax.broadcasted_iota(jnp.int32, (S5_SUPER_ST, S5_SUPER_CH), 0)
    colc = lax.broadcasted_iota(jnp.int32, (S5_SUPER_ST, S5_SUPER_CH), 1)
    ownc = (rowc // S5_STATE) == (colc // S5_GROUP)
    wc_ref[0, 0:S5_SUPER_ST, :] = jnp.where(ownc, cre_ref[...], 0.0).astype(BF16)
    wc_ref[0, S5_SUPER_ST:, :] = jnp.where(ownc, -cim_ref[...], 0.0).astype(BF16)
    a_re, a_im, _, _ = _s5_discretize(lref_ref[...], limf_ref[...], ldtf_ref[...])
    are_ref[...] = a_re
    aim_ref[...] = a_im


def _s5_prep(lam_re, lam_im, log_dt, b_re, b_im, c_re, c_im):
    g, p, ch, sg = S5_GROUPS, S5_STATE, S5_GROUP, S5_SUPER

    def rows_by_group(a_gp):
        return jnp.broadcast_to(a_gp[:, None, :], (g, sg, p)).reshape(g, sg * p)

    def b_layout(b):
        bt = jnp.transpose(b, (0, 2, 1))
        return jnp.broadcast_to(bt[:, :, None, :], (g, ch, sg, p)).reshape(g * ch, sg * p)

    def c_layout(cm):
        ct = jnp.transpose(cm.reshape(g // sg, sg, ch, p), (3, 0, 1, 2)).reshape(p, g * ch)
        return jnp.broadcast_to(ct[None], (sg, p, g * ch)).reshape(sg * p, g * ch)

    ldt_gp = jnp.broadcast_to(log_dt[:, None], (g, p))
    big = pl.BlockSpec((S5_SUPER_CH, S5_SUPER_ST), lambda j: (j, 0))
    per_group = pl.BlockSpec((S5_SUPER, S5_SUPER_ST), lambda j: (j, 0))
    cspec = pl.BlockSpec((S5_SUPER_ST, S5_SUPER_CH), lambda j: (0, j))
    flat = pl.BlockSpec((1, S5_SUPER_ST), lambda j: (0, j))
    return pl.pallas_call(
        _s5_prep_kernel,
        grid=(S5_NSUPER,),
        in_specs=[per_group] * 3 + [big] * 2 + [cspec] * 2 + [flat] * 3,
        out_specs=[
            pl.BlockSpec((1, S5_SUPER_CH, 2 * S5_SUPER_ST), lambda j: (j, 0, 0)),
            pl.BlockSpec((1, 2 * S5_SUPER_ST, S5_SUPER_CH), lambda j: (j, 0, 0)),
            flat, flat,
        ],
        out_shape=[
            jax.ShapeDtypeStruct((S5_NSUPER, S5_SUPER_CH, 2 * S5_SUPER_ST), BF16),
            jax.ShapeDtypeStruct((S5_NSUPER, 2 * S5_SUPER_ST, S5_SUPER_CH), BF16),
            jax.ShapeDtypeStruct((1, S5_FLAT), F32),
            jax.ShapeDtypeStruct((1, S5_FLAT), F32),
        ],
        compiler_params=pltpu.CompilerParams(dimension_semantics=("arbitrary",)),
        name="s5_prep",
    )(rows_by_group(lam_re), rows_by_group(lam_im), rows_by_group(ldt_gp),
      b_layout(b_re), b_layout(b_im), c_layout(c_re), c_layout(c_im),
      lam_re.reshape(1, S5_FLAT), lam_im.reshape(1, S5_FLAT), ldt_gp.reshape(1, S5_FLAT))


def _s5_kernel(x_ref, g_ref, wb_ref, wc_ref, are_ref, aim_ref, d_ref, wglu_ref, bglu_ref,
               h0re_ref, h0im_ref, perm_ref, unperm_ref, o_ref, hre_ref, him_ref,
               u_ref, sre_ref, sim_ref, z_ref, *, bsz, steps, sub):
    i = pl.program_id(0)
    rows = bsz * steps

    @pl.when(i == 0)
    def _():
        hre_ref[...] = h0re_ref[...]
        him_ref[...] = h0im_ref[...]

    def x_group(k):
        xk = x_ref[...] if sub == 1 else x_ref[:, k * steps:(k + 1) * steps, :]
        return xk.reshape(rows, D_MODEL)

    def region(k):
        r0 = (k % S5_REGIONS) * rows
        return slice(r0, r0 + rows)

    def input_side(k):
        grp = region(k)
        u = _dot(perm_ref[...], _rms(x_group(k), g_ref[0:1, :]).astype(BF16))
        u_ref[grp, :] = u
        ub = u.astype(BF16)
        for j in range(S5_NSUPER):
            bu = _dot(ub[:, j * S5_SUPER_CH:(j + 1) * S5_SUPER_CH], wb_ref[j])
            sre_ref[grp, j * S5_SUPER_ST:(j + 1) * S5_SUPER_ST] = bu[:, :S5_SUPER_ST]
            sim_ref[grp, j * S5_SUPER_ST:(j + 1) * S5_SUPER_ST] = bu[:, S5_SUPER_ST:]

    def scan_group(k):
        base = (k % S5_REGIONS) * rows
        for lc in range(S5_FLAT // S5_LANES):
            ls = slice(lc * S5_LANES, (lc + 1) * S5_LANES)
            a_re = jnp.broadcast_to(are_ref[:, ls], (SUBLANE, S5_LANES))
            a_im = jnp.broadcast_to(aim_ref[:, ls], (SUBLANE, S5_LANES))
            for rt in range(bsz // SUBLANE):
                rs = slice(rt * SUBLANE, (rt + 1) * SUBLANE)

                def step(t, carry, ls=ls, rt=rt, a_re=a_re, a_im=a_im):
                    h_re, h_im = carry
                    r8 = pl.ds(pl.multiple_of(base + t * bsz + rt * SUBLANE, SUBLANE), SUBLANE)
                    n_re = a_re * h_re - a_im * h_im + sre_ref[r8, ls]
                    n_im = a_re * h_im + a_im * h_re + sim_ref[r8, ls]
                    sre_ref[r8, ls] = n_re
                    sim_ref[r8, ls] = n_im
                    return n_re, n_im

                h_re, h_im = lax.fori_loop(0, steps, step, (hre_ref[rs, ls], him_ref[rs, ls]),
                                           unroll=True)
                hre_ref[rs, ls] = h_re
                him_ref[rs, ls] = h_im

    for k in range(min(S5_REGIONS, sub)):
        input_side(k)
    for k in range(sub):
        scan_group(k)
        grp = region(k)
        for j in range(S5_NSUPER):
            st = slice(j * S5_SUPER_ST, (j + 1) * S5_SUPER_ST)
            ch = slice(j * S5_SUPER_CH, (j + 1) * S5_SUPER_CH)
            y = (_dot(sre_ref[grp, st].astype(BF16), wc_ref[j, 0:S5_SUPER_ST, :])
                 + _dot(sim_ref[grp, st].astype(BF16), wc_ref[j, S5_SUPER_ST:, :]))
            z_ref[grp, ch] = (y + d_ref[:, ch] * u_ref[grp, ch]).astype(BF16)
        z = _dot(unperm_ref[...], z_ref[grp, :]).astype(BF16)
        zz = _dot(z, wglu_ref[...]) + bglu_ref[...]
        out = zz[:, :D_MODEL] * _sigmoid(zz[:, D_MODEL:])
        res = x_group(k) + _rms(out, g_ref[1:2, :])
        if sub == 1:
            o_ref[...] = res.reshape(o_ref.shape)
        else:
            o_ref[:, k * steps:(k + 1) * steps, :] = res.reshape(bsz, steps, D_MODEL)
        if k + S5_REGIONS < sub:
            input_side(k + S5_REGIONS)


def _s5(x, steps_per_block, g2, prep, d_skip, w_glu, b_glu, h0_re, h0_im):
    wb, wc, a_re, a_im = prep
    bsz, t, _ = x.shape
    rows = steps_per_block * bsz
    r = np.arange(rows)
    perm = np.zeros((rows, rows), np.float32)
    perm[(r % steps_per_block) * bsz + r // steps_per_block, r] = 1.0
    zero2 = lambda i: (0, 0)
    zero3 = lambda i: (0, 0, 0)
    resident = dict(pipeline_mode=pl.Buffered(1))
    state = pl.BlockSpec((bsz, S5_FLAT), zero2)
    if steps_per_block == t:
        sub = 1
        x = x.reshape(1, rows, D_MODEL)
        xspec = pl.BlockSpec((1, rows, D_MODEL), lambda i: (0, 0, 0))
    else:
        sub = S5_GROUPS_PER_STEP
        assert steps_per_block % SUBLANE == 0 and t % (sub * steps_per_block) == 0
        xspec = pl.BlockSpec((bsz, sub * steps_per_block, D_MODEL), lambda i: (0, i, 0))
    y, h_re, h_im = pl.pallas_call(
        functools.partial(_s5_kernel, bsz=bsz, steps=steps_per_block, sub=sub),
        grid=(t // (sub * steps_per_block),),
        in_specs=[
            xspec,
            pl.BlockSpec((2, D_MODEL), zero2),
            pl.BlockSpec(wb.shape, zero3, **resident),
            pl.BlockSpec(wc.shape, zero3, **resident),
            pl.BlockSpec((1, S5_FLAT), zero2),
            pl.BlockSpec((1, S5_FLAT), zero2),
            pl.BlockSpec((1, D_MODEL), zero2),
            pl.BlockSpec((D_MODEL, 2 * D_MODEL), zero2, **resident),
            pl.BlockSpec((1, 2 * D_MODEL), zero2),
            state, state,
            pl.BlockSpec((rows, rows), zero2),
            pl.BlockSpec((rows, rows), zero2),
        ],
        out_specs=[xspec, state, state],
        out_shape=[
            jax.ShapeDtypeStruct(x.shape, F32),
            jax.ShapeDtypeStruct((bsz, S5_FLAT), F32),
            jax.ShapeDtypeStruct((bsz, S5_FLAT), F32),
        ],
        scratch_shapes=[
            pltpu.VMEM((min(sub, S5_REGIONS) * rows, D_MODEL), F32),
            pltpu.VMEM((min(sub, S5_REGIONS) * rows, S5_FLAT), F32),
            pltpu.VMEM((min(sub, S5_REGIONS) * rows, S5_FLAT), F32),
            pltpu.VMEM((min(sub, S5_REGIONS) * rows, D_MODEL), BF16),
        ],
        compiler_params=pltpu.CompilerParams(
            dimension_semantics=("arbitrary",), vmem_limit_bytes=VMEM_LIMIT),
        name="s5",
    )(x, g2, wb, wc, a_re, a_im, d_skip.reshape(1, D_MODEL), w_glu.astype(BF16),
      b_glu.reshape(1, 2 * D_MODEL), h0_re, h0_im,
      jnp.asarray(perm, BF16), jnp.asarray(perm.T, BF16))
    return y.reshape(bsz, t, D_MODEL), h_re, h_im


def kernel(x_prompt, x_sample, state_gla, state_s5_re, state_s5_im, norm_g, w_ffn_gu, w_ffn_down,
           gla_w_in, gla_w_g2, gla_b_g, gla_g_onorm, gla_w_out,
           s5_lam_re, s5_lam_im, s5_log_dt, s5_b_re, s5_b_im, s5_c_re, s5_c_im, s5_d, s5_w_glu,
           s5_b_glu):
    pb = x_prompt.shape[0]
    sb, st, _ = x_sample.shape

    def ffn(xp, xs, layer, which):
        g2 = norm_g[layer, 4 * which:4 * which + 2]
        xs, wgu_bf, wd_bf = _ffn_stream(xs, g2, w_ffn_gu, w_ffn_down, layer, which)
        return _ffn_resident(xp, g2, wgu_bf, wd_bf), xs

    gla_w = _gla_weights(gla_w_in[0], gla_w_g2[0], gla_b_g[0], gla_g_onorm[0], gla_w_out[0])
    xp, xs = ffn(x_prompt, x_sample, 0, 0)
    xp, gla_p = _gla_prompt(xp, norm_g[0, 2:4], gla_w)
    xs, gla_s = _gla_sample(xs, state_gla.reshape(sb, GLA_HEADS, GLA_DK, GLA_DV), norm_g[0, 2:4], gla_w)
    xp, xs = ffn(xp, xs, 0, 1)

    xp, xs = ffn(xp, xs, 1, 0)
    prep = _s5_prep(s5_lam_re[0], s5_lam_im[0], s5_log_dt[0], s5_b_re[0], s5_b_im[0],
                    s5_c_re[0], s5_c_im[0])
    zeros = jnp.zeros((pb, S5_FLAT), F32)
    s5_args = (norm_g[1, 2:4], prep, s5_d[0], s5_w_glu[0], s5_b_glu[0])
    xp, hre_p, him_p = _s5(xp, S5_STEPS, *s5_args, zeros, zeros)
    xs, hre_s, him_s = _s5(xs, st, *s5_args,
                           state_s5_re.reshape(sb, S5_FLAT), state_s5_im.reshape(sb, S5_FLAT))
    y_prompt, y_sample = ffn(xp, xs, 1, 1)

    def s5_state(h, b):
        return h.reshape(1, b, S5_GROUPS, S5_STATE)

    return (y_prompt, y_sample, gla_p.reshape(1, pb, GLA_HEADS, GLA_DK, GLA_DV),
            s5_state(hre_p, pb), s5_state(him_p, pb),
            gla_s.reshape(1, sb, GLA_HEADS, GLA_DK, GLA_DV),
            s5_state(hre_s, sb), s5_state(him_s, sb))
```

```python
import functools
import math

import jax
import jax.numpy as jnp
import numpy as np
from jax import lax
from jax.experimental import pallas as pl
from jax.experimental.pallas import tpu as pltpu

F32 = jnp.float32
BF16 = jnp.bfloat16

D_MODEL = 1024
D_FF = 2816
GLA_HEADS = 4
GLA_DK = 128
GLA_DV = 256
GLA_KEY_DIM = GLA_HEADS * GLA_DK
GLA_VAL_DIM = GLA_HEADS * GLA_DV
GATE_RANK = 16
GLA_MAIN_DIM = 2 * GLA_KEY_DIM + 2 * GLA_VAL_DIM
GLA_IN_DIM = GLA_MAIN_DIM + GATE_RANK
GATE_TAU = 16.0
S5_GROUP = 16
S5_GROUPS = 64
S5_STATE = 64
S5_FLAT = S5_GROUPS * S5_STATE
S5_SUPER = 8
S5_NSUPER = S5_GROUPS // S5_SUPER
S5_SUPER_CH = S5_SUPER * S5_GROUP
S5_SUPER_ST = S5_SUPER * S5_STATE
EPS = 1e-6
LANE = 128
SUBLANE = 8
VMEM_LIMIT = 56 * 1024 * 1024

FFN_ROWS = 1024
FFN_GROUP = 512
FFN_CHUNK = 256
FFN_SLOTS = 4
GLA_CHUNK = 64
GLA_TBLOCK = 1024
GLA_PROJ_CHUNK = 256
GLA_CUM_ROWS = 256
LOG2E = math.log2(math.e)
GLA_SAMPLE_SEQS = 16
S5_LANES = 512
S5_STEPS = 32
S5_GROUPS_PER_STEP = 4
S5_REGIONS = 2


def _dot(a, b):
    return jnp.dot(a, b, preferred_element_type=F32)


def _dot_nt(a, b):
    return lax.dot_general(a, b, (((1,), (1,)), ((), ())), preferred_element_type=F32)


def _rms(x, g):
    ms = jnp.mean(x * x, axis=-1, keepdims=True)
    return x * lax.rsqrt(ms + EPS) * g


def _sigmoid(x):
    return 1.0 / (1.0 + jnp.exp(-x))


def _split_bf16(x):
    hi = x.astype(BF16)
    lo = (x - hi.astype(F32)).astype(BF16)
    return hi, lo


def _ffn_row_groups(n_rows):
    return [slice(r0, min(r0 + FFN_GROUP, n_rows)) for r0 in range(0, n_rows, FFN_GROUP)]


def _ffn_hidden_chunk(xn_ref, wgu_ref, act_ref, rows, c):
    lo = c * FFN_CHUNK
    gate = _dot(xn_ref[rows, :], wgu_ref[:, lo:lo + FFN_CHUNK])
    up = _dot(xn_ref[rows, :], wgu_ref[:, D_FF + lo:D_FF + lo + FFN_CHUNK])
    act_ref[rows, lo:lo + FFN_CHUNK] = (gate * _sigmoid(gate) * up).astype(BF16)


def _ffn_finish(x_ref, g_ref, wd_ref, o_ref, act_ref, rows):
    y = _dot(act_ref[rows, :], wd_ref[...])
    o_ref[rows, :] = x_ref[rows, :] + 0.5 * _rms(y, g_ref[1:2, :])


def _ffn_stream_kernel(x_ref, g_ref, wgu_hbm, wd_hbm, o_ref, wgu_ref, wd_ref,
                       xn_ref, act_ref, gu_stage, d_stage, sem, *, layer, which):
    n_chunks = D_FF // FFN_CHUNK
    groups = _ffn_row_groups(x_ref.shape[0])
    xn_ref[...] = _rms(x_ref[...], g_ref[0:1, :]).astype(BF16)

    def chunk_slices(c):
        lo = c * FFN_CHUNK
        return ((slice(None), pl.ds(lo, FFN_CHUNK)), (slice(None), pl.ds(D_FF + lo, FFN_CHUNK)),
                (pl.ds(lo, FFN_CHUNK), slice(None)))

    def fetches(c):
        slot = c % FFN_SLOTS
        gate, up, down = chunk_slices(c)
        return (
            pltpu.make_async_copy(wgu_hbm.at[(layer, which) + gate], gu_stage.at[slot, 0], sem.at[slot, 0]),
            pltpu.make_async_copy(wgu_hbm.at[(layer, which) + up], gu_stage.at[slot, 1], sem.at[slot, 1]),
            pltpu.make_async_copy(wd_hbm.at[(layer, which) + down], d_stage.at[slot], sem.at[slot, 2]),
        )

    ahead = FFN_SLOTS - 1
    for c in range(min(ahead, n_chunks)):
        for cp in fetches(c):
            cp.start()
    for c in range(n_chunks):
        slot, lo = c % FFN_SLOTS, c * FFN_CHUNK
        for cp in fetches(c):
            cp.wait()
        if c + ahead < n_chunks:
            for cp in fetches(c + ahead):
                cp.start()
        wgu_ref[:, lo:lo + FFN_CHUNK] = gu_stage[slot, 0].astype(BF16)
        wgu_ref[:, D_FF + lo:D_FF + lo + FFN_CHUNK] = gu_stage[slot, 1].astype(BF16)
        wd_ref[lo:lo + FFN_CHUNK, :] = d_stage[slot].astype(BF16)
        for rows in groups:
            _ffn_hidden_chunk(xn_ref, wgu_ref, act_ref, rows, c)
    for rows in groups:
        _ffn_finish(x_ref, g_ref, wd_ref, o_ref, act_ref, rows)


def _ffn_resident_kernel(x_ref, g_ref, wgu_ref, wd_ref, o_ref, xn_ref, act_ref):
    xn_ref[...] = _rms(x_ref[...], g_ref[0:1, :]).astype(BF16)
    for rows in _ffn_row_groups(x_ref.shape[0]):
        for c in range(D_FF // FFN_CHUNK):
            _ffn_hidden_chunk(xn_ref, wgu_ref, act_ref, rows, c)
        _ffn_finish(x_ref, g_ref, wd_ref, o_ref, act_ref, rows)


def _ffn_stream(x, g2, wgu, wd, layer, which):
    shape = x.shape
    x = x.reshape(-1, D_MODEL)
    n = x.shape[0]
    zero = lambda i: (0, 0)
    y, wgu_bf, wd_bf = pl.pallas_call(
        functools.partial(_ffn_stream_kernel, layer=layer, which=which),
        grid=(1,),
        in_specs=[
            pl.BlockSpec((n, D_MODEL), zero),
            pl.BlockSpec((2, D_MODEL), zero),
            pl.BlockSpec(memory_space=pl.ANY),
            pl.BlockSpec(memory_space=pl.ANY),
        ],
        out_specs=[
            pl.BlockSpec((n, D_MODEL), zero),
            pl.BlockSpec((D_MODEL, 2 * D_FF), zero, pipeline_mode=pl.Buffered(1)),
            pl.BlockSpec((D_FF, D_MODEL), zero, pipeline_mode=pl.Buffered(1)),
        ],
        out_shape=[
            jax.ShapeDtypeStruct((n, D_MODEL), F32),
            jax.ShapeDtypeStruct((D_MODEL, 2 * D_FF), BF16),
            jax.ShapeDtypeStruct((D_FF, D_MODEL), BF16),
        ],
        scratch_shapes=[
            pltpu.VMEM((n, D_MODEL), BF16),
            pltpu.VMEM((n, D_FF), BF16),
            pltpu.VMEM((FFN_SLOTS, 2, D_MODEL, FFN_CHUNK), F32),
            pltpu.VMEM((FFN_SLOTS, FFN_CHUNK, D_MODEL), F32),
            pltpu.SemaphoreType.DMA((FFN_SLOTS, 3)),
        ],
        compiler_params=pltpu.CompilerParams(
            dimension_semantics=("arbitrary",), vmem_limit_bytes=VMEM_LIMIT),
        name="ffn_stream",
    )(x, g2, wgu, wd)
    return y.reshape(shape), wgu_bf, wd_bf


def _ffn_resident(x, g2, wgu_bf, wd_bf):
    shape = x.shape
    x = x.reshape(-1, D_MODEL)
    n = x.shape[0]
    tm = min(FFN_ROWS, n)
    assert n % tm == 0
    zero = lambda i: (0, 0)
    resident = dict(pipeline_mode=pl.Buffered(1))
    return pl.pallas_call(
        _ffn_resident_kernel,
        grid=(n // tm,),
        in_specs=[
            pl.BlockSpec((tm, D_MODEL), lambda i: (i, 0)),
            pl.BlockSpec((2, D_MODEL), zero),
            pl.BlockSpec((D_MODEL, 2 * D_FF), zero, **resident),
            pl.BlockSpec((D_FF, D_MODEL), zero, **resident),
        ],
        out_specs=pl.BlockSpec((tm, D_MODEL), lambda i: (i, 0)),
        out_shape=jax.ShapeDtypeStruct((n, D_MODEL), F32),
        scratch_shapes=[pltpu.VMEM((tm, D_MODEL), BF16), pltpu.VMEM((tm, D_FF), BF16)],
        compiler_params=pltpu.CompilerParams(
            dimension_semantics=("arbitrary",), vmem_limit_bytes=VMEM_LIMIT),
        name="ffn",
    )(x, g2, wgu_bf, wd_bf).reshape(shape)


def _gla_tables(rows, seq):
    t = np.arange(rows)[:, None]
    r = np.arange(rows)[None, :]
    cum = ((t // seq) == (r // seq)) & (r <= t)
    masks = [t == r]
    m = seq // 2
    while m >= 1:
        masks.append(((t // (2 * m)) == (r // (2 * m))) & (t % (2 * m) >= m) & (r % (2 * m) < m))
        m //= 2
    return jnp.asarray(cum.astype(np.float32), BF16), jnp.asarray(np.stack(masks).astype(np.float32))


def _gla_project_cols(hn_ref, win_ref, proj_ref, c0):
    proj_ref[:, c0:c0 + GLA_PROJ_CHUNK] = _dot(hn_ref[...], win_ref[:, c0:c0 + GLA_PROJ_CHUNK])


def _gla_project(x, g_ref, win_ref, wglr_ref, wg2_ref, bg_ref, cum_ref, hn_ref, proj_ref, lf_ref, b_ref):
    hn = _rms(x, g_ref[0:1, :]).astype(BF16)
    hn_ref[...] = hn
    glr = _dot(hn, wglr_ref[...]).astype(BF16)
    gate = _dot(glr, wg2_ref[...]) + bg_ref[...]
    for c0 in range(0, 2 * GLA_KEY_DIM, GLA_PROJ_CHUNK):
        _gla_project_cols(hn_ref, win_ref, proj_ref, c0)
    lf = (jnp.minimum(gate, 0.0) - jnp.log1p(jnp.exp(-jnp.abs(gate)))) * (1.0 / GATE_TAU)
    lf_ref[...] = lf
    hi, lo = _split_bf16(lf)
    span = cum_ref.shape[0]
    for r0 in range(0, lf.shape[0], span):
        b_ref[r0:r0 + span, :] = (_dot(cum_ref[...], hi[r0:r0 + span]) + _dot(cum_ref[...], lo[r0:r0 + span]))


def _block_rows(ref, r0, col, rows, period, offset):
    def bc(row, n):
        tile = ref[pl.ds(r0 + row // SUBLANE * SUBLANE, SUBLANE), col]
        return jnp.broadcast_to(tile[row % SUBLANE:row % SUBLANE + 1, :], (n, GLA_DK))

    if period >= SUBLANE:
        parts = [bc(p0 + offset, period) for p0 in range(0, rows, period)]
    else:
        assert 2 * period == SUBLANE
        low = lax.broadcasted_iota(jnp.int32, (SUBLANE, 1), 0) < period
        parts = [jnp.where(low, bc(t0 + offset, SUBLANE), bc(t0 + period + offset, SUBLANE))
                 for t0 in range(0, rows, SUBLANE)]
    return parts[0] if len(parts) == 1 else jnp.concatenate(parts, axis=0)


def _gla_attention(qs, k, b, lf, b_ref, r0, col, rows, seq, mask_ref):
    qb, kb = qs.astype(BF16), k.astype(BF16)
    att = _dot_nt(qb, kb) * mask_ref[0]
    lvl, m = 1, seq // 2
    while m >= 1:
        if m > 1:
            d = b - _block_rows(b_ref, r0, col, rows, 2 * m, m - 1)
            w = jnp.exp2(jnp.abs(d) * (-LOG2E))
        else:
            odd = lax.broadcasted_iota(jnp.int32, (rows, 1), 0) % 2 == 1
            w = jnp.where(odd, jnp.exp(lf), 1.0)
        wb = w.astype(BF16)
        att = att + _dot_nt(qb * wb, kb * wb) * mask_ref[lvl]
        lvl, m = lvl + 1, m // 2
    return att


def _gla_finish_head(o, r, gon_ref):
    on = _rms(o, gon_ref[...])
    return (on * (r * _sigmoid(r))).astype(BF16)


def _gla_prompt_kernel(x_ref, g_ref, win_ref, wglr_ref, wg2_ref, bg_ref, gon_ref, wout_ref,
                       cum_ref, mask_ref, o_ref, snew_ref,
                       hn_ref, proj_ref, lf_ref, b_ref, og_ref, s_ref, att_ref, qb_ref, tt_ref):
    tb = pl.program_id(1)
    c = GLA_CHUNK

    @pl.when(tb == 0)
    def _():
        s_ref[...] = jnp.zeros_like(s_ref)

    x = x_ref[0]
    _gla_project(x, g_ref, win_ref, wglr_ref, wg2_ref, bg_ref, cum_ref, hn_ref, proj_ref, lf_ref, b_ref)

    late_cols = list(range(2 * GLA_KEY_DIM, proj_ref.shape[1], GLA_PROJ_CHUNK))
    n_heads_total = GLA_TBLOCK // c * GLA_HEADS
    for ci, r0 in enumerate(range(0, GLA_TBLOCK, c)):
        rows = slice(r0, r0 + c)
        for h in range(GLA_HEADS):
            done = ci * GLA_HEADS + h
            for c0 in late_cols[done * len(late_cols) // n_heads_total:
                                (done + 1) * len(late_cols) // n_heads_total]:
                _gla_project_cols(hn_ref, win_ref, proj_ref, c0)
            kcol = slice(h * GLA_DK, (h + 1) * GLA_DK)
            qs = proj_ref[rows, kcol] * (GLA_DK ** -0.5)
            k = proj_ref[rows, GLA_KEY_DIM + h * GLA_DK:GLA_KEY_DIM + (h + 1) * GLA_DK]
            b = b_ref[rows, kcol]
            att = _gla_attention(qs, k, b, lf_ref[rows, kcol], b_ref, r0, kcol, c, c, mask_ref)
            att_ref[ci * GLA_HEADS + h] = att.astype(BF16)
            qb_ref[rows, kcol] = (qs * jnp.exp(b)).astype(BF16)
            khat = k * jnp.exp(_block_rows(b_ref, r0, kcol, c, c, c - 1) - b)
            tile = jnp.concatenate(
                [khat, jnp.exp(b[c - SUBLANE:c, :]), jnp.zeros((LANE - c - SUBLANE, GLA_DK), F32)], axis=0)
            tt_ref[ci * GLA_HEADS + h] = jnp.transpose(tile)

    for ci, r0 in enumerate(range(0, GLA_TBLOCK, c)):
        rows = slice(r0, r0 + c)
        for h in range(GLA_HEADS):
            kcol = slice(h * GLA_DK, (h + 1) * GLA_DK)
            vcol = slice(h * GLA_DV, (h + 1) * GLA_DV)
            v = proj_ref[rows, 2 * GLA_KEY_DIM + h * GLA_DV:2 * GLA_KEY_DIM + (h + 1) * GLA_DV].astype(BF16)
            r = proj_ref[rows, 2 * GLA_KEY_DIM + GLA_VAL_DIM + h * GLA_DV:
                         2 * GLA_KEY_DIM + GLA_VAL_DIM + (h + 1) * GLA_DV]
            s = s_ref[h]
            tt = tt_ref[ci * GLA_HEADS + h]
            o = _dot(att_ref[ci * GLA_HEADS + h], v) + _dot(qb_ref[rows, kcol], s.astype(BF16))
            decay = tt[:, c + SUBLANE - 1:c + SUBLANE]
            s_ref[h] = s * decay + _dot(tt[:, 0:c].astype(BF16), v)
            og_ref[rows, vcol] = _gla_finish_head(o, r, gon_ref)
    out = _dot(og_ref[...], wout_ref[...])
    o_ref[0] = x + _rms(out, g_ref[1:2, :])

    @pl.when(tb == pl.num_programs(1) - 1)
    def _():
        snew_ref[0] = s_ref[...]


def _gla_sample_kernel(x_ref, g_ref, win_ref, wglr_ref, wg2_ref, bg_ref, gon_ref, wout_ref,
                       cum_ref, mask_ref, s0_ref, o_ref, snew_ref,
                       hn_ref, proj_ref, lf_ref, b_ref, og_ref, *, seq):
    rows = x_ref.shape[0]
    grp = 2 * SUBLANE
    x = x_ref[...]
    _gla_project(x, g_ref, win_ref, wglr_ref, wg2_ref, bg_ref, cum_ref, hn_ref, proj_ref, lf_ref, b_ref)
    for c0 in range(2 * GLA_KEY_DIM, proj_ref.shape[1], GLA_PROJ_CHUNK):
        _gla_project_cols(hn_ref, win_ref, proj_ref, c0)
    rid = lax.broadcasted_iota(jnp.int32, (grp, 1), 0)
    for h in range(GLA_HEADS):
        kcol = slice(h * GLA_DK, (h + 1) * GLA_DK)
        vcol = slice(h * GLA_DV, (h + 1) * GLA_DV)
        qs = proj_ref[:, kcol] * (GLA_DK ** -0.5)
        k = proj_ref[:, GLA_KEY_DIM + h * GLA_DK:GLA_KEY_DIM + (h + 1) * GLA_DK]
        v = proj_ref[:, 2 * GLA_KEY_DIM + h * GLA_DV:2 * GLA_KEY_DIM + (h + 1) * GLA_DV]
        r = proj_ref[:, 2 * GLA_KEY_DIM + GLA_VAL_DIM + h * GLA_DV:
                     2 * GLA_KEY_DIM + GLA_VAL_DIM + (h + 1) * GLA_DV]
        b = b_ref[:, kcol]
        att = _gla_attention(qs, k, b, lf_ref[:, kcol], b_ref, 0, kcol, rows, seq, mask_ref)
        eb = jnp.exp(b)
        qb = (qs * eb).astype(BF16)
        khat = k * jnp.exp(_block_rows(b_ref, 0, kcol, rows, seq, seq - 1) - b)
        o_intra = _dot(att.astype(BF16), v.astype(BF16))
        o_parts = []
        for gi in range(rows // grp):
            gr = slice(gi * grp, (gi + 1) * grp)
            tile = jnp.concatenate(
                [khat[gr], eb[gr], jnp.zeros((LANE - 2 * grp, GLA_DK), F32)], axis=0)
            tt = jnp.transpose(tile)
            kt = tt[:, 0:grp].astype(BF16)
            qg = qb[gr]
            vg = v[gr]
            o_g = jnp.zeros((grp, GLA_DV), F32)
            for si in range(grp // seq):
                sq = gi * (grp // seq) + si
                mine = (rid >= si * seq) & (rid < (si + 1) * seq)
                s = s0_ref[sq, h]
                o_g = o_g + jnp.where(mine, _dot(qg, s.astype(BF16)), 0.0)
                vm = jnp.where(mine, vg, 0.0).astype(BF16)
                last = grp + (si + 1) * seq - 1
                snew_ref[sq, h] = s * tt[:, last:last + 1] + _dot(kt, vm)
            o_parts.append(o_g)
        o = o_intra + jnp.concatenate(o_parts, axis=0)
        og_ref[:, vcol] = _gla_finish_head(o, r, gon_ref)
    out = _dot(og_ref[...], wout_ref[...])
    o_ref[...] = x + _rms(out, g_ref[1:2, :])


def _gla_weight_specs():
    zero = (lambda *_: (0, 0))
    resident = dict(pipeline_mode=pl.Buffered(1))
    return [
        pl.BlockSpec((2, D_MODEL), zero),
        pl.BlockSpec((D_MODEL, GLA_IN_DIM), zero, **resident),
        pl.BlockSpec((D_MODEL, LANE), zero, **resident),
        pl.BlockSpec((LANE, GLA_KEY_DIM), zero, **resident),
        pl.BlockSpec((1, GLA_KEY_DIM), zero),
        pl.BlockSpec((1, GLA_DV), zero),
        pl.BlockSpec((GLA_VAL_DIM, D_MODEL), zero, **resident),
    ]


def _gla_weights(w_in, w_g2, b_g, g_onorm, w_out):
    w_glr = jnp.pad(w_in[:, GLA_MAIN_DIM:], ((0, 0), (0, LANE - GATE_RANK))).astype(BF16)
    w_g2p = jnp.pad(w_g2, ((0, LANE - GATE_RANK), (0, 0))).astype(BF16)
    return (w_in.astype(BF16), w_glr, w_g2p, b_g.reshape(1, GLA_KEY_DIM),
            g_onorm.reshape(1, GLA_DV), w_out.astype(BF16))


def _gla_prompt(x, g2, weights):
    bsz, t, _ = x.shape
    cum, _ = _gla_tables(GLA_CUM_ROWS, GLA_CHUNK)
    _, masks = _gla_tables(GLA_CHUNK, GLA_CHUNK)
    n_main = GLA_MAIN_DIM
    const2 = lambda b, i: (0, 0)
    return pl.pallas_call(
        _gla_prompt_kernel,
        grid=(bsz, t // GLA_TBLOCK),
        in_specs=[pl.BlockSpec((1, GLA_TBLOCK, D_MODEL), lambda b, i: (b, i, 0))]
        + _gla_weight_specs()
        + [pl.BlockSpec(cum.shape, const2),
           pl.BlockSpec(masks.shape, lambda b, i: (0, 0, 0))],
        out_specs=[
            pl.BlockSpec((1, GLA_TBLOCK, D_MODEL), lambda b, i: (b, i, 0)),
            pl.BlockSpec((1, GLA_HEADS, GLA_DK, GLA_DV), lambda b, i: (b, 0, 0, 0)),
        ],
        out_shape=[
            jax.ShapeDtypeStruct(x.shape, F32),
            jax.ShapeDtypeStruct((bsz, GLA_HEADS, GLA_DK, GLA_DV), F32),
        ],
        scratch_shapes=[
            pltpu.VMEM((GLA_TBLOCK, D_MODEL), BF16),
            pltpu.VMEM((GLA_TBLOCK, n_main), F32),
            pltpu.VMEM((GLA_TBLOCK, GLA_KEY_DIM), F32),
            pltpu.VMEM((GLA_TBLOCK, GLA_KEY_DIM), F32),
            pltpu.VMEM((GLA_TBLOCK, GLA_VAL_DIM), BF16),
            pltpu.VMEM((GLA_HEADS, GLA_DK, GLA_DV), F32),
            pltpu.VMEM((GLA_TBLOCK // GLA_CHUNK * GLA_HEADS, GLA_CHUNK, GLA_CHUNK), BF16),
            pltpu.VMEM((GLA_TBLOCK, GLA_KEY_DIM), BF16),
            pltpu.VMEM((GLA_TBLOCK // GLA_CHUNK * GLA_HEADS, LANE, GLA_DK), F32),
        ],
        compiler_params=pltpu.CompilerParams(
            dimension_semantics=("arbitrary", "arbitrary"), vmem_limit_bytes=VMEM_LIMIT),
        name="gla_prompt",
    )(x, g2, *weights, cum, masks)


def _gla_sample(x, s0, g2, weights):
    bsz, seq, _ = x.shape
    rows = GLA_SAMPLE_SEQS * seq
    cum, masks = _gla_tables(rows, seq)
    n_main = GLA_MAIN_DIM
    state_spec = pl.BlockSpec((GLA_SAMPLE_SEQS, GLA_HEADS, GLA_DK, GLA_DV), lambda i: (i, 0, 0, 0))
    y, snew = pl.pallas_call(
        functools.partial(_gla_sample_kernel, seq=seq),
        grid=(bsz // GLA_SAMPLE_SEQS,),
        in_specs=[pl.BlockSpec((rows, D_MODEL), lambda i: (i, 0))]
        + _gla_weight_specs()
        + [pl.BlockSpec(cum.shape, lambda i: (0, 0)),
           pl.BlockSpec(masks.shape, lambda i: (0, 0, 0)),
           state_spec],
        out_specs=[pl.BlockSpec((rows, D_MODEL), lambda i: (i, 0)), state_spec],
        out_shape=[
            jax.ShapeDtypeStruct((bsz * seq, D_MODEL), F32),
            jax.ShapeDtypeStruct(s0.shape, F32),
        ],
        scratch_shapes=[
            pltpu.VMEM((rows, D_MODEL), BF16),
            pltpu.VMEM((rows, n_main), F32),
            pltpu.VMEM((rows, GLA_KEY_DIM), F32),
            pltpu.VMEM((rows, GLA_KEY_DIM), F32),
            pltpu.VMEM((rows, GLA_VAL_DIM), BF16),
        ],
        compiler_params=pltpu.CompilerParams(
            dimension_semantics=("arbitrary",), vmem_limit_bytes=VMEM_LIMIT),
        name="gla_sample",
    )(x.reshape(bsz * seq, D_MODEL), g2, *weights, cum, masks, s0)
    return y.reshape(x.shape), snew


def _s5_discretize(lam_re, lam_im, log_dt):
    dt = jnp.exp(log_dt)
    mag = jnp.exp(lam_re * dt)
    ang = lam_im * dt
    a_re, a_im = mag * jnp.cos(ang), mag * jnp.sin(ang)
    nr, ni = a_re - 1.0, a_im
    den = lam_re * lam_re + lam_im * lam_im
    f_re = (nr * lam_re + ni * lam_im) / den
    f_im = (ni * lam_re - nr * lam_im) / den
    return a_re, a_im, f_re, f_im


def _s5_prep_kernel(lre_ref, lim_ref, ldt_ref, bre_ref, bim_ref, cre_ref, cim_ref,
                    lref_ref, limf_ref, ldtf_ref, wb_ref, wc_ref, are_ref, aim_ref):
    _, _, f_re, f_im = _s5_discretize(lre_ref[...], lim_ref[...], ldt_ref[...])

    def per_channel(f):
        return jnp.broadcast_to(f[:, None, :], (S5_SUPER, S5_GROUP, S5_SUPER_ST)).reshape(
            S5_SUPER_CH, S5_SUPER_ST)

    f_re, f_im = per_channel(f_re), per_channel(f_im)
    b_re, b_im = bre_ref[...], bim_ref[...]
    row = lax.broadcasted_iota(jnp.int32, (S5_SUPER_CH, S5_SUPER_ST), 0)
    col = lax.broadcasted_iota(jnp.int32, (S5_SUPER_CH, S5_SUPER_ST), 1)
    own = (row // S5_GROUP) == (col // S5_STATE)
    wb_ref[0, :, 0:S5_SUPER_ST] = jnp.where(own, f_re * b_re - f_im * b_im, 0.0).astype(BF16)
    wb_ref[0, :, S5_SUPER_ST:] = jnp.where(own, f_re * b_im + f_im * b_re, 0.0).astype(BF16)
    rowc = lax.broadcasted_iota(jnp.int32, (S5_SUPER_ST, S5_SUPER_CH), 0)
    colc = lax.broadcasted_iota(jnp.int32, (S5_SUPER_ST, S5_SUPER_CH), 1)
    ownc = (rowc // S5_STATE) == (colc // S5_GROUP)
    wc_ref[0, 0:S5_SUPER_ST, :] = jnp.where(ownc, cre_ref[...], 0.0).astype(BF16)
    wc_ref[0, S5_SUPER_ST:, :] = jnp.where(ownc, -cim_ref[...], 0.0).astype(BF16)
    a_re, a_im, _, _ = _s5_discretize(lref_ref[...], limf_ref[...], ldtf_ref[...])
    are_ref[...] = a_re
    aim_ref[...] = a_im


def _s5_prep(lam_re, lam_im, log_dt, b_re, b_im, c_re, c_im):
    g, p, ch, sg = S5_GROUPS, S5_STATE, S5_GROUP, S5_SUPER

    def rows_by_group(a_gp):
        return jnp.broadcast_to(a_gp[:, None, :], (g, sg, p)).reshape(g, sg * p)

    def b_layout(b):
        bt = jnp.transpose(b, (0, 2, 1))
        return jnp.broadcast_to(bt[:, :, None, :], (g, ch, sg, p)).reshape(g * ch, sg * p)

    def c_layout(cm):
        ct = jnp.transpose(cm.reshape(g // sg, sg, ch, p), (3, 0, 1, 2)).reshape(p, g * ch)
        return jnp.broadcast_to(ct[None], (sg, p, g * ch)).reshape(sg * p, g * ch)

    ldt_gp = jnp.broadcast_to(log_dt[:, None], (g, p))
    big = pl.BlockSpec((S5_SUPER_CH, S5_SUPER_ST), lambda j: (j, 0))
    per_group = pl.BlockSpec((S5_SUPER, S5_SUPER_ST), lambda j: (j, 0))
    cspec = pl.BlockSpec((S5_SUPER_ST, S5_SUPER_CH), lambda j: (0, j))
    flat = pl.BlockSpec((1, S5_SUPER_ST), lambda j: (0, j))
    return pl.pallas_call(
        _s5_prep_kernel,
        grid=(S5_NSUPER,),
        in_specs=[per_group] * 3 + [big] * 2 + [cspec] * 2 + [flat] * 3,
        out_specs=[
            pl.BlockSpec((1, S5_SUPER_CH, 2 * S5_SUPER_ST), lambda j: (j, 0, 0)),
            pl.BlockSpec((1, 2 * S5_SUPER_ST, S5_SUPER_CH), lambda j: (j, 0, 0)),
            flat, flat,
        ],
        out_shape=[
            jax.ShapeDtypeStruct((S5_NSUPER, S5_SUPER_CH, 2 * S5_SUPER_ST), BF16),
            jax.ShapeDtypeStruct((S5_NSUPER, 2 * S5_SUPER_ST, S5_SUPER_CH), BF16),
            jax.ShapeDtypeStruct((1, S5_FLAT), F32),
            jax.ShapeDtypeStruct((1, S5_FLAT), F32),
        ],
        compiler_params=pltpu.CompilerParams(dimension_semantics=("arbitrary",)),
        name="s5_prep",
    )(rows_by_group(lam_re), rows_by_group(lam_im), rows_by_group(ldt_gp),
      b_layout(b_re), b_layout(b_im), c_layout(c_re), c_layout(c_im),
      lam_re.reshape(1, S5_FLAT), lam_im.reshape(1, S5_FLAT), ldt_gp.reshape(1, S5_FLAT))


def _s5_kernel(x_ref, g_ref, wb_ref, wc_ref, are_ref, aim_ref, d_ref, wglu_ref, bglu_ref,
               h0re_ref, h0im_ref, perm_ref, unperm_ref, o_ref, hre_ref, him_ref,
               u_ref, sre_ref, sim_ref, z_ref, *, bsz, steps, sub):
    i = pl.program_id(0)
    rows = bsz * steps

    @pl.when(i == 0)
    def _():
        hre_ref[...] = h0re_ref[...]
        him_ref[...] = h0im_ref[...]

    def x_group(k):
        xk = x_ref[...] if sub == 1 else x_ref[:, k * steps:(k + 1) * steps, :]
        return xk.reshape(rows, D_MODEL)

    def region(k):
        r0 = (k % S5_REGIONS) * rows
        return slice(r0, r0 + rows)

    def input_side(k):
        grp = region(k)
        u = _dot(perm_ref[...], _rms(x_group(k), g_ref[0:1, :]).astype(BF16))
        u_ref[grp, :] = u
        ub = u.astype(BF16)
        for j in range(S5_NSUPER):
            bu = _dot(ub[:, j * S5_SUPER_CH:(j + 1) * S5_SUPER_CH], wb_ref[j])
            sre_ref[grp, j * S5_SUPER_ST:(j + 1) * S5_SUPER_ST] = bu[:, :S5_SUPER_ST]
            sim_ref[grp, j * S5_SUPER_ST:(j + 1) * S5_SUPER_ST] = bu[:, S5_SUPER_ST:]

    def scan_group(k):
        base = (k % S5_REGIONS) * rows
        for lc in range(S5_FLAT // S5_LANES):
            ls = slice(lc * S5_LANES, (lc + 1) * S5_LANES)
            a_re = jnp.broadcast_to(are_ref[:, ls], (SUBLANE, S5_LANES))
            a_im = jnp.broadcast_to(aim_ref[:, ls], (SUBLANE, S5_LANES))
            for rt in range(bsz // SUBLANE):
                rs = slice(rt * SUBLANE, (rt + 1) * SUBLANE)

                def step(t, carry, ls=ls, rt=rt, a_re=a_re, a_im=a_im):
                    h_re, h_im = carry
                    r8 = pl.ds(pl.multiple_of(base + t * bsz + rt * SUBLANE, SUBLANE), SUBLANE)
                    n_re = a_re * h_re - a_im * h_im + sre_ref[r8, ls]
                    n_im = a_re * h_im + a_im * h_re + sim_ref[r8, ls]
                    sre_ref[r8, ls] = n_re
                    sim_ref[r8, ls] = n_im
                    return n_re, n_im

                h_re, h_im = lax.fori_loop(0, steps, step, (hre_ref[rs, ls], him_ref[rs, ls]),
                                           unroll=True)
                hre_ref[rs, ls] = h_re
                him_ref[rs, ls] = h_im

    for k in range(min(S5_REGIONS, sub)):
        input_side(k)
    for k in range(sub):
        scan_group(k)
        grp = region(k)
        for j in range(S5_NSUPER):
            st = slice(j * S5_SUPER_ST, (j + 1) * S5_SUPER_ST)
            ch = slice(j * S5_SUPER_CH, (j + 1) * S5_SUPER_CH)
            y = (_dot(sre_ref[grp, st].astype(BF16), wc_ref[j, 0:S5_SUPER_ST, :])
                 + _dot(sim_ref[grp, st].astype(BF16), wc_ref[j, S5_SUPER_ST:, :]))
            z_ref[grp, ch] = (y + d_ref[:, ch] * u_ref[grp, ch]).astype(BF16)
        z = _dot(unperm_ref[...], z_ref[grp, :]).astype(BF16)
        zz = _dot(z, wglu_ref[...]) + bglu_ref[...]
        out = zz[:, :D_MODEL] * _sigmoid(zz[:, D_MODEL:])
        res = x_group(k) + _rms(out, g_ref[1:2, :])
        if sub == 1:
            o_ref[...] = res.reshape(o_ref.shape)
        else:
            o_ref[:, k * steps:(k + 1) * steps, :] = res.reshape(bsz, steps, D_MODEL)
        if k + S5_REGIONS < sub:
            input_side(k + S5_REGIONS)


def _s5(x, steps_per_block, g2, prep, d_skip, w_glu, b_glu, h0_re, h0_im):
    wb, wc, a_re, a_im = prep
    bsz, t, _ = x.shape
    rows = steps_per_block * bsz
    r = np.arange(rows)
    perm = np.zeros((rows, rows), np.float32)
    perm[(r % steps_per_block) * bsz + r // steps_per_block, r] = 1.0
    zero2 = lambda i: (0, 0)
    zero3 = lambda i: (0, 0, 0)
    resident = dict(pipeline_mode=pl.Buffered(1))
    state = pl.BlockSpec((bsz, S5_FLAT), zero2)
    if steps_per_block == t:
        sub = 1
        x = x.reshape(1, rows, D_MODEL)
        xspec = pl.BlockSpec((1, rows, D_MODEL), lambda i: (0, 0, 0))
    else:
        sub = S5_GROUPS_PER_STEP
        assert steps_per_block % SUBLANE == 0 and t % (sub * steps_per_block) == 0
        xspec = pl.BlockSpec((bsz, sub * steps_per_block, D_MODEL), lambda i: (0, i, 0))
    y, h_re, h_im = pl.pallas_call(
        functools.partial(_s5_kernel, bsz=bsz, steps=steps_per_block, sub=sub),
        grid=(t // (sub * steps_per_block),),
        in_specs=[
            xspec,
            pl.BlockSpec((2, D_MODEL), zero2),
            pl.BlockSpec(wb.shape, zero3, **resident),
            pl.BlockSpec(wc.shape, zero3, **resident),
            pl.BlockSpec((1, S5_FLAT), zero2),
            pl.BlockSpec((1, S5_FLAT), zero2),
            pl.BlockSpec((1, D_MODEL), zero2),
            pl.BlockSpec((D_MODEL, 2 * D_MODEL), zero2, **resident),
            pl.BlockSpec((1, 2 * D_MODEL), zero2),
            state, state,
            pl.BlockSpec((rows, rows), zero2),
            pl.BlockSpec((rows, rows), zero2),
        ],
        out_specs=[xspec, state, state],
        out_shape=[
            jax.ShapeDtypeStruct(x.shape, F32),
            jax.ShapeDtypeStruct((bsz, S5_FLAT), F32),
            jax.ShapeDtypeStruct((bsz, S5_FLAT), F32),
        ],
        scratch_shapes=[
            pltpu.VMEM((min(sub, S5_REGIONS) * rows, D_MODEL), F32),
            pltpu.VMEM((min(sub, S5_REGIONS) * rows, S5_FLAT), F32),
            pltpu.VMEM((min(sub, S5_REGIONS) * rows, S5_FLAT), F32),
            pltpu.VMEM((min(sub, S5_REGIONS) * rows, D_MODEL), BF16),
        ],
        compiler_params=pltpu.CompilerParams(
            dimension_semantics=("arbitrary",), vmem_limit_bytes=VMEM_LIMIT),
        name="s5",
    )(x, g2, wb, wc, a_re, a_im, d_skip.reshape(1, D_MODEL), w_glu.astype(BF16),
      b_glu.reshape(1, 2 * D_MODEL), h0_re, h0_im,
      jnp.asarray(perm, BF16), jnp.asarray(perm.T, BF16))
    return y.reshape(bsz, t, D_MODEL), h_re, h_im


def kernel(x_prompt, x_sample, state_gla, state_s5_re, state_s5_im, norm_g, w_ffn_gu, w_ffn_down,
           gla_w_in, gla_w_g2, gla_b_g, gla_g_onorm, gla_w_out,
           s5_lam_re, s5_lam_im, s5_log_dt, s5_b_re, s5_b_im, s5_c_re, s5_c_im, s5_d, s5_w_glu,
           s5_b_glu):
    pb = x_prompt.shape[0]
    sb, st, _ = x_sample.shape

    def ffn(xp, xs, layer, which):
        g2 = norm_g[layer, 4 * which:4 * which + 2]
        xs, wgu_bf, wd_bf = _ffn_stream(xs, g2, w_ffn_gu, w_ffn_down, layer, which)
        return _ffn_resident(xp, g2, wgu_bf, wd_bf), xs

    gla_w = _gla_weights(gla_w_in[0], gla_w_g2[0], gla_b_g[0], gla_g_onorm[0], gla_w_out[0])
    xp, xs = ffn(x_prompt, x_sample, 0, 0)
    xp, gla_p = _gla_prompt(xp, norm_g[0, 2:4], gla_w)
    xs, gla_s = _gla_sample(xs, state_gla.reshape(sb, GLA_HEADS, GLA_DK, GLA_DV), norm_g[0, 2:4], gla_w)
    xp, xs = ffn(xp, xs, 0, 1)

    xp, xs = ffn(xp, xs, 1, 0)
    prep = _s5_prep(s5_lam_re[0], s5_lam_im[0], s5_log_dt[0], s5_b_re[0], s5_b_im[0],
                    s5_c_re[0], s5_c_im[0])
    zeros = jnp.zeros((pb, S5_FLAT), F32)
    s5_args = (norm_g[1, 2:4], prep, s5_d[0], s5_w_glu[0], s5_b_glu[0])
    xp, hre_p, him_p = _s5(xp, S5_STEPS, *s5_args, zeros, zeros)
    xs, hre_s, him_s = _s5(xs, st, *s5_args,
                           state_s5_re.reshape(sb, S5_FLAT), state_s5_im.reshape(sb, S5_FLAT))
    y_prompt, y_sample = ffn(xp, xs, 1, 1)

    def s5_state(h, b):
        return h.reshape(1, b, S5_GROUPS, S5_STATE)

    return (y_prompt, y_sample, gla_p.reshape(1, pb, GLA_HEADS, GLA_DK, GLA_DV),
            s5_state(hre_p, pb), s5_state(him_p, pb),
            gla_s.reshape(1, sb, GLA_HEADS, GLA_DK, GLA_DV),
            s5_state(hre_s, sb), s5_state(him_s, sb))
```
